```python
import math
import jax, jax.numpy as jnp
from jax import lax
import numpy as np

D_MODEL = 1024
BATCH = 8
SEQ = 2048
DEPTH = 1
DEC_BATCH = 128
DEC_SEQ = 8
PAST_LEN = 16384
PAGE_SIZE = 128

MIX_WIDTH = D_MODEL
RET_HEADS = 4
RET_DK = (MIX_WIDTH // 2) // RET_HEADS
RET_DV = RET_DK
RET_W = RET_HEADS * RET_DK
HG_HEADS = 4
HG_DK = (MIX_WIDTH - RET_W) // HG_HEADS
HG_DV = HG_DK
HG_W = HG_HEADS * HG_DK
IN_COLS = 4 * RET_W + 4 * HG_W
D_FF = 4 * D_MODEL
CHUNK = 64
ROPE_BASE = 10000.0
NORM_EPS = 1e-6

kernel_name = "retnet_hgrn2_parallel_heads_step"


def rmsnorm(x, w):
    xf = x.astype(jnp.float32)
    r = xf * lax.rsqrt(jnp.mean(xf * xf, axis=-1, keepdims=True) + NORM_EPS)
    return (r * w.astype(jnp.float32)).astype(x.dtype)


def head_rmsnorm(o, w):
    return o * lax.rsqrt(jnp.mean(o * o, axis=-1, keepdims=True) + NORM_EPS) * w.astype(jnp.float32)


def rotary(x, pos):
    half = x.shape[-1] // 2
    inv_freq = ROPE_BASE ** (-jnp.arange(half, dtype=jnp.float32) / half)
    ang = pos.astype(jnp.float32)[:, None] * inv_freq[None, :]
    cos = jnp.cos(ang)[None, :, None, :]
    sin = jnp.sin(ang)[None, :, None, :]
    xf = x.astype(jnp.float32)
    x1, x2 = xf[..., :half], xf[..., half:]
    return jnp.concatenate([x1 * cos - x2 * sin, x1 * sin + x2 * cos], axis=-1)


def chunk_gla(q, k, v, log_g, s0):
    B, T, H, dk = q.shape
    dv = v.shape[-1]
    C = math.gcd(T, CHUNK)
    n = T // C

    def to_chunks(a):
        return a.astype(jnp.float32).reshape(B, n, C, H, a.shape[-1]).transpose(1, 0, 3, 2, 4)

    qc, kc, vc, gc = to_chunks(q), to_chunks(k), to_chunks(v), to_chunks(log_g)
    causal = jnp.tril(jnp.ones((C, C), dtype=bool))
    scalar_decay = log_g.shape[-1] == 1

    def step(S, inp):
        qb, kb, vb, gb = inp
        b = jnp.cumsum(gb, axis=2)
        b_last = b[:, :, -1:, :]
        if scalar_decay:
            diff = b[:, :, :, None, 0] - b[:, :, None, :, 0]
            decay = jnp.exp(jnp.where(causal, diff, -jnp.inf))
            A = jnp.einsum('bhtd,bhsd->bhts', qb, kb) * decay
        else:
            diff = b[:, :, :, None, :] - b[:, :, None, :, :]
            decay = jnp.exp(jnp.where(causal[:, :, None], diff, -jnp.inf))
            A = jnp.einsum('bhtd,bhsd,bhtsd->bhts', qb, kb, decay)
        o = jnp.einsum('bhts,bhsv->bhtv', A, vb) + jnp.einsum('bhtd,bhdv->bhtv', qb * jnp.exp(b), S)
        k_dec = kb * jnp.exp(b_last - b)
        S_new = S * jnp.exp(b_last[:, :, 0, :])[..., None] + jnp.einsum('bhsd,bhsv->bhdv', k_dec, vb)
        return S_new, o

    S_fin, oc = lax.scan(step, s0.astype(jnp.float32), (qc, kc, vc, gc))
    o = oc.transpose(1, 0, 3, 2, 4).reshape(B, T, H, dv)
    return o, S_fin


def mixer(h, pos, s_ret, s_hg, w_in, ret_norm_w, hgrn_norm_w, lb, w_out):
    B, T, _ = h.shape
    proj = h @ w_in
    sizes = [RET_W] * 4 + [HG_W] * 4
    idx = [int(s) for s in np.cumsum(sizes)[:-1]]
    rq, rk, rv, rg, hq, hf, hi, hg = jnp.split(proj, idx, axis=-1)

    rq = rotary(rq.reshape(B, T, RET_HEADS, RET_DK), pos)
    rk = rotary(rk.reshape(B, T, RET_HEADS, RET_DK), pos) * (RET_DK ** -0.5)
    rv = rv.reshape(B, T, RET_HEADS, RET_DV)
    log_gamma = jnp.log(1.0 - 2.0 ** (-5.0 - jnp.arange(RET_HEADS, dtype=jnp.float32)))
    log_g_ret = jnp.broadcast_to(log_gamma[None, None, :, None], (B, T, RET_HEADS, 1))
    o_r, s_ret_new = chunk_gla(rq, rk, rv, log_g_ret, s_ret)
    gate_r = jax.nn.silu(rg.astype(jnp.float32)).reshape(B, T, RET_HEADS, RET_DV)
    o_r = (head_rmsnorm(o_r, ret_norm_w) * gate_r).reshape(B, T, RET_W)

    hq = jax.nn.silu(hq.astype(jnp.float32)).reshape(B, T, HG_HEADS, HG_DK) * (HG_DK ** -0.5)
    lb_h = lb.reshape(HG_HEADS, HG_DK)
    f = lb_h + (1.0 - lb_h) * jax.nn.sigmoid(hf.astype(jnp.float32).reshape(B, T, HG_HEADS, HG_DK))
    log_f = jnp.log(f)
    k_in = 1.0 - f
    v_in = hi.reshape(B, T, HG_HEADS, HG_DV)
    o_h, s_hg_new = chunk_gla(hq, k_in, v_in, log_f, s_hg)
    gate_h = jax.nn.silu(hg.astype(jnp.float32)).reshape(B, T, HG_HEADS, HG_DV)
    o_h = (head_rmsnorm(o_h, hgrn_norm_w) * gate_h).reshape(B, T, HG_W)

    o = jnp.concatenate([o_r, o_h], axis=-1).astype(h.dtype)
    return o @ w_out, s_ret_new, s_hg_new


def setup_inputs(seed: int = 0) -> dict:
    key = jax.random.key(seed)
    ks = jax.random.split(key, 16)
    f32 = jnp.float32
    nrm = lambda k, shape, s: jax.random.normal(k, shape, f32) * s
    return {
        "x_prompt": nrm(ks[0], (BATCH, SEQ, D_MODEL), 1.0),
        "x_sample": nrm(ks[1], (DEC_BATCH, DEC_SEQ, D_MODEL), 1.0),
        "state_ret": nrm(ks[2], (DEPTH, DEC_BATCH, RET_HEADS, RET_DK, RET_DV), 0.3),
        "state_hgrn": nrm(ks[3], (DEPTH, DEC_BATCH, HG_HEADS, HG_DK, HG_DV), 0.3),
        "norm_mix_w": 1.0 + nrm(ks[4], (DEPTH, D_MODEL), 0.02),
        "w_in": nrm(ks[5], (DEPTH, D_MODEL, IN_COLS), D_MODEL ** -0.5),
        "ret_norm_w": 1.0 + nrm(ks[6], (DEPTH, RET_DV), 0.02),
        "hgrn_norm_w": 1.0 + nrm(ks[7], (DEPTH, HG_DV), 0.02),
        "lb_logits": nrm(ks[8], (DEPTH + 1, HG_W), 0.5),
        "w_out": nrm(ks[9], (DEPTH, MIX_WIDTH, D_MODEL), MIX_WIDTH ** -0.5),
        "norm_ffn_w": 1.0 + nrm(ks[10], (DEPTH, D_MODEL), 0.02),
        "w_up": nrm(ks[11], (DEPTH, D_MODEL, D_FF), D_MODEL ** -0.5),
        "w_down": nrm(ks[12], (DEPTH, D_FF, D_MODEL), D_FF ** -0.5),
        "final_norm_w": 1.0 + nrm(ks[13], (D_MODEL,), 0.02),
    }


def reference(x_prompt, x_sample, state_ret, state_hgrn, norm_mix_w, w_in, ret_norm_w, hgrn_norm_w,
              lb_logits, w_out, norm_ffn_w, w_up, w_down, final_norm_w):
    lb_all = jnp.cumsum(jax.nn.softmax(lb_logits.astype(jnp.float32), axis=0), axis=0)
    pos_p = jnp.arange(SEQ, dtype=jnp.int32)
    pos_s = PAST_LEN + jnp.arange(DEC_SEQ, dtype=jnp.int32)
    zero_ret = jnp.zeros((BATCH, RET_HEADS, RET_DK, RET_DV), jnp.float32)
    zero_hg = jnp.zeros((BATCH, HG_HEADS, HG_DK, HG_DV), jnp.float32)

    xp, xs = x_prompt, x_sample
    ret_p, hg_p, ret_s, hg_s = [], [], [], []
    for l in range(DEPTH):
        hp = rmsnorm(xp, norm_mix_w[l])
        mp, sr_p, sh_p = mixer(hp, pos_p, zero_ret, zero_hg, w_in[l], ret_norm_w[l], hgrn_norm_w[l], lb_all[l], w_out[l])
        hs = rmsnorm(xs, norm_mix_w[l])
        ms, sr_s, sh_s = mixer(hs, pos_s, state_ret[l], state_hgrn[l], w_in[l], ret_norm_w[l], hgrn_norm_w[l], lb_all[l], w_out[l])
        xp = xp + mp
        xs = xs + ms
        hp = rmsnorm(xp, norm_ffn_w[l])
        xp = xp + jnp.square(jax.nn.relu(hp @ w_up[l])) @ w_down[l]
        hs = rmsnorm(xs, norm_ffn_w[l])
        xs = xs + jnp.square(jax.nn.relu(hs @ w_up[l])) @ w_down[l]
        ret_p.append(sr_p.astype(x_prompt.dtype))
        hg_p.append(sh_p.astype(x_prompt.dtype))
        ret_s.append(sr_s.astype(state_ret.dtype))
        hg_s.append(sh_s.astype(state_hgrn.dtype))

    y_prompt = rmsnorm(xp, final_norm_w)
    y_sample = rmsnorm(xs, final_norm_w)
    ret_state_prompt = jnp.stack(ret_p, axis=0)
    hgrn_state_prompt = jnp.stack(hg_p, axis=0)
    ret_state_sample = jnp.stack(ret_s, axis=0)
    hgrn_state_sample = jnp.stack(hg_s, axis=0)
    return (y_prompt, y_sample, ret_state_prompt, hgrn_state_prompt, ret_state_sample, hgrn_state_sample)
```

```python
import functools
import math

import jax
import jax.numpy as jnp
import numpy as np
from jax import lax
from jax.experimental import pallas as pl
from jax.experimental.pallas import tpu as pltpu

D_MODEL = 1024
HEADS = 4
DH = 128
GROUP_W = HEADS * DH
IN_COLS = 8 * GROUP_W
D_FF = 4 * D_MODEL
ROPE_BASE = 10000.0
NORM_EPS = 1e-6
QK_SCALE = DH ** -0.5
PAST_LEN = 16384

PROMPT_CHUNK = 256
SAMPLE_SEQS = 8
FFN_ROWS = 512
VMEM_LIMIT = 56 * 1024 * 1024

F32 = jnp.float32
BF16 = jnp.bfloat16


def _dot(a, b):
    return jnp.dot(a, b, preferred_element_type=F32)


def _dot_nt(a, b):
    return lax.dot_general(a, b, (((1,), (1,)), ((), ())), preferred_element_type=F32)


def _dot_tn(a, b):
    return lax.dot_general(a, b, (((0,), (0,)), ((), ())), preferred_element_type=F32)


def _rms(x, w):
    ms = jnp.mean(x * x, axis=-1, keepdims=True)
    return x * lax.rsqrt(ms + NORM_EPS) * w


def _sigmoid(x):
    return 1.0 / (1.0 + jnp.exp(-x))


def _silu(x):
    return x * _sigmoid(x)


def _lower_bound(lbl):
    mx = jnp.max(lbl, axis=0, keepdims=True)
    e = jnp.exp(lbl - mx)
    return e[0:1, :] / jnp.sum(e, axis=0, keepdims=True)


def _rotary(x, cosf, sinf):
    return x * cosf + pltpu.roll(x, DH // 2, 1) * sinf


def _seg_bcast(c, m, t):
    rows = c.shape[0]
    if m == 1:
        return jnp.where((t & 1) == 1, pltpu.roll(c, 1, 0), c)
    if m == 2:
        p = t & 3
        return jnp.where(p == 0, pltpu.roll(c, rows - 1, 0),
                         jnp.where(p == 1, c,
                                   jnp.where(p == 2, pltpu.roll(c, 1, 0), pltpu.roll(c, 2, 0))))
    blk = 2 * m
    c3 = c.reshape(rows // blk, blk, DH)
    return jnp.broadcast_to(c3[:, m - 1:m, :], (rows // blk, blk, DH)).reshape(rows, DH)


def _last_bcast(c, seg):
    rows = c.shape[0]
    if seg == rows:
        return jnp.broadcast_to(c[rows - 1:rows, :], (rows, DH))
    c3 = c.reshape(rows // seg, seg, DH)
    return jnp.broadcast_to(c3[:, seg - 1:seg, :], (rows // seg, seg, DH)).reshape(rows, DH)


def _hgrn_intra(q, k, logf, v, seg, t, xr):
    scores = jnp.where(xr == 0, _dot_nt(q.astype(BF16), k.astype(BF16)), 0.0)
    c = logf
    for j in range(int(math.log2(seg))):
        m = 1 << j
        tot = _seg_bcast(c, m, t)
        upper = (t & m) != 0
        w = jnp.exp(jnp.where(upper, c, tot - c))
        qt = jnp.where(upper, q * w, 0.0).astype(BF16)
        kt = jnp.where(upper, 0.0, k * w).astype(BF16)
        scores = jnp.where((xr >> j) == 1, _dot_nt(qt, kt), scores)
        c = jnp.where(upper, c + tot, c)
    return _dot(scores.astype(BF16), v.astype(BF16)), c


def _ret_mask(rows, seg, log_gamma):
    r = lax.broadcasted_iota(jnp.int32, (rows, rows), 0)
    s = lax.broadcasted_iota(jnp.int32, (rows, rows), 1)
    valid = jnp.logical_and(((r ^ s) >> int(math.log2(seg))) == 0, r >= s)
    d = jnp.where(valid, r - s, 0).astype(F32)
    return jnp.where(valid, jnp.exp(d * log_gamma) * QK_SCALE, 0.0)


def _log_gamma(hd):
    return math.log(1.0 - 2.0 ** (-5.0 - hd))


def _head_out(o, w, gate):
    return o * lax.rsqrt(jnp.mean(o * o, axis=-1, keepdims=True) + NORM_EPS) * w * _silu(gate)


def _project(x_ref, nw_ref, win_ref, proj_ref):
    h = _rms(x_ref[...], nw_ref[...]).astype(BF16)
    for g in range(8):
        cols = slice(g * GROUP_W, (g + 1) * GROUP_W)
        proj_ref[:, cols] = _dot(h, win_ref[:, cols])


def _cols(group, hd):
    return slice(group * GROUP_W + hd * DH, group * GROUP_W + (hd + 1) * DH)


def _mix_prompt_kernel(x_ref, nw_ref, win_ref, cos_ref, sin_ref, rnw_ref, hnw_ref, lbl_ref, wout_ref,
                       x1_ref, sret_ref, shg_ref,
                       proj_ref, o_ref, srt_ref, sht_ref):
    L = PROMPT_CHUNK
    ci = pl.program_id(1)

    @pl.when(ci == 0)
    def _():
        srt_ref[...] = jnp.zeros_like(srt_ref)
        sht_ref[...] = jnp.zeros_like(sht_ref)

    _project(x_ref, nw_ref, win_ref, proj_ref)

    t = lax.broadcasted_iota(jnp.int32, (L, DH), 0)
    tf = t.astype(F32)
    xr = (lax.broadcasted_iota(jnp.int32, (L, L), 0) ^ lax.broadcasted_iota(jnp.int32, (L, L), 1))
    cosf = cos_ref[...]
    sinf = sin_ref[...]
    lb = _lower_bound(lbl_ref[...])

    for hd in range(HEADS):
        lg = _log_gamma(hd)
        q = _rotary(proj_ref[:, _cols(0, hd)], cosf, sinf)
        k = _rotary(proj_ref[:, _cols(1, hd)], cosf, sinf)
        v = proj_ref[:, _cols(2, hd)].astype(BF16)
        scores = _dot_nt(q.astype(BF16), k.astype(BF16)) * _ret_mask(L, L, lg)
        o = _dot(scores.astype(BF16), v)
        st = srt_ref[hd]
        qh = q * jnp.exp((tf + 1.0) * lg)
        o = o + _dot_nt(qh.astype(BF16), st.astype(BF16))
        kh = k * (jnp.exp((L - 1.0 - tf) * lg) * QK_SCALE)
        srt_ref[hd] = st * math.exp(L * lg) + _dot_tn(v, kh.astype(BF16))
        o_ref[:, _cols(0, hd)] = _head_out(o, rnw_ref[...], proj_ref[:, _cols(3, hd)]).astype(BF16)

    for hd in range(HEADS):
        q = _silu(proj_ref[:, _cols(4, hd)]) * QK_SCALE
        lbh = lb[:, hd * DH:(hd + 1) * DH]
        f = lbh + (1.0 - lbh) * _sigmoid(proj_ref[:, _cols(5, hd)])
        k = 1.0 - f
        v = proj_ref[:, _cols(6, hd)]
        o, b = _hgrn_intra(q, k, jnp.log(f), v, L, t, xr)
        blast = b[L - 1:L, :]
        st = sht_ref[hd]
        qh = q * jnp.exp(b)
        o = o + _dot_nt(qh.astype(BF16), st.astype(BF16))
        kh = k * jnp.exp(blast - b)
        sht_ref[hd] = st * jnp.exp(blast) + _dot_tn(v.astype(BF16), kh.astype(BF16))
        o_ref[:, _cols(1, hd)] = _head_out(o, hnw_ref[...], proj_ref[:, _cols(7, hd)]).astype(BF16)

    x1_ref[...] = x_ref[...] + _dot(o_ref[...], wout_ref[...])

    @pl.when(ci == pl.num_programs(1) - 1)
    def _():
        for hd in range(HEADS):
            sret_ref[hd] = srt_ref[hd].T
            shg_ref[hd] = sht_ref[hd].T


def _const_spec(shape):
    return pl.BlockSpec(shape, lambda *_: (0,) * len(shape))


def _mix_prompt(x, nw, win, cosf, sinf, rnw, hnw, lbl, wout):
    B, T, D = x.shape
    L = PROMPT_CHUNK
    nc = T // L
    state = jax.ShapeDtypeStruct((B, HEADS, DH, DH), F32)
    state_spec = pl.BlockSpec((None, HEADS, DH, DH), lambda b, c: (b, 0, 0, 0))
    return pl.pallas_call(
        _mix_prompt_kernel,
        grid=(B, nc),
        in_specs=[
            pl.BlockSpec((None, L, D), lambda b, c: (b, c, 0)),
            _const_spec((1, D)),
            _const_spec((D, IN_COLS)),
            pl.BlockSpec((L, DH), lambda b, c: (c, 0)),
            pl.BlockSpec((L, DH), lambda b, c: (c, 0)),
            _const_spec((1, DH)),
            _const_spec((1, DH)),
            _const_spec(lbl.shape),
            _const_spec((D, D)),
        ],
        out_specs=[pl.BlockSpec((None, L, D), lambda b, c: (b, c, 0)), state_spec, state_spec],
        out_shape=[jax.ShapeDtypeStruct((B, T, D), F32), state, state],
        scratch_shapes=[
            pltpu.VMEM((L, IN_COLS), F32),
            pltpu.VMEM((L, D), BF16),
            pltpu.VMEM((HEADS, DH, DH), F32),
            pltpu.VMEM((HEADS, DH, DH), F32),
        ],
        compiler_params=pltpu.CompilerParams(
            dimension_semantics=("arbitrary", "arbitrary"), vmem_limit_bytes=VMEM_LIMIT),
        name="mix_prompt",
    )(x, nw, win, cosf, sinf, rnw, hnw, lbl, wout)


def _mix_sample_kernel(seg, x_ref, nw_ref, win_ref, cos_ref, sin_ref, rnw_ref, hnw_ref, lbl_ref,
                       wout_ref, sret_in_ref, shg_in_ref,
                       x1_ref, sret_ref, shg_ref,
                       proj_ref, o_ref, qh_ref, kh_ref, dec_ref, ob_ref):
    rows = x_ref.shape[0]
    nseq = rows // seg
    _project(x_ref, nw_ref, win_ref, proj_ref)

    t = lax.broadcasted_iota(jnp.int32, (rows, DH), 0)
    tl = (t & (seg - 1)).astype(F32)
    xr = (lax.broadcasted_iota(jnp.int32, (rows, rows), 0)
          ^ lax.broadcasted_iota(jnp.int32, (rows, rows), 1))
    cosf = cos_ref[...]
    sinf = sin_ref[...]
    lb = _lower_bound(lbl_ref[...])

    for hd in range(HEADS):
        lg = _log_gamma(hd)
        q = _rotary(proj_ref[:, _cols(0, hd)], cosf, sinf)
        k = _rotary(proj_ref[:, _cols(1, hd)], cosf, sinf)
        scores = _dot_nt(q.astype(BF16), k.astype(BF16)) * _ret_mask(rows, seg, lg)
        o_ref[:, _cols(0, hd)] = _dot(scores.astype(BF16), proj_ref[:, _cols(2, hd)].astype(BF16))
        qh_ref[:, _cols(0, hd)] = q * jnp.exp((tl + 1.0) * lg)
        kh_ref[:, _cols(0, hd)] = k * (jnp.exp((seg - 1.0 - tl) * lg) * QK_SCALE)

    for hd in range(HEADS):
        q = _silu(proj_ref[:, _cols(4, hd)]) * QK_SCALE
        lbh = lb[:, hd * DH:(hd + 1) * DH]
        f = lbh + (1.0 - lbh) * _sigmoid(proj_ref[:, _cols(5, hd)])
        k = 1.0 - f
        o, b = _hgrn_intra(q, k, jnp.log(f), proj_ref[:, _cols(6, hd)], seg, t, xr)
        blast = _last_bcast(b, seg)
        o_ref[:, _cols(1, hd)] = o
        qh_ref[:, _cols(1, hd)] = q * jnp.exp(b)
        kh_ref[:, _cols(1, hd)] = k * jnp.exp(blast - b)
        dec_ref[:, hd * DH:(hd + 1) * DH] = jnp.exp(blast)

    def seq_body(s, carry):
        r = pl.ds(pl.multiple_of(s * seg, seg), seg)
        for hd in range(HEADS):
            st = sret_in_ref[s, hd]
            c0 = _cols(0, hd)
            o_ref[r, c0] = o_ref[r, c0] + _dot(qh_ref[r, c0], st)
            upd = _dot_tn(kh_ref[r, c0], proj_ref[r, _cols(2, hd)])
            sret_ref[s, hd] = st * math.exp(seg * _log_gamma(hd)) + upd
        for hd in range(HEADS):
            st = shg_in_ref[s, hd]
            c1 = _cols(1, hd)
            o_ref[r, c1] = o_ref[r, c1] + _dot(qh_ref[r, c1], st)
            upd = _dot_tn(kh_ref[r, c1], proj_ref[r, _cols(6, hd)])
            dec = dec_ref[r, hd * DH:(hd + 1) * DH][0:1, :]
            dcol = jnp.broadcast_to(dec, (DH, DH)).T
            shg_ref[s, hd] = st * dcol + upd
        return carry

    lax.fori_loop(0, nseq, seq_body, 0)

    for hd in range(HEADS):
        c0 = _cols(0, hd)
        ob_ref[:, c0] = _head_out(o_ref[:, c0], rnw_ref[...], proj_ref[:, _cols(3, hd)]).astype(BF16)
        c1 = _cols(1, hd)
        ob_ref[:, c1] = _head_out(o_ref[:, c1], hnw_ref[...], proj_ref[:, _cols(7, hd)]).astype(BF16)
    x1_ref[...] = x_ref[...] + _dot(ob_ref[...], wout_ref[...])


def _mix_sample(x, nw, win, cosf, sinf, rnw, hnw, lbl, wout, sret, shg):
    nb, seg, D = x.shape
    ns = SAMPLE_SEQS
    rows = ns * seg
    x2 = x.reshape(nb * seg, D)
    state = jax.ShapeDtypeStruct((nb, HEADS, DH, DH), F32)
    state_spec = pl.BlockSpec((ns, HEADS, DH, DH), lambda i: (i, 0, 0, 0))
    x1, sret_new, shg_new = pl.pallas_call(
        functools.partial(_mix_sample_kernel, seg),
        grid=(nb // ns,),
        in_specs=[
            pl.BlockSpec((rows, D), lambda i: (i, 0)),
            _const_spec((1, D)),
            _const_spec((D, IN_COLS)),
            _const_spec((rows, DH)),
            _const_spec((rows, DH)),
            _const_spec((1, DH)),
            _const_spec((1, DH)),
            _const_spec(lbl.shape),
            _const_spec((D, D)),
            state_spec,
            state_spec,
        ],
        out_specs=[pl.BlockSpec((rows, D), lambda i: (i, 0)), state_spec, state_spec],
        out_shape=[jax.ShapeDtypeStruct((nb * seg, D), F32), state, state],
        scratch_shapes=[
            pltpu.VMEM((rows, IN_COLS), F32),
            pltpu.VMEM((rows, D), F32),
            pltpu.VMEM((rows, D), F32),
            pltpu.VMEM((rows, D), F32),
            pltpu.VMEM((rows, GROUP_W), F32),
            pltpu.VMEM((rows, D), BF16),
        ],
        compiler_params=pltpu.CompilerParams(
            dimension_semantics=("arbitrary",), vmem_limit_bytes=VMEM_LIMIT),
        name="mix_sample",
    )(x2, nw, win, cosf, sinf, rnw, hnw, lbl, wout, sret, shg)
    return x1, sret_new, shg_new


def _ffn_kernel(x_ref, nw_ref, wup_ref, wdn_ref, fw_ref, y_ref):
    x = x_ref[...]
    h = _rms(x, nw_ref[...]).astype(BF16)
    acc = x
    step = D_MODEL
    for g in range(D_FF // step):
        u = jnp.maximum(_dot(h, wup_ref[:, g * step:(g + 1) * step]), 0.0)
        acc = acc + _dot((u * u).astype(BF16), wdn_ref[g * step:(g + 1) * step, :])
    y_ref[...] = _rms(acc, fw_ref[...])


def _ffn(x, nw, wup, wdn, fw):
    n, D = x.shape
    rows = min(FFN_ROWS, n)
    return pl.pallas_call(
        _ffn_kernel,
        grid=(n // rows,),
        in_specs=[
            pl.BlockSpec((rows, D), lambda i: (i, 0)),
            _const_spec((1, D)),
            _const_spec((D, D_FF)),
            _const_spec((D_FF, D)),
            _const_spec((1, D)),
        ],
        out_specs=pl.BlockSpec((rows, D), lambda i: (i, 0)),
        out_shape=jax.ShapeDtypeStruct((n, D), F32),
        compiler_params=pltpu.CompilerParams(
            dimension_semantics=("arbitrary",), vmem_limit_bytes=VMEM_LIMIT),
        name="ffn",
    )(x, nw, wup, wdn, fw)


def _rope_tables(pos):
    half = DH // 2
    inv_freq = ROPE_BASE ** (-jnp.arange(half, dtype=F32) / half)
    ang = pos.astype(F32)[:, None] * inv_freq[None, :]
    cos, sin = jnp.cos(ang), jnp.sin(ang)
    return jnp.concatenate([cos, cos], axis=-1), jnp.concatenate([-sin, sin], axis=-1)


def kernel(x_prompt, x_sample, state_ret, state_hgrn, norm_mix_w, w_in, ret_norm_w, hgrn_norm_w,
           lb_logits, w_out, norm_ffn_w, w_up, w_down, final_norm_w):
    B, T, D = x_prompt.shape
    nb, seg, _ = x_sample.shape
    assert w_in.shape[0] == 1, "one layer"

    nw = norm_mix_w[0].reshape(1, D)
    win = w_in[0].astype(BF16)
    wout = w_out[0].astype(BF16)
    rnw = ret_norm_w[0].reshape(1, DH)
    hnw = hgrn_norm_w[0].reshape(1, DH)
    lbl = lb_logits.astype(F32)
    fnw = norm_ffn_w[0].reshape(1, D)
    wup = w_up[0].astype(BF16)
    wdn = w_down[0].astype(BF16)
    fw = final_norm_w.reshape(1, D)

    cos_p, sin_p = _rope_tables(jnp.arange(T, dtype=jnp.int32))
    cos_s, sin_s = _rope_tables(PAST_LEN + jnp.arange(seg, dtype=jnp.int32))
    cos_s = jnp.tile(cos_s, (SAMPLE_SEQS, 1))
    sin_s = jnp.tile(sin_s, (SAMPLE_SEQS, 1))

    xp1, sret_p, shg_p = _mix_prompt(x_prompt, nw, win, cos_p, sin_p, rnw, hnw, lbl, wout)
    xs1, sret_s, shg_s = _mix_sample(x_sample, nw, win, cos_s, sin_s, rnw, hnw, lbl, wout,
                                     state_ret[0], state_hgrn[0])

    y_p = _ffn(xp1.reshape(B * T, D), fnw, wup, wdn, fw).reshape(B, T, D)
    y_s = _ffn(xs1, fnw, wup, wdn, fw).reshape(nb, seg, D)
    return (y_p, y_s, sret_p[None], shg_p[None], sret_s[None], shg_s[None])
```

```python
import functools
import math

import jax
import jax.numpy as jnp
from jax import lax
from jax.experimental import pallas as pl
from jax.experimental.pallas import tpu as pltpu

D_MODEL = 1024
HEADS = 4
DH = 128
GROUP_W = HEADS * DH
IN_COLS = 8 * GROUP_W
D_FF = 4 * D_MODEL
ROPE_BASE = 10000.0
NORM_EPS = 1e-6
QK_SCALE = DH ** -0.5
PAST_LEN = 16384

PROMPT_CHUNK = 256
SAMPLE_SEQS = 8
FFN_ROWS = 512
VMEM_LIMIT = 56 * 1024 * 1024

F32 = jnp.float32
BF16 = jnp.bfloat16


def _dot(a, b):
    return jnp.dot(a, b, preferred_element_type=F32)


def _dot_nt(a, b):
    return lax.dot_general(a, b, (((1,), (1,)), ((), ())), preferred_element_type=F32)


def _dot_tn(a, b):
    return lax.dot_general(a, b, (((0,), (0,)), ((), ())), preferred_element_type=F32)


def _rms(x, w):
    ms = jnp.mean(x * x, axis=-1, keepdims=True)
    return x * lax.rsqrt(ms + NORM_EPS) * w


def _sigmoid(x):
    return 1.0 / (1.0 + jnp.exp(-x))


def _silu(x):
    return x * _sigmoid(x)


def _lower_bound(lbl):
    mx = jnp.max(lbl, axis=0, keepdims=True)
    e = jnp.exp(lbl - mx)
    return e[0:1, :] / jnp.sum(e, axis=0, keepdims=True)


def _rotary(x, cosf, sinf):
    return x * cosf + pltpu.roll(x, DH // 2, 1) * sinf


def _split_halves(x, m):
    blk = 2 * m
    nb = x.shape[0] // blk
    lo = [x[b * blk:b * blk + m] for b in range(nb)]
    up = [x[b * blk + m:(b + 1) * blk] for b in range(nb)]
    cat = lambda ps: ps[0] if len(ps) == 1 else jnp.concatenate(ps, axis=0)
    return cat(lo), cat(up)


def _merge_halves(lo, up, m):
    nb = lo.shape[0] // m
    pieces = []
    for b in range(nb):
        pieces += [lo[b * m:(b + 1) * m], up[b * m:(b + 1) * m]]
    return jnp.concatenate(pieces, axis=0)


def _block_row_bcast(x, m, row):
    nb = x.shape[0] // m
    pieces = [jnp.broadcast_to(x[b * m + row:b * m + row + 1], (m, x.shape[1])) for b in range(nb)]
    return pieces[0] if nb == 1 else jnp.concatenate(pieces, axis=0)


def _hgrn_level_masks(rows, seg):
    masks = []
    for j in range(int(math.log2(seg))):
        if (1 << j) < 8:
            r = lax.broadcasted_iota(jnp.int32, (rows, rows), 0)
            s = lax.broadcasted_iota(jnp.int32, (rows, rows), 1)
            masks.append(jnp.logical_and(((r ^ s) >> j) == 1, r > s))
        else:
            r = lax.broadcasted_iota(jnp.int32, (rows // 2, rows), 0)
            s = lax.broadcasted_iota(jnp.int32, (rows // 2, rows), 1)
            masks.append((s >> j) == 2 * (r >> j))
    return masks


def _hgrn_intra(q, k, f, v, seg, t, masks):
    rows = q.shape[0]
    c = jnp.log2(f)
    scores = None
    for j in range(int(math.log2(seg))):
        m = 1 << j
        if m < 8:
            c3 = c.reshape(rows // 8, 8, DH)
            upper = (t & m) != 0
            if m == 1:
                w = jnp.where(upper, f, 1.0)
                tot = pltpu.roll(c3, 1, 1).reshape(rows, DH)
            else:
                if m == 2:
                    low4 = ((t & 7) < 4).reshape(rows // 8, 8, DH)
                    tot3 = jnp.where(low4, jnp.broadcast_to(c3[:, 1:2, :], c3.shape),
                                     jnp.broadcast_to(c3[:, 5:6, :], c3.shape))
                else:
                    tot3 = jnp.broadcast_to(c3[:, 3:4, :], c3.shape)
                tot = tot3.reshape(rows, DH)
                w = jnp.exp2(jnp.where(upper, c, tot - c))
            part = _dot_nt((q * w).astype(BF16), (k * w).astype(BF16))
            scores = jnp.where(masks[j], part, 0.0 if scores is None else scores)
            c = jnp.where(upper, c + tot, c)
        else:
            c_lo, c_up = _split_halves(c, m)
            tot = _block_row_bcast(c_lo, m, m - 1)
            q_up = _split_halves(q, m)[1]
            k_lo, k_up = _split_halves(k, m)
            qt = (q_up * jnp.exp2(c_up)).astype(BF16)
            kt = _merge_halves(k_lo * jnp.exp2(tot - c_lo), k_up, m).astype(BF16)
            s_lo, s_up = _split_halves(scores, m)
            s_up = jnp.where(masks[j], _dot_nt(qt, kt), s_up)
            scores = _merge_halves(s_lo, s_up, m)
            c = _merge_halves(c_lo, c_up + tot, m)
    o = _dot(scores.astype(BF16), v.astype(BF16))
    o = o + jnp.sum(q * k, axis=-1, keepdims=True) * v
    return o, c


def _ret_mask(rows, seg, log_gamma):
    r = lax.broadcasted_iota(jnp.int32, (rows, rows), 0)
    s = lax.broadcasted_iota(jnp.int32, (rows, rows), 1)
    valid = jnp.logical_and(((r ^ s) >> int(math.log2(seg))) == 0, r >= s)
    d = jnp.where(valid, r - s, 0).astype(F32)
    return jnp.where(valid, jnp.exp(d * log_gamma) * QK_SCALE, 0.0)


def _log_gamma(hd):
    return math.log(1.0 - 2.0 ** (-5.0 - hd))


def _head_out(o, w, gate):
    return o * lax.rsqrt(jnp.mean(o * o, axis=-1, keepdims=True) + NORM_EPS) * w * _silu(gate)


def _project(x_ref, nw_ref, win_ref, proj_ref):
    h = _rms(x_ref[...], nw_ref[...]).astype(BF16)
    for g in range(8):
        cols = slice(g * GROUP_W, (g + 1) * GROUP_W)
        proj_ref[:, cols] = _dot(h, win_ref[:, cols])


def _cols(group, hd):
    return slice(group * GROUP_W + hd * DH, group * GROUP_W + (hd + 1) * DH)


def _mix_prompt_kernel(x_ref, nw_ref, win_ref, cos_ref, sin_ref, rnw_ref, hnw_ref, lbl_ref, wout_ref,
                       x1_ref, sret_ref, shg_ref,
                       proj_ref, o_ref, srt_ref, sht_ref, dm_ref, qd_ref, kd_ref):
    L = PROMPT_CHUNK
    ci = pl.program_id(1)
    t = lax.broadcasted_iota(jnp.int32, (L, DH), 0)

    @pl.when(jnp.logical_and(pl.program_id(0) == 0, ci == 0))
    def _():
        tf = t.astype(F32)
        for hd in range(HEADS):
            lg = _log_gamma(hd)
            dm_ref[hd] = _ret_mask(L, L, lg)
            qd_ref[hd] = jnp.exp((tf + 1.0) * lg)
            kd_ref[hd] = jnp.exp((L - 1.0 - tf) * lg) * QK_SCALE

    @pl.when(ci == 0)
    def _():
        srt_ref[...] = jnp.zeros_like(srt_ref)
        sht_ref[...] = jnp.zeros_like(sht_ref)

    _project(x_ref, nw_ref, win_ref, proj_ref)

    cosf = cos_ref[...]
    sinf = sin_ref[...]
    lb = _lower_bound(lbl_ref[...])
    masks = _hgrn_level_masks(L, L)

    for hd in range(HEADS):
        q = _rotary(proj_ref[:, _cols(0, hd)], cosf, sinf)
        k = _rotary(proj_ref[:, _cols(1, hd)], cosf, sinf)
        v = proj_ref[:, _cols(2, hd)].astype(BF16)
        scores = _dot_nt(q.astype(BF16), k.astype(BF16)) * dm_ref[hd]
        o = _dot(scores.astype(BF16), v)
        st = srt_ref[hd]
        o = o + _dot_nt((q * qd_ref[hd]).astype(BF16), st.astype(BF16))
        kh = (k * kd_ref[hd]).astype(BF16)
        srt_ref[hd] = st * math.exp(L * _log_gamma(hd)) + _dot_tn(v, kh)
        o_ref[:, _cols(0, hd)] = _head_out(o, rnw_ref[...], proj_ref[:, _cols(3, hd)]).astype(BF16)

    for hd in range(HEADS):
        q = _silu(proj_ref[:, _cols(4, hd)]) * QK_SCALE
        lbh = lb[:, hd * DH:(hd + 1) * DH]
        f = lbh + (1.0 - lbh) * _sigmoid(proj_ref[:, _cols(5, hd)])
        k = 1.0 - f
        v = proj_ref[:, _cols(6, hd)]
        o, b = _hgrn_intra(q, k, f, v, L, t, masks)
        blast = b[L - 1:L, :]
        st = sht_ref[hd]
        o = o + _dot_nt((q * jnp.exp2(b)).astype(BF16), st.astype(BF16))
        kh = (k * jnp.exp2(blast - b)).astype(BF16)
        sht_ref[hd] = st * jnp.exp2(blast) + _dot_tn(v.astype(BF16), kh)
        o_ref[:, _cols(1, hd)] = _head_out(o, hnw_ref[...], proj_ref[:, _cols(7, hd)]).astype(BF16)

    x1_ref[...] = x_ref[...] + _dot(o_ref[...], wout_ref[...])

    @pl.when(ci == pl.num_programs(1) - 1)
    def _():
        for hd in range(HEADS):
            sret_ref[hd] = srt_ref[hd].T
            shg_ref[hd] = sht_ref[hd].T


def _const_spec(shape):
    return pl.BlockSpec(shape, lambda *_: (0,) * len(shape))


def _mix_prompt(x, nw, win, cosf, sinf, rnw, hnw, lbl, wout):
    B, T, D = x.shape
    L = PROMPT_CHUNK
    nc = T // L
    state = jax.ShapeDtypeStruct((B, HEADS, DH, DH), F32)
    state_spec = pl.BlockSpec((None, HEADS, DH, DH), lambda b, c: (b, 0, 0, 0))
    return pl.pallas_call(
        _mix_prompt_kernel,
        grid=(B, nc),
        in_specs=[
            pl.BlockSpec((None, L, D), lambda b, c: (b, c, 0)),
            _const_spec((1, D)),
            _const_spec((D, IN_COLS)),
            pl.BlockSpec((L, DH), lambda b, c: (c, 0)),
            pl.BlockSpec((L, DH), lambda b, c: (c, 0)),
            _const_spec((1, DH)),
            _const_spec((1, DH)),
            _const_spec(lbl.shape),
            _const_spec((D, D)),
        ],
        out_specs=[pl.BlockSpec((None, L, D), lambda b, c: (b, c, 0)), state_spec, state_spec],
        out_shape=[jax.ShapeDtypeStruct((B, T, D), F32), state, state],
        scratch_shapes=[
            pltpu.VMEM((L, IN_COLS), F32),
            pltpu.VMEM((L, D), BF16),
            pltpu.VMEM((HEADS, DH, DH), F32),
            pltpu.VMEM((HEADS, DH, DH), F32),
            pltpu.VMEM((HEADS, L, L), F32),
            pltpu.VMEM((HEADS, L, DH), F32),
            pltpu.VMEM((HEADS, L, DH), F32),
        ],
        compiler_params=pltpu.CompilerParams(
            dimension_semantics=("arbitrary", "arbitrary"), vmem_limit_bytes=VMEM_LIMIT),
        name="mix_prompt",
    )(x, nw, win, cosf, sinf, rnw, hnw, lbl, wout)


def _mix_sample_kernel(seg, x_ref, nw_ref, win_ref, cos_ref, sin_ref, rnw_ref, hnw_ref, lbl_ref,
                       wout_ref, sret_in_ref, shg_in_ref,
                       x1_ref, sret_ref, shg_ref,
                       proj_ref, o_ref, qh_ref, kh_ref, dec_ref, ob_ref):
    rows = x_ref.shape[0]
    nseq = rows // seg
    _project(x_ref, nw_ref, win_ref, proj_ref)

    t = lax.broadcasted_iota(jnp.int32, (rows, DH), 0)
    tl = (t & (seg - 1)).astype(F32)
    cosf = cos_ref[...]
    sinf = sin_ref[...]
    lb = _lower_bound(lbl_ref[...])
    masks = _hgrn_level_masks(rows, seg)

    for hd in range(HEADS):
        lg = _log_gamma(hd)
        q = _rotary(proj_ref[:, _cols(0, hd)], cosf, sinf)
        k = _rotary(proj_ref[:, _cols(1, hd)], cosf, sinf)
        scores = _dot_nt(q.astype(BF16), k.astype(BF16)) * _ret_mask(rows, seg, lg)
        o_ref[:, _cols(0, hd)] = _dot(scores.astype(BF16), proj_ref[:, _cols(2, hd)].astype(BF16))
        qh_ref[:, _cols(0, hd)] = q * jnp.exp((tl + 1.0) * lg)
        kh_ref[:, _cols(0, hd)] = k * (jnp.exp((seg - 1.0 - tl) * lg) * QK_SCALE)

    for hd in range(HEADS):
        q = _silu(proj_ref[:, _cols(4, hd)]) * QK_SCALE
        lbh = lb[:, hd * DH:(hd + 1) * DH]
        f = lbh + (1.0 - lbh) * _sigmoid(proj_ref[:, _cols(5, hd)])
        k = 1.0 - f
        o, b = _hgrn_intra(q, k, f, proj_ref[:, _cols(6, hd)], seg, t, masks)
        blast = _block_row_bcast(b, seg, seg - 1)
        o_ref[:, _cols(1, hd)] = o
        qh_ref[:, _cols(1, hd)] = q * jnp.exp2(b)
        kh_ref[:, _cols(1, hd)] = k * jnp.exp2(blast - b)
        dec_ref[:, hd * DH:(hd + 1) * DH] = jnp.exp2(blast)

    def seq_body(s, carry):
        r = pl.ds(pl.multiple_of(s * seg, seg), seg)
        for hd in range(HEADS):
            st = sret_in_ref[s, hd]
            c0 = _cols(0, hd)
            o_ref[r, c0] = o_ref[r, c0] + _dot(qh_ref[r, c0], st)
            upd = _dot_tn(kh_ref[r, c0], proj_ref[r, _cols(2, hd)])
            sret_ref[s, hd] = st * math.exp(seg * _log_gamma(hd)) + upd
        for hd in range(HEADS):
            st = shg_in_ref[s, hd]
            c1 = _cols(1, hd)
            o_ref[r, c1] = o_ref[r, c1] + _dot(qh_ref[r, c1], st)
            upd = _dot_tn(kh_ref[r, c1], proj_ref[r, _cols(6, hd)])
            dec = dec_ref[r, hd * DH:(hd + 1) * DH][0:1, :]
            dcol = jnp.broadcast_to(dec, (DH, DH)).T
            shg_ref[s, hd] = st * dcol + upd
        return carry

    lax.fori_loop(0, nseq, seq_body, 0)

    for hd in range(HEADS):
        c0 = _cols(0, hd)
        ob_ref[:, c0] = _head_out(o_ref[:, c0], rnw_ref[...], proj_ref[:, _cols(3, hd)]).astype(BF16)
        c1 = _cols(1, hd)
        ob_ref[:, c1] = _head_out(o_ref[:, c1], hnw_ref[...], proj_ref[:, _cols(7, hd)]).astype(BF16)
    x1_ref[...] = x_ref[...] + _dot(ob_ref[...], wout_ref[...])


def _mix_sample(x, nw, win, cosf, sinf, rnw, hnw, lbl, wout, sret, shg):
    nb, seg, D = x.shape
    ns = SAMPLE_SEQS
    rows = ns * seg
    x2 = x.reshape(nb * seg, D)
    state = jax.ShapeDtypeStruct((nb, HEADS, DH, DH), F32)
    state_spec = pl.BlockSpec((ns, HEADS, DH, DH), lambda i: (i, 0, 0, 0))
    x1, sret_new, shg_new = pl.pallas_call(
        functools.partial(_mix_sample_kernel, seg),
        grid=(nb // ns,),
        in_specs=[
            pl.BlockSpec((rows, D), lambda i: (i, 0)),
            _const_spec((1, D)),
            _const_spec((D, IN_COLS)),
            _const_spec((rows, DH)),
            _const_spec((rows, DH)),
            _const_spec((1, DH)),
            _const_spec((1, DH)),
            _const_spec(lbl.shape),
            _const_spec((D, D)),
            state_spec,
            state_spec,
        ],
        out_specs=[pl.BlockSpec((rows, D), lambda i: (i, 0)), state_spec, state_spec],
        out_shape=[jax.ShapeDtypeStruct((nb * seg, D), F32), state, state],
        scratch_shapes=[
            pltpu.VMEM((rows, IN_COLS), F32),
            pltpu.VMEM((rows, D), F32),
            pltpu.VMEM((rows, D), F32),
            pltpu.VMEM((rows, D), F32),
            pltpu.VMEM((rows, GROUP_W), F32),
            pltpu.VMEM((rows, D), BF16),
        ],
        compiler_params=pltpu.CompilerParams(
            dimension_semantics=("arbitrary",), vmem_limit_bytes=VMEM_LIMIT),
        name="mix_sample",
    )(x2, nw, win, cosf, sinf, rnw, hnw, lbl, wout, sret, shg)
    return x1, sret_new, shg_new


def _ffn_kernel(x_ref, nw_ref, wup_ref, wdn_ref, fw_ref, y_ref):
    x = x_ref[...]
    h = _rms(x, nw_ref[...]).astype(BF16)
    acc = x
    step = D_MODEL
    for g in range(D_FF // step):
        u = jnp.maximum(_dot(h, wup_ref[:, g * step:(g + 1) * step]), 0.0)
        acc = acc + _dot((u * u).astype(BF16), wdn_ref[g * step:(g + 1) * step, :])
    y_ref[...] = _rms(acc, fw_ref[...])


def _ffn(x, nw, wup, wdn, fw):
    n, D = x.shape
    rows = min(FFN_ROWS, n)
    return pl.pallas_call(
        _ffn_kernel,
        grid=(n // rows,),
        in_specs=[
            pl.BlockSpec((rows, D), lambda i: (i, 0)),
            _const_spec((1, D)),
            _const_spec((D, D_FF)),
            _const_spec((D_FF, D)),
            _const_spec((1, D)),
        ],
        out_specs=pl.BlockSpec((rows, D), lambda i: (i, 0)),
        out_shape=jax.ShapeDtypeStruct((n, D), F32),
        compiler_params=pltpu.CompilerParams(
            dimension_semantics=("arbitrary",), vmem_limit_bytes=VMEM_LIMIT),
        name="ffn",
    )(x, nw, wup, wdn, fw)


def _rope_tables(pos):
    half = DH // 2
    inv_freq = ROPE_BASE ** (-jnp.arange(half, dtype=F32) / half)
    ang = pos.astype(F32)[:, None] * inv_freq[None, :]
    cos, sin = jnp.cos(ang), jnp.sin(ang)
    return jnp.concatenate([cos, cos], axis=-1), jnp.concatenate([-sin, sin], axis=-1)


def kernel(x_prompt, x_sample, state_ret, state_hgrn, norm_mix_w, w_in, ret_norm_w, hgrn_norm_w,
           lb_logits, w_out, norm_ffn_w, w_up, w_down, final_norm_w):
    B, T, D = x_prompt.shape
    nb, seg, _ = x_sample.shape
    assert w_in.shape[0] == 1, "one layer"

    nw = norm_mix_w[0].reshape(1, D)
    win = w_in[0].astype(BF16)
    wout = w_out[0].astype(BF16)
    rnw = ret_norm_w[0].reshape(1, DH)
    hnw = hgrn_norm_w[0].reshape(1, DH)
    lbl = lb_logits.astype(F32)
    fnw = norm_ffn_w[0].reshape(1, D)
    wup = w_up[0].astype(BF16)
    wdn = w_down[0].astype(BF16)
    fw = final_norm_w.reshape(1, D)

    cos_p, sin_p = _rope_tables(jnp.arange(T, dtype=jnp.int32))
    cos_s, sin_s = _rope_tables(PAST_LEN + jnp.arange(seg, dtype=jnp.int32))
    cos_s = jnp.tile(cos_s, (SAMPLE_SEQS, 1))
    sin_s = jnp.tile(sin_s, (SAMPLE_SEQS, 1))

    xp1, sret_p, shg_p = _mix_prompt(x_prompt, nw, win, cos_p, sin_p, rnw, hnw, lbl, wout)
    xs1, sret_s, shg_s = _mix_sample(x_sample, nw, win, cos_s, sin_s, rnw, hnw, lbl, wout,
                                     state_ret[0], state_hgrn[0])

    y_p = _ffn(xp1.reshape(B * T, D), fnw, wup, wdn, fw).reshape(B, T, D)
    y_s = _ffn(xs1, fnw, wup, wdn, fw).reshape(nb, seg, D)
    return (y_p, y_s, sret_p[None], shg_p[None], sret_s[None], shg_s[None])
```

```python
import functools
import math

import jax
import jax.numpy as jnp
from jax import lax
from jax.experimental import pallas as pl
from jax.experimental.pallas import tpu as pltpu

D_MODEL = 1024
HEADS = 4
DH = 128
GROUP_W = HEADS * DH
IN_COLS = 8 * GROUP_W
D_FF = 4 * D_MODEL
ROPE_BASE = 10000.0
NORM_EPS = 1e-6
QK_SCALE = DH ** -0.5
PAST_LEN = 16384

PROMPT_CHUNK = 256
SAMPLE_SEQS = 8
FFN_ROWS = 512
PROJ_PIECE = 512
VMEM_LIMIT = 56 * 1024 * 1024

F32 = jnp.float32
BF16 = jnp.bfloat16


def _dot(a, b):
    return jnp.dot(a, b, preferred_element_type=F32)


def _dot_nt(a, b):
    return lax.dot_general(a, b, (((1,), (1,)), ((), ())), preferred_element_type=F32)


def _dot_tn(a, b):
    return lax.dot_general(a, b, (((0,), (0,)), ((), ())), preferred_element_type=F32)


def _rms(x, w):
    ms = jnp.mean(x * x, axis=-1, keepdims=True)
    return x * lax.rsqrt(ms + NORM_EPS) * w


def _sigmoid(x):
    return 1.0 / (1.0 + jnp.exp(-x))


def _silu(x):
    return x * _sigmoid(x)


def _lower_bound(lbl):
    mx = jnp.max(lbl, axis=0, keepdims=True)
    e = jnp.exp(lbl - mx)
    return e[0:1, :] / jnp.sum(e, axis=0, keepdims=True)


def _rotary(x, cosf, sinf):
    return x * cosf + pltpu.roll(x, DH // 2, 1) * sinf


def _split_halves(x, m):
    blk = 2 * m
    nb = x.shape[0] // blk
    lo = [x[b * blk:b * blk + m] for b in range(nb)]
    up = [x[b * blk + m:(b + 1) * blk] for b in range(nb)]
    cat = lambda ps: ps[0] if len(ps) == 1 else jnp.concatenate(ps, axis=0)
    return cat(lo), cat(up)


def _merge_halves(lo, up, m):
    nb = lo.shape[0] // m
    pieces = []
    for b in range(nb):
        pieces += [lo[b * m:(b + 1) * m], up[b * m:(b + 1) * m]]
    return jnp.concatenate(pieces, axis=0)


def _block_row_bcast(x, m, row):
    nb = x.shape[0] // m
    pieces = [jnp.broadcast_to(x[b * m + row:b * m + row + 1], (m, x.shape[1])) for b in range(nb)]
    return pieces[0] if nb == 1 else jnp.concatenate(pieces, axis=0)


def _hgrn_level_masks(rows, seg):
    masks = []
    for j in range(int(math.log2(seg))):
        if (1 << j) < 8:
            r = lax.broadcasted_iota(jnp.int32, (rows, rows), 0)
            s = lax.broadcasted_iota(jnp.int32, (rows, rows), 1)
            masks.append(jnp.logical_and(((r ^ s) >> j) == 1, r > s))
        else:
            r = lax.broadcasted_iota(jnp.int32, (rows // 2, rows), 0)
            s = lax.broadcasted_iota(jnp.int32, (rows // 2, rows), 1)
            masks.append((s >> j) == 2 * (r >> j))
    return masks


def _hgrn_intra(q, k, f, v, seg, t, masks, mid_hook=None):
    rows = q.shape[0]
    c = jnp.log2(f)
    scores = None
    for j in range(int(math.log2(seg))):
        m = 1 << j
        if m == 8 and mid_hook is not None:
            mid_hook()
        if m < 8:
            c3 = c.reshape(rows // 8, 8, DH)
            upper = (t & m) != 0
            if m == 1:
                w = jnp.where(upper, f, 1.0)
                tot = pltpu.roll(c3, 1, 1).reshape(rows, DH)
            else:
                if m == 2:
                    low4 = ((t & 7) < 4).reshape(rows // 8, 8, DH)
                    tot3 = jnp.where(low4, jnp.broadcast_to(c3[:, 1:2, :], c3.shape),
                                     jnp.broadcast_to(c3[:, 5:6, :], c3.shape))
                else:
                    tot3 = jnp.broadcast_to(c3[:, 3:4, :], c3.shape)
                tot = tot3.reshape(rows, DH)
                w = jnp.exp2(jnp.where(upper, c, tot - c))
            part = _dot_nt((q * w).astype(BF16), (k * w).astype(BF16))
            scores = jnp.where(masks[j], part, 0.0 if scores is None else scores)
            c = jnp.where(upper, c + tot, c)
        else:
            c_lo, c_up = _split_halves(c, m)
            tot = _block_row_bcast(c_lo, m, m - 1)
            q_up = _split_halves(q, m)[1]
            k_lo, k_up = _split_halves(k, m)
            qt = (q_up * jnp.exp2(c_up)).astype(BF16)
            kt = _merge_halves(k_lo * jnp.exp2(tot - c_lo), k_up, m).astype(BF16)
            s_lo, s_up = _split_halves(scores, m)
            s_up = jnp.where(masks[j], _dot_nt(qt, kt), s_up)
            scores = _merge_halves(s_lo, s_up, m)
            c = _merge_halves(c_lo, c_up + tot, m)
    o = _dot(scores.astype(BF16), v.astype(BF16))
    o = o + jnp.sum(q * k, axis=-1, keepdims=True) * v
    return o, c


def _ret_mask(rows, seg, log_gamma):
    r = lax.broadcasted_iota(jnp.int32, (rows, rows), 0)
    s = lax.broadcasted_iota(jnp.int32, (rows, rows), 1)
    valid = jnp.logical_and(((r ^ s) >> int(math.log2(seg))) == 0, r >= s)
    d = jnp.where(valid, r - s, 0).astype(F32)
    return jnp.where(valid, jnp.exp(d * log_gamma) * QK_SCALE, 0.0)


def _log_gamma(hd):
    return math.log(1.0 - 2.0 ** (-5.0 - hd))


def _head_out(o, w, gate):
    return o * lax.rsqrt(jnp.mean(o * o, axis=-1, keepdims=True) + NORM_EPS) * w * _silu(gate)


def _project(x_ref, nw_ref, win_ref, proj_ref):
    h = _rms(x_ref[...], nw_ref[...]).astype(BF16)
    for g in range(8):
        cols = slice(g * GROUP_W, (g + 1) * GROUP_W)
        proj_ref[:, cols] = _dot(h, win_ref[:, cols])


def _cols(group, hd):
    return slice(group * GROUP_W + hd * DH, group * GROUP_W + (hd + 1) * DH)


def _mix_prompt_kernel(nc, xn_ref, xc_ref, nw_ref, win_ref, cos_ref, sin_ref, rnw_ref, hnw_ref,
                       lbl_ref, wout_ref,
                       x1_ref, sret_ref, shg_ref,
                       proj_ref, o_ref, srt_ref, sht_ref, dm_ref, qd_ref, kd_ref):
    L = PROMPT_CHUNK
    g = pl.program_id(0)
    ci = (jnp.maximum(g, 1) - 1) % nc
    slot = g % 2
    t = lax.broadcasted_iota(jnp.int32, (L, DH), 0)

    @pl.when(g == 0)
    def _():
        tf = t.astype(F32)
        for hd in range(HEADS):
            lg = _log_gamma(hd)
            dm_ref[hd] = _ret_mask(L, L, lg)
            qd_ref[hd] = jnp.exp((tf + 1.0) * lg)
            kd_ref[hd] = jnp.exp((L - 1.0 - tf) * lg) * QK_SCALE
        proj_ref[1] = jnp.zeros((L, IN_COLS), F32)

    @pl.when(ci == 0)
    def _():
        srt_ref[...] = jnp.zeros_like(srt_ref)
        sht_ref[...] = jnp.zeros_like(sht_ref)

    def stages(pw, pr):
        h_next = _rms(xn_ref[...], nw_ref[...]).astype(BF16)

        def project_piece(pi):
            cols = slice(pi * PROJ_PIECE, (pi + 1) * PROJ_PIECE)
            pw[:, cols] = _dot(h_next, win_ref[:, cols])

        cosf = cos_ref[...]
        sinf = sin_ref[...]
        lb = _lower_bound(lbl_ref[...])
        masks = _hgrn_level_masks(L, L)

        for hd in range(HEADS):
            project_piece(hd)
            q = _rotary(pr[:, _cols(0, hd)], cosf, sinf)
            k = _rotary(pr[:, _cols(1, hd)], cosf, sinf)
            v = pr[:, _cols(2, hd)].astype(BF16)
            scores = _dot_nt(q.astype(BF16), k.astype(BF16)) * dm_ref[hd]
            o = _dot(scores.astype(BF16), v)
            st = srt_ref[hd]
            o = o + _dot_nt((q * qd_ref[hd]).astype(BF16), st.astype(BF16))
            kh = (k * kd_ref[hd]).astype(BF16)
            srt_ref[hd] = st * math.exp(L * _log_gamma(hd)) + _dot_tn(v, kh)
            o_ref[:, _cols(0, hd)] = _head_out(o, rnw_ref[...], pr[:, _cols(3, hd)]).astype(BF16)

        for hd in range(HEADS):
            project_piece(HEADS + hd)
            q = _silu(pr[:, _cols(4, hd)]) * QK_SCALE
            lbh = lb[:, hd * DH:(hd + 1) * DH]
            f = lbh + (1.0 - lbh) * _sigmoid(pr[:, _cols(5, hd)])
            k = 1.0 - f
            v = pr[:, _cols(6, hd)]
            o, b = _hgrn_intra(q, k, f, v, L, t, masks)
            blast = b[L - 1:L, :]
            st = sht_ref[hd]
            o = o + _dot_nt((q * jnp.exp2(b)).astype(BF16), st.astype(BF16))
            kh = (k * jnp.exp2(blast - b)).astype(BF16)
            sht_ref[hd] = st * jnp.exp2(blast) + _dot_tn(v.astype(BF16), kh)
            o_ref[:, _cols(1, hd)] = _head_out(o, hnw_ref[...], pr[:, _cols(7, hd)]).astype(BF16)

        x1_ref[...] = xc_ref[...] + _dot(o_ref[...], wout_ref[...])

    @pl.when(slot == 0)
    def _():
        stages(proj_ref.at[0], proj_ref.at[1])

    @pl.when(slot == 1)
    def _():
        stages(proj_ref.at[1], proj_ref.at[0])

    @pl.when(ci == nc - 1)
    def _():
        for hd in range(HEADS):
            sret_ref[hd] = srt_ref[hd].T
            shg_ref[hd] = sht_ref[hd].T


def _const_spec(shape):
    return pl.BlockSpec(shape, lambda *_: (0,) * len(shape))


def _mix_prompt(x, nw, win, cosf, sinf, rnw, hnw, lbl, wout):
    B, T, D = x.shape
    L = PROMPT_CHUNK
    nc = T // L
    n = B * nc
    nxt = lambda g: jnp.minimum(g, n - 1)
    cur = lambda g: jnp.maximum(g, 1) - 1
    state = jax.ShapeDtypeStruct((B, HEADS, DH, DH), F32)
    state_spec = pl.BlockSpec((None, HEADS, DH, DH), lambda g: (cur(g) // nc, 0, 0, 0))
    return pl.pallas_call(
        functools.partial(_mix_prompt_kernel, nc),
        grid=(n + 1,),
        in_specs=[
            pl.BlockSpec((None, L, D), lambda g: (nxt(g) // nc, nxt(g) % nc, 0)),
            pl.BlockSpec((None, L, D), lambda g: (cur(g) // nc, cur(g) % nc, 0)),
            _const_spec((1, D)),
            _const_spec((D, IN_COLS)),
            pl.BlockSpec((L, DH), lambda g: (cur(g) % nc, 0)),
            pl.BlockSpec((L, DH), lambda g: (cur(g) % nc, 0)),
            _const_spec((1, DH)),
            _const_spec((1, DH)),
            _const_spec(lbl.shape),
            _const_spec((D, D)),
        ],
        out_specs=[pl.BlockSpec((None, L, D), lambda g: (cur(g) // nc, cur(g) % nc, 0)),
                   state_spec, state_spec],
        out_shape=[jax.ShapeDtypeStruct((B, T, D), F32), state, state],
        scratch_shapes=[
            pltpu.VMEM((2, L, IN_COLS), F32),
            pltpu.VMEM((L, D), BF16),
            pltpu.VMEM((HEADS, DH, DH), F32),
            pltpu.VMEM((HEADS, DH, DH), F32),
            pltpu.VMEM((HEADS, L, L), F32),
            pltpu.VMEM((HEADS, L, DH), F32),
            pltpu.VMEM((HEADS, L, DH), F32),
        ],
        compiler_params=pltpu.CompilerParams(
            dimension_semantics=("arbitrary",), vmem_limit_bytes=VMEM_LIMIT),
        name="mix_prompt",
    )(x, x, nw, win, cosf, sinf, rnw, hnw, lbl, wout)


def _mix_sample_kernel(seg, x_ref, nw_ref, win_ref, cos_ref, sin_ref, rnw_ref, hnw_ref, lbl_ref,
                       wout_ref, sret_in_ref, shg_in_ref,
                       x1_ref, sret_ref, shg_ref,
                       proj_ref, o_ref, qh_ref, kh_ref, dec_ref, ob_ref):
    rows = x_ref.shape[0]
    nseq = rows // seg
    _project(x_ref, nw_ref, win_ref, proj_ref)

    t = lax.broadcasted_iota(jnp.int32, (rows, DH), 0)
    tl = (t & (seg - 1)).astype(F32)
    cosf = cos_ref[...]
    sinf = sin_ref[...]
    lb = _lower_bound(lbl_ref[...])
    masks = _hgrn_level_masks(rows, seg)

    for hd in range(HEADS):
        lg = _log_gamma(hd)
        q = _rotary(proj_ref[:, _cols(0, hd)], cosf, sinf)
        k = _rotary(proj_ref[:, _cols(1, hd)], cosf, sinf)
        scores = _dot_nt(q.astype(BF16), k.astype(BF16)) * _ret_mask(rows, seg, lg)
        o_ref[:, _cols(0, hd)] = _dot(scores.astype(BF16), proj_ref[:, _cols(2, hd)].astype(BF16))
        qh_ref[:, _cols(0, hd)] = q * jnp.exp((tl + 1.0) * lg)
        kh_ref[:, _cols(0, hd)] = k * (jnp.exp((seg - 1.0 - tl) * lg) * QK_SCALE)

    for hd in range(HEADS):
        q = _silu(proj_ref[:, _cols(4, hd)]) * QK_SCALE
        lbh = lb[:, hd * DH:(hd + 1) * DH]
        f = lbh + (1.0 - lbh) * _sigmoid(proj_ref[:, _cols(5, hd)])
        k = 1.0 - f
        o, b = _hgrn_intra(q, k, f, proj_ref[:, _cols(6, hd)], seg, t, masks)
        blast = _block_row_bcast(b, seg, seg - 1)
        o_ref[:, _cols(1, hd)] = o
        qh_ref[:, _cols(1, hd)] = q * jnp.exp2(b)
        kh_ref[:, _cols(1, hd)] = k * jnp.exp2(blast - b)
        dec_ref[:, hd * DH:(hd + 1) * DH] = jnp.exp2(blast)

    def seq_body(s, carry):
        r = pl.ds(pl.multiple_of(s * seg, seg), seg)
        for hd in range(HEADS):
            st = sret_in_ref[s, hd]
            c0 = _cols(0, hd)
            o_ref[r, c0] = o_ref[r, c0] + _dot(qh_ref[r, c0], st)
            upd = _dot_tn(kh_ref[r, c0], proj_ref[r, _cols(2, hd)])
            sret_ref[s, hd] = st * math.exp(seg * _log_gamma(hd)) + upd
        for hd in range(HEADS):
            st = shg_in_ref[s, hd]
            c1 = _cols(1, hd)
            o_ref[r, c1] = o_ref[r, c1] + _dot(qh_ref[r, c1], st)
            upd = _dot_tn(kh_ref[r, c1], proj_ref[r, _cols(6, hd)])
            dec = dec_ref[r, hd * DH:(hd + 1) * DH][0:1, :]
            dcol = jnp.broadcast_to(dec, (DH, DH)).T
            shg_ref[s, hd] = st * dcol + upd
        return carry

    lax.fori_loop(0, nseq, seq_body, 0)

    for hd in range(HEADS):
        c0 = _cols(0, hd)
        ob_ref[:, c0] = _head_out(o_ref[:, c0], rnw_ref[...], proj_ref[:, _cols(3, hd)]).astype(BF16)
        c1 = _cols(1, hd)
        ob_ref[:, c1] = _head_out(o_ref[:, c1], hnw_ref[...], proj_ref[:, _cols(7, hd)]).astype(BF16)
    x1_ref[...] = x_ref[...] + _dot(ob_ref[...], wout_ref[...])


def _mix_sample(x, nw, win, cosf, sinf, rnw, hnw, lbl, wout, sret, shg):
    nb, seg, D = x.shape
    ns = SAMPLE_SEQS
    rows = ns * seg
    x2 = x.reshape(nb * seg, D)
    state = jax.ShapeDtypeStruct((nb, HEADS, DH, DH), F32)
    state_spec = pl.BlockSpec((ns, HEADS, DH, DH), lambda i: (i, 0, 0, 0))
    x1, sret_new, shg_new = pl.pallas_call(
        functools.partial(_mix_sample_kernel, seg),
        grid=(nb // ns,),
        in_specs=[
            pl.BlockSpec((rows, D), lambda i: (i, 0)),
            _const_spec((1, D)),
            _const_spec((D, IN_COLS)),
            _const_spec((rows, DH)),
            _const_spec((rows, DH)),
            _const_spec((1, DH)),
            _const_spec((1, DH)),
            _const_spec(lbl.shape),
            _const_spec((D, D)),
            state_spec,
            state_spec,
        ],
        out_specs=[pl.BlockSpec((rows, D), lambda i: (i, 0)), state_spec, state_spec],
        out_shape=[jax.ShapeDtypeStruct((nb * seg, D), F32), state, state],
        scratch_shapes=[
            pltpu.VMEM((rows, IN_COLS), F32),
            pltpu.VMEM((rows, D), F32),
            pltpu.VMEM((rows, D), F32),
            pltpu.VMEM((rows, D), F32),
            pltpu.VMEM((rows, GROUP_W), F32),
            pltpu.VMEM((rows, D), BF16),
        ],
        compiler_params=pltpu.CompilerParams(
            dimension_semantics=("arbitrary",), vmem_limit_bytes=VMEM_LIMIT),
        name="mix_sample",
    )(x2, nw, win, cosf, sinf, rnw, hnw, lbl, wout, sret, shg)
    return x1, sret_new, shg_new


def _ffn_kernel(x_ref, nw_ref, wup_ref, wdn_ref, fw_ref, y_ref):
    x = x_ref[...]
    h = _rms(x, nw_ref[...]).astype(BF16)
    acc = x
    step = D_MODEL
    for g in range(D_FF // step):
        u = jnp.maximum(_dot(h, wup_ref[:, g * step:(g + 1) * step]), 0.0)
        acc = acc + _dot((u * u).astype(BF16), wdn_ref[g * step:(g + 1) * step, :])
    y_ref[...] = _rms(acc, fw_ref[...])


def _ffn(x, nw, wup, wdn, fw):
    n, D = x.shape
    rows = min(FFN_ROWS, n)
    return pl.pallas_call(
        _ffn_kernel,
        grid=(n // rows,),
        in_specs=[
            pl.BlockSpec((rows, D), lambda i: (i, 0)),
            _const_spec((1, D)),
            _const_spec((D, D_FF)),
            _const_spec((D_FF, D)),
            _const_spec((1, D)),
        ],
        out_specs=pl.BlockSpec((rows, D), lambda i: (i, 0)),
        out_shape=jax.ShapeDtypeStruct((n, D), F32),
        compiler_params=pltpu.CompilerParams(
            dimension_semantics=("arbitrary",), vmem_limit_bytes=VMEM_LIMIT),
        name="ffn",
    )(x, nw, wup, wdn, fw)


def _rope_tables(pos):
    half = DH // 2
    inv_freq = ROPE_BASE ** (-jnp.arange(half, dtype=F32) / half)
    ang = pos.astype(F32)[:, None] * inv_freq[None, :]
    cos, sin = jnp.cos(ang), jnp.sin(ang)
    return jnp.concatenate([cos, cos], axis=-1), jnp.concatenate([-sin, sin], axis=-1)


def kernel(x_prompt, x_sample, state_ret, state_hgrn, norm_mix_w, w_in, ret_norm_w, hgrn_norm_w,
           lb_logits, w_out, norm_ffn_w, w_up, w_down, final_norm_w):
    B, T, D = x_prompt.shape
    nb, seg, _ = x_sample.shape
    assert w_in.shape[0] == 1, "one layer"

    nw = norm_mix_w[0].reshape(1, D)
    win = w_in[0].astype(BF16)
    wout = w_out[0].astype(BF16)
    rnw = ret_norm_w[0].reshape(1, DH)
    hnw = hgrn_norm_w[0].reshape(1, DH)
    lbl = lb_logits.astype(F32)
    fnw = norm_ffn_w[0].reshape(1, D)
    wup = w_up[0].astype(BF16)
    wdn = w_down[0].astype(BF16)
    fw = final_norm_w.reshape(1, D)

    cos_p, sin_p = _rope_tables(jnp.arange(T, dtype=jnp.int32))
    cos_s, sin_s = _rope_tables(PAST_LEN + jnp.arange(seg, dtype=jnp.int32))
    cos_s = jnp.tile(cos_s, (SAMPLE_SEQS, 1))
    sin_s = jnp.tile(sin_s, (SAMPLE_SEQS, 1))

    xp1, sret_p, shg_p = _mix_prompt(x_prompt, nw, win, cos_p, sin_p, rnw, hnw, lbl, wout)
    xs1, sret_s, shg_s = _mix_sample(x_sample, nw, win, cos_s, sin_s, rnw, hnw, lbl, wout,
                                     state_ret[0], state_hgrn[0])

    y_p = _ffn(xp1.reshape(B * T, D), fnw, wup, wdn, fw).reshape(B, T, D)
    y_s = _ffn(xs1, fnw, wup, wdn, fw).reshape(nb, seg, D)
    return (y_p, y_s, sret_p[None], shg_p[None], sret_s[None], shg_s[None])
```

```python
import functools
import math

import jax
import jax.numpy as jnp
from jax import lax
from jax.experimental import pallas as pl
from jax.experimental.pallas import tpu as pltpu

D_MODEL = 1024
HEADS = 4
DH = 128
GROUP_W = HEADS * DH
IN_COLS = 8 * GROUP_W
D_FF = 4 * D_MODEL
ROPE_BASE = 10000.0
NORM_EPS = 1e-6
QK_SCALE = DH ** -0.5
PAST_LEN = 16384

PROMPT_CHUNK = 256
SAMPLE_ROWS = 256
SAMPLE_SUB_SEQS = 8
FFN_ROWS = 512
PROJ_PIECE = 512
VMEM_LIMIT = 56 * 1024 * 1024

F32 = jnp.float32
BF16 = jnp.bfloat16


def _dot(a, b):
    return jnp.dot(a, b, preferred_element_type=F32)


def _dot_nt(a, b):
    return lax.dot_general(a, b, (((1,), (1,)), ((), ())), preferred_element_type=F32)


def _dot_tn(a, b):
    return lax.dot_general(a, b, (((0,), (0,)), ((), ())), preferred_element_type=F32)


def _rms(x, w):
    ms = jnp.mean(x * x, axis=-1, keepdims=True)
    return x * lax.rsqrt(ms + NORM_EPS) * w


def _sigmoid(x):
    return 1.0 / (1.0 + jnp.exp(-x))


def _silu(x):
    return x * _sigmoid(x)


def _lower_bound(lbl):
    mx = jnp.max(lbl, axis=0, keepdims=True)
    e = jnp.exp(lbl - mx)
    return e[0:1, :] / jnp.sum(e, axis=0, keepdims=True)


def _rotary(x, cosf, sinf):
    return x * cosf + pltpu.roll(x, DH // 2, 1) * sinf


def _split_halves(x, m):
    blk = 2 * m
    nb = x.shape[0] // blk
    lo = [x[b * blk:b * blk + m] for b in range(nb)]
    up = [x[b * blk + m:(b + 1) * blk] for b in range(nb)]
    cat = lambda ps: ps[0] if len(ps) == 1 else jnp.concatenate(ps, axis=0)
    return cat(lo), cat(up)


def _merge_halves(lo, up, m):
    nb = lo.shape[0] // m
    pieces = []
    for b in range(nb):
        pieces += [lo[b * m:(b + 1) * m], up[b * m:(b + 1) * m]]
    return jnp.concatenate(pieces, axis=0)


def _block_row_bcast(x, m, row):
    nb = x.shape[0] // m
    pieces = [jnp.broadcast_to(x[b * m + row:b * m + row + 1], (m, x.shape[1])) for b in range(nb)]
    return pieces[0] if nb == 1 else jnp.concatenate(pieces, axis=0)


def _hgrn_level_masks(rows, seg):
    masks = []
    for j in range(int(math.log2(seg))):
        if (1 << j) < 8:
            r = lax.broadcasted_iota(jnp.int32, (rows, rows), 0)
            s = lax.broadcasted_iota(jnp.int32, (rows, rows), 1)
            masks.append(jnp.logical_and(((r ^ s) >> j) == 1, r > s))
        else:
            r = lax.broadcasted_iota(jnp.int32, (rows // 2, rows), 0)
            s = lax.broadcasted_iota(jnp.int32, (rows // 2, rows), 1)
            masks.append((s >> j) == 2 * (r >> j))
    return masks


def _hgrn_intra(q, k, f, v, seg, t, masks, mid_hook=None):
    rows = q.shape[0]
    c = jnp.log2(f)
    scores = None
    for j in range(int(math.log2(seg))):
        m = 1 << j
        if m == 8 and mid_hook is not None:
            mid_hook()
        if m < 8:
            c3 = c.reshape(rows // 8, 8, DH)
            upper = (t & m) != 0
            if m == 1:
                w = jnp.where(upper, f, 1.0)
                tot = pltpu.roll(c3, 1, 1).reshape(rows, DH)
            else:
                if m == 2:
                    low4 = ((t & 7) < 4).reshape(rows // 8, 8, DH)
                    tot3 = jnp.where(low4, jnp.broadcast_to(c3[:, 1:2, :], c3.shape),
                                     jnp.broadcast_to(c3[:, 5:6, :], c3.shape))
                else:
                    tot3 = jnp.broadcast_to(c3[:, 3:4, :], c3.shape)
                tot = tot3.reshape(rows, DH)
                w = jnp.exp2(jnp.where(upper, c, tot - c))
            part = _dot_nt((q * w).astype(BF16), (k * w).astype(BF16))
            scores = jnp.where(masks[j], part, 0.0 if scores is None else scores)
            c = jnp.where(upper, c + tot, c)
        else:
            c_lo, c_up = _split_halves(c, m)
            tot = _block_row_bcast(c_lo, m, m - 1)
            q_up = _split_halves(q, m)[1]
            k_lo, k_up = _split_halves(k, m)
            qt = (q_up * jnp.exp2(c_up)).astype(BF16)
            kt = _merge_halves(k_lo * jnp.exp2(tot - c_lo), k_up, m).astype(BF16)
            s_lo, s_up = _split_halves(scores, m)
            s_up = jnp.where(masks[j], _dot_nt(qt, kt), s_up)
            scores = _merge_halves(s_lo, s_up, m)
            c = _merge_halves(c_lo, c_up + tot, m)
    o = _dot(scores.astype(BF16), v.astype(BF16))
    o = o + jnp.sum(q * k, axis=-1, keepdims=True) * v
    return o, c


def _ret_mask(rows, seg, log_gamma):
    r = lax.broadcasted_iota(jnp.int32, (rows, rows), 0)
    s = lax.broadcasted_iota(jnp.int32, (rows, rows), 1)
    valid = jnp.logical_and(((r ^ s) >> int(math.log2(seg))) == 0, r >= s)
    d = jnp.where(valid, r - s, 0).astype(F32)
    return jnp.where(valid, jnp.exp(d * log_gamma) * QK_SCALE, 0.0)


def _log_gamma(hd):
    return math.log(1.0 - 2.0 ** (-5.0 - hd))


def _head_out(o, w, gate):
    return o * lax.rsqrt(jnp.mean(o * o, axis=-1, keepdims=True) + NORM_EPS) * w * _silu(gate)


def _project(x_ref, nw_ref, win_ref, proj_ref):
    h = _rms(x_ref[...], nw_ref[...]).astype(BF16)
    for g in range(8):
        cols = slice(g * GROUP_W, (g + 1) * GROUP_W)
        proj_ref[:, cols] = _dot(h, win_ref[:, cols])


def _cols(group, hd):
    return slice(group * GROUP_W + hd * DH, group * GROUP_W + (hd + 1) * DH)


def _mix_prompt_kernel(nc, xn_ref, xc_ref, nw_ref, win_ref, cos_ref, sin_ref, rnw_ref, hnw_ref,
                       lbl_ref, wout_ref,
                       x1_ref, sret_ref, shg_ref,
                       proj_ref, o_ref, srt_ref, sht_ref, dm_ref, qd_ref, kd_ref):
    L = PROMPT_CHUNK
    g = pl.program_id(0)
    ci = (jnp.maximum(g, 1) - 1) % nc
    slot = g % 2
    t = lax.broadcasted_iota(jnp.int32, (L, DH), 0)

    @pl.when(g == 0)
    def _():
        tf = t.astype(F32)
        for hd in range(HEADS):
            lg = _log_gamma(hd)
            dm_ref[hd] = _ret_mask(L, L, lg)
            qd_ref[hd] = jnp.exp((tf + 1.0) * lg)
            kd_ref[hd] = jnp.exp((L - 1.0 - tf) * lg) * QK_SCALE
        proj_ref[1] = jnp.zeros((L, IN_COLS), F32)

    @pl.when(ci == 0)
    def _():
        srt_ref[...] = jnp.zeros_like(srt_ref)
        sht_ref[...] = jnp.zeros_like(sht_ref)

    def stages(pw, pr):
        h_next = _rms(xn_ref[...], nw_ref[...]).astype(BF16)

        def project_piece(pi):
            cols = slice(pi * PROJ_PIECE, (pi + 1) * PROJ_PIECE)
            pw[:, cols] = _dot(h_next, win_ref[:, cols])

        cosf = cos_ref[...]
        sinf = sin_ref[...]
        lb = _lower_bound(lbl_ref[...])
        masks = _hgrn_level_masks(L, L)

        for hd in range(HEADS):
            project_piece(hd)
            q = _rotary(pr[:, _cols(0, hd)], cosf, sinf)
            k = _rotary(pr[:, _cols(1, hd)], cosf, sinf)
            v = pr[:, _cols(2, hd)].astype(BF16)
            scores = _dot_nt(q.astype(BF16), k.astype(BF16)) * dm_ref[hd]
            o = _dot(scores.astype(BF16), v)
            st = srt_ref[hd]
            o = o + _dot_nt((q * qd_ref[hd]).astype(BF16), st.astype(BF16))
            kh = (k * kd_ref[hd]).astype(BF16)
            srt_ref[hd] = st * math.exp(L * _log_gamma(hd)) + _dot_tn(v, kh)
            o_ref[:, _cols(0, hd)] = _head_out(o, rnw_ref[...], pr[:, _cols(3, hd)]).astype(BF16)

        for hd in range(HEADS):
            project_piece(HEADS + hd)
            q = _silu(pr[:, _cols(4, hd)]) * QK_SCALE
            lbh = lb[:, hd * DH:(hd + 1) * DH]
            f = lbh + (1.0 - lbh) * _sigmoid(pr[:, _cols(5, hd)])
            k = 1.0 - f
            v = pr[:, _cols(6, hd)]
            o, b = _hgrn_intra(q, k, f, v, L, t, masks)
            blast = b[L - 1:L, :]
            st = sht_ref[hd]
            o = o + _dot_nt((q * jnp.exp2(b)).astype(BF16), st.astype(BF16))
            kh = (k * jnp.exp2(blast - b)).astype(BF16)
            sht_ref[hd] = st * jnp.exp2(blast) + _dot_tn(v.astype(BF16), kh)
            o_ref[:, _cols(1, hd)] = _head_out(o, hnw_ref[...], pr[:, _cols(7, hd)]).astype(BF16)

        x1_ref[...] = xc_ref[...] + _dot(o_ref[...], wout_ref[...])

    @pl.when(slot == 0)
    def _():
        stages(proj_ref.at[0], proj_ref.at[1])

    @pl.when(slot == 1)
    def _():
        stages(proj_ref.at[1], proj_ref.at[0])

    @pl.when(ci == nc - 1)
    def _():
        for hd in range(HEADS):
            sret_ref[hd] = srt_ref[hd].T
            shg_ref[hd] = sht_ref[hd].T


def _const_spec(shape):
    return pl.BlockSpec(shape, lambda *_: (0,) * len(shape))


def _mix_prompt(x, nw, win, cosf, sinf, rnw, hnw, lbl, wout):
    B, T, D = x.shape
    L = PROMPT_CHUNK
    nc = T // L
    n = B * nc
    nxt = lambda g: jnp.minimum(g, n - 1)
    cur = lambda g: jnp.maximum(g, 1) - 1
    state = jax.ShapeDtypeStruct((B, HEADS, DH, DH), F32)
    state_spec = pl.BlockSpec((None, HEADS, DH, DH), lambda g: (cur(g) // nc, 0, 0, 0))
    return pl.pallas_call(
        functools.partial(_mix_prompt_kernel, nc),
        grid=(n + 1,),
        in_specs=[
            pl.BlockSpec((None, L, D), lambda g: (nxt(g) // nc, nxt(g) % nc, 0)),
            pl.BlockSpec((None, L, D), lambda g: (cur(g) // nc, cur(g) % nc, 0)),
            _const_spec((1, D)),
            _const_spec((D, IN_COLS)),
            pl.BlockSpec((L, DH), lambda g: (cur(g) % nc, 0)),
            pl.BlockSpec((L, DH), lambda g: (cur(g) % nc, 0)),
            _const_spec((1, DH)),
            _const_spec((1, DH)),
            _const_spec(lbl.shape),
            _const_spec((D, D)),
        ],
        out_specs=[pl.BlockSpec((None, L, D), lambda g: (cur(g) // nc, cur(g) % nc, 0)),
                   state_spec, state_spec],
        out_shape=[jax.ShapeDtypeStruct((B, T, D), F32), state, state],
        scratch_shapes=[
            pltpu.VMEM((2, L, IN_COLS), F32),
            pltpu.VMEM((L, D), BF16),
            pltpu.VMEM((HEADS, DH, DH), F32),
            pltpu.VMEM((HEADS, DH, DH), F32),
            pltpu.VMEM((HEADS, L, L), F32),
            pltpu.VMEM((HEADS, L, DH), F32),
            pltpu.VMEM((HEADS, L, DH), F32),
        ],
        compiler_params=pltpu.CompilerParams(
            dimension_semantics=("arbitrary",), vmem_limit_bytes=VMEM_LIMIT),
        name="mix_prompt",
    )(x, x, nw, win, cosf, sinf, rnw, hnw, lbl, wout)


def _mix_sample_kernel(seg, x_ref, nw_ref, win_ref, cos_ref, sin_ref, rnw_ref, hnw_ref, lbl_ref,
                       wout_ref, sret_in_ref, shg_in_ref,
                       x1_ref, sret_ref, shg_ref,
                       proj_ref, o_ref, qh_ref, kh_ref, dec_ref, ob_ref):
    rows = x_ref.shape[0]
    nsub = sret_in_ref.shape[0]
    j = pl.program_id(1)

    @pl.when(j == 0)
    def _():
        _project(x_ref, nw_ref, win_ref, proj_ref)
        t = lax.broadcasted_iota(jnp.int32, (rows, DH), 0)
        tl = (t & (seg - 1)).astype(F32)
        cosf = cos_ref[...]
        sinf = sin_ref[...]
        lb = _lower_bound(lbl_ref[...])
        masks = _hgrn_level_masks(rows, seg)
        for hd in range(HEADS):
            lg = _log_gamma(hd)
            q = _rotary(proj_ref[:, _cols(0, hd)], cosf, sinf)
            k = _rotary(proj_ref[:, _cols(1, hd)], cosf, sinf)
            scores = _dot_nt(q.astype(BF16), k.astype(BF16)) * _ret_mask(rows, seg, lg)
            o_ref[:, _cols(0, hd)] = _dot(scores.astype(BF16),
                                          proj_ref[:, _cols(2, hd)].astype(BF16))
            qh_ref[:, _cols(0, hd)] = q * jnp.exp((tl + 1.0) * lg)
            kh_ref[:, _cols(0, hd)] = k * (jnp.exp((seg - 1.0 - tl) * lg) * QK_SCALE)
        for hd in range(HEADS):
            q = _silu(proj_ref[:, _cols(4, hd)]) * QK_SCALE
            lbh = lb[:, hd * DH:(hd + 1) * DH]
            f = lbh + (1.0 - lbh) * _sigmoid(proj_ref[:, _cols(5, hd)])
            k = 1.0 - f
            o, b = _hgrn_intra(q, k, f, proj_ref[:, _cols(6, hd)], seg, t, masks)
            blast = _block_row_bcast(b, seg, seg - 1)
            o_ref[:, _cols(1, hd)] = o
            qh_ref[:, _cols(1, hd)] = q * jnp.exp2(b)
            kh_ref[:, _cols(1, hd)] = k * jnp.exp2(blast - b)
            dec_ref[:, hd * DH:(hd + 1) * DH] = jnp.exp2(blast)

    def seq_body(s, carry):
        r = pl.ds(pl.multiple_of((j * nsub + s) * seg, seg), seg)
        qh = qh_ref[r, :].astype(BF16)
        kh = kh_ref[r, :].astype(BF16)
        v_ret = proj_ref[r, 2 * GROUP_W:3 * GROUP_W].astype(BF16)
        v_hg = proj_ref[r, 6 * GROUP_W:7 * GROUP_W].astype(BF16)
        dec = dec_ref[r, :][0:1, :]
        o_parts, new_ret, new_hg = [], [], []
        for hd in range(HEADS):
            st = sret_in_ref[s, hd]
            c = slice(hd * DH, (hd + 1) * DH)
            o_parts.append(_dot(qh[:, c], st.astype(BF16)))
            new_ret.append(st * math.exp(seg * _log_gamma(hd)) + _dot_tn(kh[:, c], v_ret[:, c]))
        for hd in range(HEADS):
            st = shg_in_ref[s, hd]
            c = slice(hd * DH, (hd + 1) * DH)
            c1 = slice(GROUP_W + hd * DH, GROUP_W + (hd + 1) * DH)
            o_parts.append(_dot(qh[:, c1], st.astype(BF16)))
            dcol = jnp.broadcast_to(dec[:, c], (DH, DH)).T
            new_hg.append(st * dcol + _dot_tn(kh[:, c1], v_hg[:, c]))
        o_ref[r, :] = o_ref[r, :] + jnp.concatenate(o_parts, axis=1)
        for hd in range(HEADS):
            sret_ref[s, hd] = new_ret[hd]
            shg_ref[s, hd] = new_hg[hd]
        return carry

    lax.fori_loop(0, nsub, seq_body, 0, unroll=2)

    @pl.when(j == pl.num_programs(1) - 1)
    def _():
        for hd in range(HEADS):
            c0 = _cols(0, hd)
            ob_ref[:, c0] = _head_out(o_ref[:, c0], rnw_ref[...],
                                      proj_ref[:, _cols(3, hd)]).astype(BF16)
            c1 = _cols(1, hd)
            ob_ref[:, c1] = _head_out(o_ref[:, c1], hnw_ref[...],
                                      proj_ref[:, _cols(7, hd)]).astype(BF16)
        x1_ref[...] = x_ref[...] + _dot(ob_ref[...], wout_ref[...])


def _mix_sample(x, nw, win, cosf, sinf, rnw, hnw, lbl, wout, sret, shg):
    nb, seg, D = x.shape
    rows = SAMPLE_ROWS
    nsub = SAMPLE_SUB_SEQS
    nj = rows // seg // nsub
    x2 = x.reshape(nb * seg, D)
    state = jax.ShapeDtypeStruct((nb, HEADS, DH, DH), F32)
    state_spec = pl.BlockSpec((nsub, HEADS, DH, DH), lambda i, j: (i * nj + j, 0, 0, 0))
    x1, sret_new, shg_new = pl.pallas_call(
        functools.partial(_mix_sample_kernel, seg),
        grid=(nb * seg // rows, nj),
        in_specs=[
            pl.BlockSpec((rows, D), lambda i, j: (i, 0)),
            _const_spec((1, D)),
            _const_spec((D, IN_COLS)),
            _const_spec((rows, DH)),
            _const_spec((rows, DH)),
            _const_spec((1, DH)),
            _const_spec((1, DH)),
            _const_spec(lbl.shape),
            _const_spec((D, D)),
            state_spec,
            state_spec,
        ],
        out_specs=[pl.BlockSpec((rows, D), lambda i, j: (i, 0)), state_spec, state_spec],
        out_shape=[jax.ShapeDtypeStruct((nb * seg, D), F32), state, state],
        scratch_shapes=[
            pltpu.VMEM((rows, IN_COLS), F32),
            pltpu.VMEM((rows, D), F32),
            pltpu.VMEM((rows, D), F32),
            pltpu.VMEM((rows, D), F32),
            pltpu.VMEM((rows, GROUP_W), F32),
            pltpu.VMEM((rows, D), BF16),
        ],
        compiler_params=pltpu.CompilerParams(
            dimension_semantics=("arbitrary", "arbitrary"), vmem_limit_bytes=VMEM_LIMIT),
        name="mix_sample",
    )(x2, nw, win, cosf, sinf, rnw, hnw, lbl, wout, sret, shg)
    return x1, sret_new, shg_new


def _ffn_kernel(x_ref, nw_ref, wup_ref, wdn_ref, fw_ref, y_ref):
    x = x_ref[...]
    h = _rms(x, nw_ref[...]).astype(BF16)
    acc = x
    step = D_MODEL
    for g in range(D_FF // step):
        u = jnp.maximum(_dot(h, wup_ref[:, g * step:(g + 1) * step]), 0.0)
        acc = acc + _dot((u * u).astype(BF16), wdn_ref[g * step:(g + 1) * step, :])
    y_ref[...] = _rms(acc, fw_ref[...])


def _ffn(x, nw, wup, wdn, fw):
    n, D = x.shape
    rows = min(FFN_ROWS, n)
    return pl.pallas_call(
        _ffn_kernel,
        grid=(n // rows,),
        in_specs=[
            pl.BlockSpec((rows, D), lambda i: (i, 0)),
            _const_spec((1, D)),
            _const_spec((D, D_FF)),
            _const_spec((D_FF, D)),
            _const_spec((1, D)),
        ],
        out_specs=pl.BlockSpec((rows, D), lambda i: (i, 0)),
        out_shape=jax.ShapeDtypeStruct((n, D), F32),
        compiler_params=pltpu.CompilerParams(
            dimension_semantics=("arbitrary",), vmem_limit_bytes=VMEM_LIMIT),
        name="ffn",
    )(x, nw, wup, wdn, fw)


def _rope_tables(pos):
    half = DH // 2
    inv_freq = ROPE_BASE ** (-jnp.arange(half, dtype=F32) / half)
    ang = pos.astype(F32)[:, None] * inv_freq[None, :]
    cos, sin = jnp.cos(ang), jnp.sin(ang)
    return jnp.concatenate([cos, cos], axis=-1), jnp.concatenate([-sin, sin], axis=-1)


def kernel(x_prompt, x_sample, state_ret, state_hgrn, norm_mix_w, w_in, ret_norm_w, hgrn_norm_w,
           lb_logits, w_out, norm_ffn_w, w_up, w_down, final_norm_w):
    B, T, D = x_prompt.shape
    nb, seg, _ = x_sample.shape
    assert w_in.shape[0] == 1, "one layer"

    nw = norm_mix_w[0].reshape(1, D)
    win = w_in[0].astype(BF16)
    wout = w_out[0].astype(BF16)
    rnw = ret_norm_w[0].reshape(1, DH)
    hnw = hgrn_norm_w[0].reshape(1, DH)
    lbl = lb_logits.astype(F32)
    fnw = norm_ffn_w[0].reshape(1, D)
    wup = w_up[0].astype(BF16)
    wdn = w_down[0].astype(BF16)
    fw = final_norm_w.reshape(1, D)

    cos_p, sin_p = _rope_tables(jnp.arange(T, dtype=jnp.int32))
    cos_s, sin_s = _rope_tables(PAST_LEN + jnp.arange(seg, dtype=jnp.int32))
    cos_s = jnp.tile(cos_s, (SAMPLE_ROWS // seg, 1))
    sin_s = jnp.tile(sin_s, (SAMPLE_ROWS // seg, 1))

    xp1, sret_p, shg_p = _mix_prompt(x_prompt, nw, win, cos_p, sin_p, rnw, hnw, lbl, wout)
    xs1, sret_s, shg_s = _mix_sample(x_sample, nw, win, cos_s, sin_s, rnw, hnw, lbl, wout,
                                     state_ret[0], state_hgrn[0])

    y_p = _ffn(xp1.reshape(B * T, D), fnw, wup, wdn, fw).reshape(B, T, D)
    y_s = _ffn(xs1, fnw, wup, wdn, fw).reshape(nb, seg, D)
    return (y_p, y_s, sret_p[None], shg_p[None], sret_s[None], shg_s[None])
```

```python
import functools
import math

import jax
import jax.numpy as jnp
from jax import lax
from jax.experimental import pallas as pl
from jax.experimental.pallas import tpu as pltpu

D_MODEL = 1024
HEADS = 4
DH = 128
GROUP_W = HEADS * DH
IN_COLS = 8 * GROUP_W
D_FF = 4 * D_MODEL
ROPE_BASE = 10000.0
NORM_EPS = 1e-6
QK_SCALE = DH ** -0.5
PAST_LEN = 16384

PROMPT_CHUNK = 256
SAMPLE_ROWS = 256
SAMPLE_SUB_SEQS = 8
FFN_ROWS = 512
PROJ_PIECE = 512
VMEM_LIMIT = 56 * 1024 * 1024

F32 = jnp.float32
BF16 = jnp.bfloat16


def _dot(a, b):
    return jnp.dot(a, b, preferred_element_type=F32)


def _dot_nt(a, b):
    return jnp.dot(a, b.T, preferred_element_type=F32)


def _dot_tn(a, b):
    return lax.dot_general(a, b, (((0,), (0,)), ((), ())), preferred_element_type=F32)


def _rms(x, w):
    ms = jnp.mean(x * x, axis=-1, keepdims=True)
    return x * lax.rsqrt(ms + NORM_EPS) * w


def _sigmoid(x):
    return 1.0 / (1.0 + jnp.exp(-x))


def _silu(x):
    return x * _sigmoid(x)


def _lower_bound(lbl):
    mx = jnp.max(lbl, axis=0, keepdims=True)
    e = jnp.exp(lbl - mx)
    return e[0:1, :] / jnp.sum(e, axis=0, keepdims=True)


def _rotary(x, cosf, sinf):
    return x * cosf + pltpu.roll(x, DH // 2, 1) * sinf


def _split_halves(x, m):
    blk = 2 * m
    nb = x.shape[0] // blk
    lo = [x[b * blk:b * blk + m] for b in range(nb)]
    up = [x[b * blk + m:(b + 1) * blk] for b in range(nb)]
    cat = lambda ps: ps[0] if len(ps) == 1 else jnp.concatenate(ps, axis=0)
    return cat(lo), cat(up)


def _merge_halves(lo, up, m):
    nb = lo.shape[0] // m
    pieces = []
    for b in range(nb):
        pieces += [lo[b * m:(b + 1) * m], up[b * m:(b + 1) * m]]
    return jnp.concatenate(pieces, axis=0)


def _block_row_bcast(x, m, row):
    nb = x.shape[0] // m
    pieces = [jnp.broadcast_to(x[b * m + row:b * m + row + 1], (m, x.shape[1])) for b in range(nb)]
    return pieces[0] if nb == 1 else jnp.concatenate(pieces, axis=0)


def _hgrn_level_masks(rows, seg):
    masks = []
    for j in range(int(math.log2(seg))):
        if (1 << j) < 8:
            r = lax.broadcasted_iota(jnp.int32, (rows, rows), 0)
            s = lax.broadcasted_iota(jnp.int32, (rows, rows), 1)
            masks.append(jnp.logical_and(((r ^ s) >> j) == 1, r > s))
        else:
            r = lax.broadcasted_iota(jnp.int32, (rows // 2, rows), 0)
            s = lax.broadcasted_iota(jnp.int32, (rows // 2, rows), 1)
            masks.append((s >> j) == 2 * (r >> j))
    return masks


def _hgrn_intra(q, k, f, v, seg, t, masks, mid_hook=None):
    rows = q.shape[0]
    c = jnp.log2(f)
    scores = None
    for j in range(int(math.log2(seg))):
        m = 1 << j
        if m < 8:
            c3 = c.reshape(rows // 8, 8, DH)
            upper = (t & m) != 0
            if m == 1:
                w = jnp.where(upper, f, 1.0)
                tot = pltpu.roll(c3, 1, 1).reshape(rows, DH)
            else:
                if m == 2:
                    low4 = ((t & 7) < 4).reshape(rows // 8, 8, DH)
                    tot3 = jnp.where(low4, jnp.broadcast_to(c3[:, 1:2, :], c3.shape),
                                     jnp.broadcast_to(c3[:, 5:6, :], c3.shape))
                else:
                    tot3 = jnp.broadcast_to(c3[:, 3:4, :], c3.shape)
                tot = tot3.reshape(rows, DH)
                w = jnp.exp2(jnp.where(upper, c, tot - c))
            part = _dot_nt((q * w).astype(BF16), (k * w).astype(BF16))
            scores = jnp.where(masks[j], part, 0.0 if scores is None else scores)
            c = jnp.where(upper, c + tot, c)
        else:
            c_lo, c_up = _split_halves(c, m)
            tot = _block_row_bcast(c_lo, m, m - 1)
            q_up = _split_halves(q, m)[1]
            k_lo, k_up = _split_halves(k, m)
            qt = (q_up * jnp.exp2(c_up)).astype(BF16)
            kt = _merge_halves(k_lo * jnp.exp2(tot - c_lo), k_up, m).astype(BF16)
            s_lo, s_up = _split_halves(scores, m)
            s_up = jnp.where(masks[j], _dot_nt(qt, kt), s_up)
            scores = _merge_halves(s_lo, s_up, m)
            c = _merge_halves(c_lo, c_up + tot, m)
    if mid_hook is not None:
        mid_hook()
    o = _dot(scores.astype(BF16), v.astype(BF16))
    o = o + jnp.sum(q * k, axis=-1, keepdims=True) * v
    return o, c


def _ret_mask(rows, seg, log_gamma):
    r = lax.broadcasted_iota(jnp.int32, (rows, rows), 0)
    s = lax.broadcasted_iota(jnp.int32, (rows, rows), 1)
    valid = jnp.logical_and(((r ^ s) >> int(math.log2(seg))) == 0, r >= s)
    d = jnp.where(valid, r - s, 0).astype(F32)
    return jnp.where(valid, jnp.exp(d * log_gamma) * QK_SCALE, 0.0)


def _log_gamma(hd):
    return math.log(1.0 - 2.0 ** (-5.0 - hd))


def _head_out(o, w, gate):
    return o * lax.rsqrt(jnp.mean(o * o, axis=-1, keepdims=True) + NORM_EPS) * w * _silu(gate)


def _project(x_ref, nw_ref, win_ref, proj_ref):
    h = _rms(x_ref[...], nw_ref[...]).astype(BF16)
    for g in range(8):
        cols = slice(g * GROUP_W, (g + 1) * GROUP_W)
        proj_ref[:, cols] = _dot(h, win_ref[:, cols])


def _cols(group, hd):
    return slice(group * GROUP_W + hd * DH, group * GROUP_W + (hd + 1) * DH)


def _mix_prompt_kernel(nc, xn_ref, xc_ref, nw_ref, win_ref, cos_ref, sin_ref, rnw_ref, hnw_ref,
                       lbl_ref, wout_ref,
                       x1_ref, sret_ref, shg_ref,
                       proj_ref, o_ref, srt_ref, sht_ref, dm_ref, qd_ref, kd_ref):
    L = PROMPT_CHUNK
    g = pl.program_id(0)
    ci = (jnp.maximum(g, 1) - 1) % nc
    slot = g % 2
    t = lax.broadcasted_iota(jnp.int32, (L, DH), 0)

    @pl.when(g == 0)
    def _():
        tf = t.astype(F32)
        for hd in range(HEADS):
            lg = _log_gamma(hd)
            dm_ref[hd] = _ret_mask(L, L, lg)
            qd_ref[hd] = jnp.exp((tf + 1.0) * lg)
            kd_ref[hd] = jnp.exp((L - 1.0 - tf) * lg) * QK_SCALE
        proj_ref[1] = jnp.zeros((L, IN_COLS), F32)
        o_ref[...] = jnp.zeros_like(o_ref)

    @pl.when(ci == 0)
    def _():
        srt_ref[...] = jnp.zeros_like(srt_ref)
        sht_ref[...] = jnp.zeros_like(sht_ref)

    def stages(pw, pr):
        x1_ref[...] = xc_ref[...] + _dot(o_ref[...], wout_ref[...])

        h_next = _rms(xn_ref[...], nw_ref[...]).astype(BF16)

        def project_piece(pi):
            cols = slice(pi * PROJ_PIECE, (pi + 1) * PROJ_PIECE)
            pw[:, cols] = _dot(h_next, win_ref[:, cols])

        cosf = cos_ref[...]
        sinf = sin_ref[...]
        lb = _lower_bound(lbl_ref[...])
        masks = _hgrn_level_masks(L, L)

        for hd in range(HEADS):
            q = _rotary(pr[:, _cols(0, hd)], cosf, sinf)
            k = _rotary(pr[:, _cols(1, hd)], cosf, sinf)
            v = pr[:, _cols(2, hd)].astype(BF16)
            scores = _dot_nt(q.astype(BF16), k.astype(BF16)) * dm_ref[hd]
            project_piece(hd)
            o = _dot(scores.astype(BF16), v)
            st = srt_ref[hd]
            o = o + _dot_nt((q * qd_ref[hd]).astype(BF16), st.astype(BF16))
            kh = (k * kd_ref[hd]).astype(BF16)
            srt_ref[hd] = st * math.exp(L * _log_gamma(hd)) + _dot_tn(v, kh)
            o_ref[:, _cols(0, hd)] = _head_out(o, rnw_ref[...], pr[:, _cols(3, hd)]).astype(BF16)

        for hd in range(HEADS):
            q = _silu(pr[:, _cols(4, hd)]) * QK_SCALE
            lbh = lb[:, hd * DH:(hd + 1) * DH]
            f = lbh + (1.0 - lbh) * _sigmoid(pr[:, _cols(5, hd)])
            k = 1.0 - f
            v = pr[:, _cols(6, hd)]
            o, b = _hgrn_intra(q, k, f, v, L, t, masks,
                               functools.partial(project_piece, HEADS + hd))
            blast = b[L - 1:L, :]
            st = sht_ref[hd]
            o = o + _dot_nt((q * jnp.exp2(b)).astype(BF16), st.astype(BF16))
            kh = (k * jnp.exp2(blast - b)).astype(BF16)
            sht_ref[hd] = st * jnp.exp2(blast) + _dot_tn(v.astype(BF16), kh)
            o_ref[:, _cols(1, hd)] = _head_out(o, hnw_ref[...], pr[:, _cols(7, hd)]).astype(BF16)

    @pl.when(slot == 0)
    def _():
        stages(proj_ref.at[0], proj_ref.at[1])

    @pl.when(slot == 1)
    def _():
        stages(proj_ref.at[1], proj_ref.at[0])

    @pl.when(ci == nc - 1)
    def _():
        for hd in range(HEADS):
            sret_ref[hd] = srt_ref[hd].T
            shg_ref[hd] = sht_ref[hd].T


def _const_spec(shape):
    return pl.BlockSpec(shape, lambda *_: (0,) * len(shape), pipeline_mode=pl.Buffered(1))


def _mix_prompt(x, nw, win, cosf, sinf, rnw, hnw, lbl, wout):
    B, T, D = x.shape
    L = PROMPT_CHUNK
    nc = T // L
    n = B * nc
    nxt = lambda g: jnp.minimum(g, n - 1)
    cur = lambda g: jnp.clip(g - 1, 0, n - 1)
    prv = lambda g: jnp.clip(g - 2, 0, n - 1)
    state = jax.ShapeDtypeStruct((B, HEADS, DH, DH), F32)
    state_spec = pl.BlockSpec((None, HEADS, DH, DH), lambda g: (cur(g) // nc, 0, 0, 0))
    return pl.pallas_call(
        functools.partial(_mix_prompt_kernel, nc),
        grid=(n + 2,),
        in_specs=[
            pl.BlockSpec((None, L, D), lambda g: (nxt(g) // nc, nxt(g) % nc, 0)),
            pl.BlockSpec((None, L, D), lambda g: (prv(g) // nc, prv(g) % nc, 0)),
            _const_spec((1, D)),
            _const_spec((D, IN_COLS)),
            pl.BlockSpec((L, DH), lambda g: (cur(g) % nc, 0)),
            pl.BlockSpec((L, DH), lambda g: (cur(g) % nc, 0)),
            _const_spec((1, DH)),
            _const_spec((1, DH)),
            _const_spec(lbl.shape),
            _const_spec((D, D)),
        ],
        out_specs=[pl.BlockSpec((None, L, D), lambda g: (prv(g) // nc, prv(g) % nc, 0)),
                   state_spec, state_spec],
        out_shape=[jax.ShapeDtypeStruct((B, T, D), F32), state, state],
        scratch_shapes=[
            pltpu.VMEM((2, L, IN_COLS), F32),
            pltpu.VMEM((L, D), BF16),
            pltpu.VMEM((HEADS, DH, DH), F32),
            pltpu.VMEM((HEADS, DH, DH), F32),
            pltpu.VMEM((HEADS, L, L), F32),
            pltpu.VMEM((HEADS, L, DH), F32),
            pltpu.VMEM((HEADS, L, DH), F32),
        ],
        compiler_params=pltpu.CompilerParams(
            dimension_semantics=("arbitrary",), vmem_limit_bytes=VMEM_LIMIT),
        name="mix_prompt",
    )(x, x, nw, win, cosf, sinf, rnw, hnw, lbl, wout)


def _mix_sample_kernel(seg, x_ref, nw_ref, win_ref, cos_ref, sin_ref, rnw_ref, hnw_ref, lbl_ref,
                       wout_ref, sret_in_ref, shg_in_ref,
                       x1_ref, sret_ref, shg_ref,
                       proj_ref, o_ref, qh_ref, kh_ref, dec_ref, ob_ref):
    rows = x_ref.shape[0]
    nsub = sret_in_ref.shape[0]
    j = pl.program_id(1)

    @pl.when(j == 0)
    def _():
        _project(x_ref, nw_ref, win_ref, proj_ref)
        t = lax.broadcasted_iota(jnp.int32, (rows, DH), 0)
        tl = (t & (seg - 1)).astype(F32)
        cosf = cos_ref[...]
        sinf = sin_ref[...]
        lb = _lower_bound(lbl_ref[...])
        masks = _hgrn_level_masks(rows, seg)
        for hd in range(HEADS):
            lg = _log_gamma(hd)
            q = _rotary(proj_ref[:, _cols(0, hd)], cosf, sinf)
            k = _rotary(proj_ref[:, _cols(1, hd)], cosf, sinf)
            scores = _dot_nt(q.astype(BF16), k.astype(BF16)) * _ret_mask(rows, seg, lg)
            o_ref[:, _cols(0, hd)] = _dot(scores.astype(BF16),
                                          proj_ref[:, _cols(2, hd)].astype(BF16))
            qh_ref[:, _cols(0, hd)] = q * jnp.exp((tl + 1.0) * lg)
            kh_ref[:, _cols(0, hd)] = k * (jnp.exp((seg - 1.0 - tl) * lg) * QK_SCALE)
        for hd in range(HEADS):
            q = _silu(proj_ref[:, _cols(4, hd)]) * QK_SCALE
            lbh = lb[:, hd * DH:(hd + 1) * DH]
            f = lbh + (1.0 - lbh) * _sigmoid(proj_ref[:, _cols(5, hd)])
            k = 1.0 - f
            o, b = _hgrn_intra(q, k, f, proj_ref[:, _cols(6, hd)], seg, t, masks)
            blast = _block_row_bcast(b, seg, seg - 1)
            o_ref[:, _cols(1, hd)] = o
            qh_ref[:, _cols(1, hd)] = q * jnp.exp2(b)
            kh_ref[:, _cols(1, hd)] = k * jnp.exp2(blast - b)
            dec_ref[:, hd * DH:(hd + 1) * DH] = jnp.exp2(blast)

    def seq_body(s, carry):
        r = pl.ds(pl.multiple_of((j * nsub + s) * seg, seg), seg)
        qh = qh_ref[r, :].astype(BF16)
        kh = kh_ref[r, :].astype(BF16)
        v_ret = proj_ref[r, 2 * GROUP_W:3 * GROUP_W].astype(BF16)
        v_hg = proj_ref[r, 6 * GROUP_W:7 * GROUP_W].astype(BF16)
        dec = dec_ref[r, :][0:1, :]
        o_parts, new_ret, new_hg = [], [], []
        for hd in range(HEADS):
            st = sret_in_ref[s, hd]
            c = slice(hd * DH, (hd + 1) * DH)
            o_parts.append(_dot(qh[:, c], st.astype(BF16)))
            new_ret.append(st * math.exp(seg * _log_gamma(hd)) + _dot_tn(kh[:, c], v_ret[:, c]))
        for hd in range(HEADS):
            st = shg_in_ref[s, hd]
            c = slice(hd * DH, (hd + 1) * DH)
            c1 = slice(GROUP_W + hd * DH, GROUP_W + (hd + 1) * DH)
            o_parts.append(_dot(qh[:, c1], st.astype(BF16)))
            dcol = jnp.broadcast_to(dec[:, c], (DH, DH)).T
            new_hg.append(st * dcol + _dot_tn(kh[:, c1], v_hg[:, c]))
        o_ref[r, :] = o_ref[r, :] + jnp.concatenate(o_parts, axis=1)
        for hd in range(HEADS):
            sret_ref[s, hd] = new_ret[hd]
            shg_ref[s, hd] = new_hg[hd]
        return carry

    lax.fori_loop(0, nsub, seq_body, 0, unroll=2)

    @pl.when(j == pl.num_programs(1) - 1)
    def _():
        for hd in range(HEADS):
            c0 = _cols(0, hd)
            ob_ref[:, c0] = _head_out(o_ref[:, c0], rnw_ref[...],
                                      proj_ref[:, _cols(3, hd)]).astype(BF16)
            c1 = _cols(1, hd)
            ob_ref[:, c1] = _head_out(o_ref[:, c1], hnw_ref[...],
                                      proj_ref[:, _cols(7, hd)]).astype(BF16)
        x1_ref[...] = x_ref[...] + _dot(ob_ref[...], wout_ref[...])


def _mix_sample(x, nw, win, cosf, sinf, rnw, hnw, lbl, wout, sret, shg):
    nb, seg, D = x.shape
    rows = SAMPLE_ROWS
    nsub = SAMPLE_SUB_SEQS
    nj = rows // seg // nsub
    x2 = x.reshape(nb * seg, D)
    state = jax.ShapeDtypeStruct((nb, HEADS, DH, DH), F32)
    state_spec = pl.BlockSpec((nsub, HEADS, DH, DH), lambda i, j: (i * nj + j, 0, 0, 0))
    x1, sret_new, shg_new = pl.pallas_call(
        functools.partial(_mix_sample_kernel, seg),
        grid=(nb * seg // rows, nj),
        in_specs=[
            pl.BlockSpec((rows, D), lambda i, j: (i, 0)),
            _const_spec((1, D)),
            _const_spec((D, IN_COLS)),
            _const_spec((rows, DH)),
            _const_spec((rows, DH)),
            _const_spec((1, DH)),
            _const_spec((1, DH)),
            _const_spec(lbl.shape),
            _const_spec((D, D)),
            state_spec,
            state_spec,
        ],
        out_specs=[pl.BlockSpec((rows, D), lambda i, j: (i, 0)), state_spec, state_spec],
        out_shape=[jax.ShapeDtypeStruct((nb * seg, D), F32), state, state],
        scratch_shapes=[
            pltpu.VMEM((rows, IN_COLS), F32),
            pltpu.VMEM((rows, D), F32),
            pltpu.VMEM((rows, D), F32),
            pltpu.VMEM((rows, D), F32),
            pltpu.VMEM((rows, GROUP_W), F32),
            pltpu.VMEM((rows, D), BF16),
        ],
        compiler_params=pltpu.CompilerParams(
            dimension_semantics=("arbitrary", "arbitrary"), vmem_limit_bytes=VMEM_LIMIT),
        name="mix_sample",
    )(x2, nw, win, cosf, sinf, rnw, hnw, lbl, wout, sret, shg)
    return x1, sret_new, shg_new


def _ffn_kernel(x_ref, nw_ref, wup_ref, wdn_ref, fw_ref, y_ref):
    x = x_ref[...]
    h = _rms(x, nw_ref[...]).astype(BF16)
    acc = x
    step = D_MODEL
    for g in range(D_FF // step):
        u = jnp.maximum(_dot(h, wup_ref[:, g * step:(g + 1) * step]), 0.0)
        acc = acc + _dot((u * u).astype(BF16), wdn_ref[g * step:(g + 1) * step, :])
    y_ref[...] = _rms(acc, fw_ref[...])


def _ffn(x, nw, wup, wdn, fw):
    n, D = x.shape
    rows = min(FFN_ROWS, n)
    return pl.pallas_call(
        _ffn_kernel,
        grid=(n // rows,),
        in_specs=[
            pl.BlockSpec((rows, D), lambda i: (i, 0)),
            _const_spec((1, D)),
            _const_spec((D, D_FF)),
            _const_spec((D_FF, D)),
            _const_spec((1, D)),
        ],
        out_specs=pl.BlockSpec((rows, D), lambda i: (i, 0)),
        out_shape=jax.ShapeDtypeStruct((n, D), F32),
        compiler_params=pltpu.CompilerParams(
            dimension_semantics=("arbitrary",), vmem_limit_bytes=VMEM_LIMIT),
        name="ffn",
    )(x, nw, wup, wdn, fw)


def _rope_tables(pos):
    half = DH // 2
    inv_freq = ROPE_BASE ** (-jnp.arange(half, dtype=F32) / half)
    ang = pos.astype(F32)[:, None] * inv_freq[None, :]
    cos, sin = jnp.cos(ang), jnp.sin(ang)
    return jnp.concatenate([cos, cos], axis=-1), jnp.concatenate([-sin, sin], axis=-1)


def kernel(x_prompt, x_sample, state_ret, state_hgrn, norm_mix_w, w_in, ret_norm_w, hgrn_norm_w,
           lb_logits, w_out, norm_ffn_w, w_up, w_down, final_norm_w):
    B, T, D = x_prompt.shape
    nb, seg, _ = x_sample.shape
    assert w_in.shape[0] == 1, "one layer"

    nw = norm_mix_w[0].reshape(1, D)
    win = w_in[0].astype(BF16)
    wout = w_out[0].astype(BF16)
    rnw = ret_norm_w[0].reshape(1, DH)
    hnw = hgrn_norm_w[0].reshape(1, DH)
    lbl = lb_logits.astype(F32)
    fnw = norm_ffn_w[0].reshape(1, D)
    wup = w_up[0].astype(BF16)
    wdn = w_down[0].astype(BF16)
    fw = final_norm_w.reshape(1, D)

    cos_p, sin_p = _rope_tables(jnp.arange(T, dtype=jnp.int32))
    cos_s, sin_s = _rope_tables(PAST_LEN + jnp.arange(seg, dtype=jnp.int32))
    cos_s = jnp.tile(cos_s, (SAMPLE_ROWS // seg, 1))
    sin_s = jnp.tile(sin_s, (SAMPLE_ROWS // seg, 1))

    xp1, sret_p, shg_p = _mix_prompt(x_prompt, nw, win, cos_p, sin_p, rnw, hnw, lbl, wout)
    xs1, sret_s, shg_s = _mix_sample(x_sample, nw, win, cos_s, sin_s, rnw, hnw, lbl, wout,
                                     state_ret[0], state_hgrn[0])

    y_p = _ffn(xp1.reshape(B * T, D), fnw, wup, wdn, fw).reshape(B, T, D)
    y_s = _ffn(xs1, fnw, wup, wdn, fw).reshape(nb, seg, D)
    return (y_p, y_s, sret_p[None], shg_p[None], sret_s[None], shg_s[None])
```

```python
import functools
import math

import jax
import jax.numpy as jnp
from jax import lax
from jax.experimental import pallas as pl
from jax.experimental.pallas import tpu as pltpu

D_MODEL = 1024
HEADS = 4
DH = 128
GROUP_W = HEADS * DH
IN_COLS = 8 * GROUP_W
D_FF = 4 * D_MODEL
ROPE_BASE = 10000.0
NORM_EPS = 1e-6
QK_SCALE = DH ** -0.5
PAST_LEN = 16384

PROMPT_CHUNK = 256
PROMPT_STEP_CHUNKS = 2
SAMPLE_ROWS = 256
SAMPLE_SUB_SEQS = 8
FFN_ROWS = 512
PROJ_PIECE = 512
VMEM_LIMIT = 56 * 1024 * 1024

F32 = jnp.float32
BF16 = jnp.bfloat16


def _dot(a, b):
    return jnp.dot(a, b, preferred_element_type=F32)


def _dot_nt(a, b):
    return lax.dot_general(a, b, (((1,), (1,)), ((), ())), preferred_element_type=F32)


def _dot_tn(a, b):
    return lax.dot_general(a, b, (((0,), (0,)), ((), ())), preferred_element_type=F32)


def _rms(x, w):
    ms = jnp.mean(x * x, axis=-1, keepdims=True)
    return x * lax.rsqrt(ms + NORM_EPS) * w


def _sigmoid(x):
    return 1.0 / (1.0 + jnp.exp(-x))


def _silu(x):
    return x * _sigmoid(x)


def _lower_bound(lbl):
    mx = jnp.max(lbl, axis=0, keepdims=True)
    e = jnp.exp(lbl - mx)
    return e[0:1, :] / jnp.sum(e, axis=0, keepdims=True)


def _rotary(x, cosf, sinf):
    return x * cosf + pltpu.roll(x, DH // 2, 1) * sinf


def _split_halves(x, m):
    blk = 2 * m
    nb = x.shape[0] // blk
    lo = [x[b * blk:b * blk + m] for b in range(nb)]
    up = [x[b * blk + m:(b + 1) * blk] for b in range(nb)]
    cat = lambda ps: ps[0] if len(ps) == 1 else jnp.concatenate(ps, axis=0)
    return cat(lo), cat(up)


def _merge_halves(lo, up, m):
    nb = lo.shape[0] // m
    pieces = []
    for b in range(nb):
        pieces += [lo[b * m:(b + 1) * m], up[b * m:(b + 1) * m]]
    return jnp.concatenate(pieces, axis=0)


def _block_row_bcast(x, m, row):
    nb = x.shape[0] // m
    pieces = [jnp.broadcast_to(x[b * m + row:b * m + row + 1], (m, x.shape[1])) for b in range(nb)]
    return pieces[0] if nb == 1 else jnp.concatenate(pieces, axis=0)


def _hgrn_level_masks(rows, seg):
    masks = []
    for j in range(int(math.log2(seg))):
        if (1 << j) < 8:
            r = lax.broadcasted_iota(jnp.int32, (rows, rows), 0)
            s = lax.broadcasted_iota(jnp.int32, (rows, rows), 1)
            masks.append(jnp.logical_and(((r ^ s) >> j) == 1, r > s))
        else:
            r = lax.broadcasted_iota(jnp.int32, (rows // 2, rows), 0)
            s = lax.broadcasted_iota(jnp.int32, (rows // 2, rows), 1)
            masks.append((s >> j) == 2 * (r >> j))
    return masks


def _hgrn_intra(q, k, f, v, seg, t, masks, mid_hook=None):
    rows = q.shape[0]
    c = jnp.log2(f)
    scores = None
    for j in range(int(math.log2(seg))):
        m = 1 << j
        if m < 8:
            c3 = c.reshape(rows // 8, 8, DH)
            upper = (t & m) != 0
            if m == 1:
                w = jnp.where(upper, f, 1.0)
                tot = pltpu.roll(c3, 1, 1).reshape(rows, DH)
            else:
                if m == 2:
                    low4 = ((t & 7) < 4).reshape(rows // 8, 8, DH)
                    tot3 = jnp.where(low4, jnp.broadcast_to(c3[:, 1:2, :], c3.shape),
                                     jnp.broadcast_to(c3[:, 5:6, :], c3.shape))
                else:
                    tot3 = jnp.broadcast_to(c3[:, 3:4, :], c3.shape)
                tot = tot3.reshape(rows, DH)
                w = jnp.exp2(jnp.where(upper, c, tot - c))
            part = _dot_nt((q * w).astype(BF16), (k * w).astype(BF16))
            scores = jnp.where(masks[j], part, 0.0 if scores is None else scores)
            c = jnp.where(upper, c + tot, c)
        else:
            c_lo, c_up = _split_halves(c, m)
            tot = _block_row_bcast(c_lo, m, m - 1)
            q_up = _split_halves(q, m)[1]
            k_lo, k_up = _split_halves(k, m)
            qt = (q_up * jnp.exp2(c_up)).astype(BF16)
            kt = _merge_halves(k_lo * jnp.exp2(tot - c_lo), k_up, m).astype(BF16)
            s_lo, s_up = _split_halves(scores, m)
            s_up = jnp.where(masks[j], _dot_nt(qt, kt), s_up)
            scores = _merge_halves(s_lo, s_up, m)
            c = _merge_halves(c_lo, c_up + tot, m)
    if mid_hook is not None:
        mid_hook()
    o = _dot(scores.astype(BF16), v.astype(BF16))
    o = o + jnp.sum(q * k, axis=-1, keepdims=True) * v
    return o, c


def _ret_mask(rows, seg, log_gamma):
    r = lax.broadcasted_iota(jnp.int32, (rows, rows), 0)
    s = lax.broadcasted_iota(jnp.int32, (rows, rows), 1)
    valid = jnp.logical_and(((r ^ s) >> int(math.log2(seg))) == 0, r >= s)
    d = jnp.where(valid, r - s, 0).astype(F32)
    return jnp.where(valid, jnp.exp(d * log_gamma) * QK_SCALE, 0.0)


def _log_gamma(hd):
    return math.log(1.0 - 2.0 ** (-5.0 - hd))


def _head_out(o, w, gate):
    return o * lax.rsqrt(jnp.mean(o * o, axis=-1, keepdims=True) + NORM_EPS) * w * _silu(gate)


def _project(x_ref, nw_ref, win_ref, proj_ref):
    h = _rms(x_ref[...], nw_ref[...]).astype(BF16)
    for g in range(8):
        cols = slice(g * GROUP_W, (g + 1) * GROUP_W)
        proj_ref[:, cols] = _dot(h, win_ref[:, cols])


def _cols(group, hd):
    return slice(group * GROUP_W + hd * DH, group * GROUP_W + (hd + 1) * DH)


def _mix_prompt_kernel(nc, *refs):
    CH = PROMPT_STEP_CHUNKS
    L = PROMPT_CHUNK
    x0_ref = refs[0]
    xn_refs = refs[1:1 + CH]
    (xc_ref, nw_ref, win_ref, cos_ref, sin_ref, rnw_ref, hnw_ref, lbl_ref, wout_ref,
     x1_ref, sret_ref, shg_ref,
     proj_ref, o_ref, srt_ref, sht_ref, dm_ref, qd_ref, kd_ref) = refs[1 + CH:]
    g = pl.program_id(0)
    ci0 = (g * CH) % nc
    t = lax.broadcasted_iota(jnp.int32, (L, DH), 0)

    @pl.when(g == 0)
    def _():
        tf = t.astype(F32)
        for hd in range(HEADS):
            lg = _log_gamma(hd)
            dm_ref[hd] = _ret_mask(L, L, lg)
            qd_ref[hd] = jnp.exp((tf + 1.0) * lg)
            kd_ref[hd] = jnp.exp((L - 1.0 - tf) * lg) * QK_SCALE
        _project(x0_ref, nw_ref, win_ref, proj_ref.at[0])

    @pl.when(ci0 == 0)
    def _():
        srt_ref[...] = jnp.zeros_like(srt_ref)
        sht_ref[...] = jnp.zeros_like(sht_ref)

    lb = _lower_bound(lbl_ref[...])
    masks = _hgrn_level_masks(L, L)

    def chunk(pr, pw, xn_ref, rows):
        h_next = _rms(xn_ref[...], nw_ref[...]).astype(BF16)

        def project_piece(pi):
            cols = slice(pi * PROJ_PIECE, (pi + 1) * PROJ_PIECE)
            pw[:, cols] = _dot(h_next, win_ref[:, cols])

        cosf = cos_ref[rows, :]
        sinf = sin_ref[rows, :]

        for hd in range(HEADS):
            q = _rotary(pr[:, _cols(0, hd)], cosf, sinf)
            k = _rotary(pr[:, _cols(1, hd)], cosf, sinf)
            v = pr[:, _cols(2, hd)].astype(BF16)
            scores = _dot_nt(q.astype(BF16), k.astype(BF16)) * dm_ref[hd]
            project_piece(hd)
            o = _dot(scores.astype(BF16), v)
            st = srt_ref[hd]
            o = o + _dot_nt((q * qd_ref[hd]).astype(BF16), st.astype(BF16))
            kh = (k * kd_ref[hd]).astype(BF16)
            srt_ref[hd] = st * math.exp(L * _log_gamma(hd)) + _dot_tn(v, kh)
            o_ref[:, _cols(0, hd)] = _head_out(o, rnw_ref[...], pr[:, _cols(3, hd)]).astype(BF16)

        for hd in range(HEADS):
            q = _silu(pr[:, _cols(4, hd)]) * QK_SCALE
            lbh = lb[:, hd * DH:(hd + 1) * DH]
            f = lbh + (1.0 - lbh) * _sigmoid(pr[:, _cols(5, hd)])
            k = 1.0 - f
            v = pr[:, _cols(6, hd)]
            o, b = _hgrn_intra(q, k, f, v, L, t, masks,
                               functools.partial(project_piece, HEADS + hd))
            blast = b[L - 1:L, :]
            st = sht_ref[hd]
            o = o + _dot_nt((q * jnp.exp2(b)).astype(BF16), st.astype(BF16))
            kh = (k * jnp.exp2(blast - b)).astype(BF16)
            sht_ref[hd] = st * jnp.exp2(blast) + _dot_tn(v.astype(BF16), kh)
            o_ref[:, _cols(1, hd)] = _head_out(o, hnw_ref[...], pr[:, _cols(7, hd)]).astype(BF16)

        x1_ref[rows, :] = xc_ref[rows, :] + _dot(o_ref[...], wout_ref[...])

    for sub in range(CH):
        chunk(proj_ref.at[sub % 2], proj_ref.at[(sub + 1) % 2], xn_refs[sub],
              slice(sub * L, (sub + 1) * L))

    @pl.when(ci0 == nc - CH)
    def _():
        for hd in range(HEADS):
            sret_ref[hd] = srt_ref[hd].T
            shg_ref[hd] = sht_ref[hd].T


def _const_spec(shape):
    return pl.BlockSpec(shape, lambda *_: (0,) * len(shape), pipeline_mode=pl.Buffered(1))


def _mix_prompt(x, nw, win, cosf, sinf, rnw, hnw, lbl, wout):
    B, T, D = x.shape
    L = PROMPT_CHUNK
    CH = PROMPT_STEP_CHUNKS
    nc = T // L
    n = B * nc
    spb = nc // CH
    assert nc % CH == 0 and CH % 2 == 0

    def next_chunk_spec(i):
        idx = lambda g: jnp.minimum(CH * g + i + 1, n - 1)
        return pl.BlockSpec((None, L, D), lambda g: (idx(g) // nc, idx(g) % nc, 0))

    state = jax.ShapeDtypeStruct((B, HEADS, DH, DH), F32)
    state_spec = pl.BlockSpec((None, HEADS, DH, DH), lambda g: (g // spb, 0, 0, 0))
    rows_spec = pl.BlockSpec((None, CH * L, D), lambda g: (g // spb, g % spb, 0))
    table_spec = pl.BlockSpec((CH * L, DH), lambda g: (g % spb, 0))
    return pl.pallas_call(
        functools.partial(_mix_prompt_kernel, nc),
        grid=(n // CH,),
        in_specs=[
            pl.BlockSpec((None, L, D), lambda g: (0, 0, 0), pipeline_mode=pl.Buffered(1)),
            *[next_chunk_spec(i) for i in range(CH)],
            rows_spec,
            _const_spec((1, D)),
            _const_spec((D, IN_COLS)),
            table_spec,
            table_spec,
            _const_spec((1, DH)),
            _const_spec((1, DH)),
            _const_spec(lbl.shape),
            _const_spec((D, D)),
        ],
        out_specs=[rows_spec, state_spec, state_spec],
        out_shape=[jax.ShapeDtypeStruct((B, T, D), F32), state, state],
        scratch_shapes=[
            pltpu.VMEM((2, L, IN_COLS), F32),
            pltpu.VMEM((L, D), BF16),
            pltpu.VMEM((HEADS, DH, DH), F32),
            pltpu.VMEM((HEADS, DH, DH), F32),
            pltpu.VMEM((HEADS, L, L), F32),
            pltpu.VMEM((HEADS, L, DH), F32),
            pltpu.VMEM((HEADS, L, DH), F32),
        ],
        compiler_params=pltpu.CompilerParams(
            dimension_semantics=("arbitrary",), vmem_limit_bytes=VMEM_LIMIT),
        name="mix_prompt",
    )(x, *([x] * CH), x, nw, win, cosf, sinf, rnw, hnw, lbl, wout)


def _mix_sample_kernel(seg, x_ref, nw_ref, win_ref, cos_ref, sin_ref, rnw_ref, hnw_ref, lbl_ref,
                       wout_ref, sret_in_ref, shg_in_ref,
                       x1_ref, sret_ref, shg_ref,
                       proj_ref, o_ref, qh_ref, kh_ref, dec_ref, ob_ref):
    rows = x_ref.shape[0]
    nsub = sret_in_ref.shape[0]
    j = pl.program_id(1)

    @pl.when(j == 0)
    def _():
        _project(x_ref, nw_ref, win_ref, proj_ref)
        t = lax.broadcasted_iota(jnp.int32, (rows, DH), 0)
        tl = (t & (seg - 1)).astype(F32)
        cosf = cos_ref[...]
        sinf = sin_ref[...]
        lb = _lower_bound(lbl_ref[...])
        masks = _hgrn_level_masks(rows, seg)
        for hd in range(HEADS):
            lg = _log_gamma(hd)
            q = _rotary(proj_ref[:, _cols(0, hd)], cosf, sinf)
            k = _rotary(proj_ref[:, _cols(1, hd)], cosf, sinf)
            scores = _dot_nt(q.astype(BF16), k.astype(BF16)) * _ret_mask(rows, seg, lg)
            o_ref[:, _cols(0, hd)] = _dot(scores.astype(BF16),
                                          proj_ref[:, _cols(2, hd)].astype(BF16))
            qh_ref[:, _cols(0, hd)] = q * jnp.exp((tl + 1.0) * lg)
            kh_ref[:, _cols(0, hd)] = k * (jnp.exp((seg - 1.0 - tl) * lg) * QK_SCALE)
        for hd in range(HEADS):
            q = _silu(proj_ref[:, _cols(4, hd)]) * QK_SCALE
            lbh = lb[:, hd * DH:(hd + 1) * DH]
            f = lbh + (1.0 - lbh) * _sigmoid(proj_ref[:, _cols(5, hd)])
            k = 1.0 - f
            o, b = _hgrn_intra(q, k, f, proj_ref[:, _cols(6, hd)], seg, t, masks)
            blast = _block_row_bcast(b, seg, seg - 1)
            o_ref[:, _cols(1, hd)] = o
            qh_ref[:, _cols(1, hd)] = q * jnp.exp2(b)
            kh_ref[:, _cols(1, hd)] = k * jnp.exp2(blast - b)
            dec_ref[:, hd * DH:(hd + 1) * DH] = jnp.exp2(blast)

    def seq_body(s, carry):
        r = pl.ds(pl.multiple_of((j * nsub + s) * seg, seg), seg)
        qh = qh_ref[r, :].astype(BF16)
        kh = kh_ref[r, :].astype(BF16)
        v_ret = proj_ref[r, 2 * GROUP_W:3 * GROUP_W].astype(BF16)
        v_hg = proj_ref[r, 6 * GROUP_W:7 * GROUP_W].astype(BF16)
        dec = dec_ref[r, :][0:1, :]
        o_parts, new_ret, new_hg = [], [], []
        for hd in range(HEADS):
            st = sret_in_ref[s, hd]
            c = slice(hd * DH, (hd + 1) * DH)
            o_parts.append(_dot(qh[:, c], st.astype(BF16)))
            new_ret.append(st * math.exp(seg * _log_gamma(hd)) + _dot_tn(kh[:, c], v_ret[:, c]))
        for hd in range(HEADS):
            st = shg_in_ref[s, hd]
            c = slice(hd * DH, (hd + 1) * DH)
            c1 = slice(GROUP_W + hd * DH, GROUP_W + (hd + 1) * DH)
            o_parts.append(_dot(qh[:, c1], st.astype(BF16)))
            dcol = jnp.broadcast_to(dec[:, c], (DH, DH)).T
            new_hg.append(st * dcol + _dot_tn(kh[:, c1], v_hg[:, c]))
        o_ref[r, :] = o_ref[r, :] + jnp.concatenate(o_parts, axis=1)
        for hd in range(HEADS):
            sret_ref[s, hd] = new_ret[hd]
            shg_ref[s, hd] = new_hg[hd]
        return carry

    lax.fori_loop(0, nsub, seq_body, 0, unroll=2)

    @pl.when(j == pl.num_programs(1) - 1)
    def _():
        for hd in range(HEADS):
            c0 = _cols(0, hd)
            ob_ref[:, c0] = _head_out(o_ref[:, c0], rnw_ref[...],
                                      proj_ref[:, _cols(3, hd)]).astype(BF16)
            c1 = _cols(1, hd)
            ob_ref[:, c1] = _head_out(o_ref[:, c1], hnw_ref[...],
                                      proj_ref[:, _cols(7, hd)]).astype(BF16)
        x1_ref[...] = x_ref[...] + _dot(ob_ref[...], wout_ref[...])


def _mix_sample(x, nw, win, cosf, sinf, rnw, hnw, lbl, wout, sret, shg):
    nb, seg, D = x.shape
    rows = SAMPLE_ROWS
    nsub = SAMPLE_SUB_SEQS
    nj = rows // seg // nsub
    x2 = x.reshape(nb * seg, D)
    state = jax.ShapeDtypeStruct((nb, HEADS, DH, DH), F32)
    state_spec = pl.BlockSpec((nsub, HEADS, DH, DH), lambda i, j: (i * nj + j, 0, 0, 0))
    x1, sret_new, shg_new = pl.pallas_call(
        functools.partial(_mix_sample_kernel, seg),
        grid=(nb * seg // rows, nj),
        in_specs=[
            pl.BlockSpec((rows, D), lambda i, j: (i, 0)),
            _const_spec((1, D)),
            _const_spec((D, IN_COLS)),
            _const_spec((rows, DH)),
            _const_spec((rows, DH)),
            _const_spec((1, DH)),
            _const_spec((1, DH)),
            _const_spec(lbl.shape),
            _const_spec((D, D)),
            state_spec,
            state_spec,
        ],
        out_specs=[pl.BlockSpec((rows, D), lambda i, j: (i, 0)), state_spec, state_spec],
        out_shape=[jax.ShapeDtypeStruct((nb * seg, D), F32), state, state],
        scratch_shapes=[
            pltpu.VMEM((rows, IN_COLS), F32),
            pltpu.VMEM((rows, D), F32),
            pltpu.VMEM((rows, D), F32),
            pltpu.VMEM((rows, D), F32),
            pltpu.VMEM((rows, GROUP_W), F32),
            pltpu.VMEM((rows, D), BF16),
        ],
        compiler_params=pltpu.CompilerParams(
            dimension_semantics=("arbitrary", "arbitrary"), vmem_limit_bytes=VMEM_LIMIT),
        name="mix_sample",
    )(x2, nw, win, cosf, sinf, rnw, hnw, lbl, wout, sret, shg)
    return x1, sret_new, shg_new


def _ffn_kernel(x_ref, nw_ref, wup_ref, wdn_ref, fw_ref, y_ref):
    x = x_ref[...]
    h = _rms(x, nw_ref[...]).astype(BF16)
    acc = x
    step = D_MODEL
    for g in range(D_FF // step):
        u = jnp.maximum(_dot(h, wup_ref[:, g * step:(g + 1) * step]), 0.0)
        acc = acc + _dot((u * u).astype(BF16), wdn_ref[g * step:(g + 1) * step, :])
    y_ref[...] = _rms(acc, fw_ref[...])


def _ffn(x, nw, wup, wdn, fw):
    n, D = x.shape
    rows = min(FFN_ROWS, n)
    return pl.pallas_call(
        _ffn_kernel,
        grid=(n // rows,),
        in_specs=[
            pl.BlockSpec((rows, D), lambda i: (i, 0)),
            _const_spec((1, D)),
            _const_spec((D, D_FF)),
            _const_spec((D_FF, D)),
            _const_spec((1, D)),
        ],
        out_specs=pl.BlockSpec((rows, D), lambda i: (i, 0)),
        out_shape=jax.ShapeDtypeStruct((n, D), F32),
        compiler_params=pltpu.CompilerParams(
            dimension_semantics=("arbitrary",), vmem_limit_bytes=VMEM_LIMIT),
        name="ffn",
    )(x, nw, wup, wdn, fw)


def _rope_tables(pos):
    half = DH // 2
    inv_freq = ROPE_BASE ** (-jnp.arange(half, dtype=F32) / half)
    ang = pos.astype(F32)[:, None] * inv_freq[None, :]
    cos, sin = jnp.cos(ang), jnp.sin(ang)
    return jnp.concatenate([cos, cos], axis=-1), jnp.concatenate([-sin, sin], axis=-1)


def kernel(x_prompt, x_sample, state_ret, state_hgrn, norm_mix_w, w_in, ret_norm_w, hgrn_norm_w,
           lb_logits, w_out, norm_ffn_w, w_up, w_down, final_norm_w):
    B, T, D = x_prompt.shape
    nb, seg, _ = x_sample.shape
    assert w_in.shape[0] == 1, "one layer"

    nw = norm_mix_w[0].reshape(1, D)
    win = w_in[0].astype(BF16)
    wout = w_out[0].astype(BF16)
    rnw = ret_norm_w[0].reshape(1, DH)
    hnw = hgrn_norm_w[0].reshape(1, DH)
    lbl = lb_logits.astype(F32)
    fnw = norm_ffn_w[0].reshape(1, D)
    wup = w_up[0].astype(BF16)
    wdn = w_down[0].astype(BF16)
    fw = final_norm_w.reshape(1, D)

    cos_p, sin_p = _rope_tables(jnp.arange(T, dtype=jnp.int32))
    cos_s, sin_s = _rope_tables(PAST_LEN + jnp.arange(seg, dtype=jnp.int32))
    cos_s = jnp.tile(cos_s, (SAMPLE_ROWS // seg, 1))
    sin_s = jnp.tile(sin_s, (SAMPLE_ROWS // seg, 1))

    xp1, sret_p, shg_p = _mix_prompt(x_prompt, nw, win, cos_p, sin_p, rnw, hnw, lbl, wout)
    xs1, sret_s, shg_s = _mix_sample(x_sample, nw, win, cos_s, sin_s, rnw, hnw, lbl, wout,
                                     state_ret[0], state_hgrn[0])

    y_p = _ffn(xp1.reshape(B * T, D), fnw, wup, wdn, fw).reshape(B, T, D)
    y_s = _ffn(xs1, fnw, wup, wdn, fw).reshape(nb, seg, D)
    return (y_p, y_s, sret_p[None], shg_p[None], sret_s[None], shg_s[None])
```

```python
import functools
import math

import jax
import jax.numpy as jnp
from jax import lax
from jax.experimental import pallas as pl
from jax.experimental.pallas import tpu as pltpu

D_MODEL = 1024
HEADS = 4
DH = 128
GROUP_W = HEADS * DH
IN_COLS = 8 * GROUP_W
D_FF = 4 * D_MODEL
ROPE_BASE = 10000.0
NORM_EPS = 1e-6
QK_SCALE = DH ** -0.5
PAST_LEN = 16384

PROMPT_CHUNK = 256
PROMPT_STEP_CHUNKS = 4
SAMPLE_ROWS = 256
SAMPLE_SUB_SEQS = 8
FFN_ROWS = 512
PROJ_PIECE = 512
VMEM_LIMIT = 56 * 1024 * 1024

F32 = jnp.float32
BF16 = jnp.bfloat16


def _dot(a, b):
    return jnp.dot(a, b, preferred_element_type=F32)


def _dot_nt(a, b):
    return lax.dot_general(a, b, (((1,), (1,)), ((), ())), preferred_element_type=F32)


def _dot_tn(a, b):
    return lax.dot_general(a, b, (((0,), (0,)), ((), ())), preferred_element_type=F32)


def _rms(x, w):
    ms = jnp.mean(x * x, axis=-1, keepdims=True)
    return x * lax.rsqrt(ms + NORM_EPS) * w


def _sigmoid(x):
    return 1.0 / (1.0 + jnp.exp(-x))


def _silu(x):
    return x * _sigmoid(x)


def _lower_bound(lbl):
    mx = jnp.max(lbl, axis=0, keepdims=True)
    e = jnp.exp(lbl - mx)
    return e[0:1, :] / jnp.sum(e, axis=0, keepdims=True)


def _rotary(x, cosf, sinf):
    return x * cosf + pltpu.roll(x, DH // 2, 1) * sinf


def _split_halves(x, m):
    blk = 2 * m
    nb = x.shape[0] // blk
    lo = [x[b * blk:b * blk + m] for b in range(nb)]
    up = [x[b * blk + m:(b + 1) * blk] for b in range(nb)]
    cat = lambda ps: ps[0] if len(ps) == 1 else jnp.concatenate(ps, axis=0)
    return cat(lo), cat(up)


def _merge_halves(lo, up, m):
    nb = lo.shape[0] // m
    pieces = []
    for b in range(nb):
        pieces += [lo[b * m:(b + 1) * m], up[b * m:(b + 1) * m]]
    return jnp.concatenate(pieces, axis=0)


def _block_row_bcast(x, m, row):
    nb = x.shape[0] // m
    pieces = [jnp.broadcast_to(x[b * m + row:b * m + row + 1], (m, x.shape[1])) for b in range(nb)]
    return pieces[0] if nb == 1 else jnp.concatenate(pieces, axis=0)


def _hgrn_level_masks(rows, seg):
    masks = []
    for j in range(int(math.log2(seg))):
        if (1 << j) < 8:
            r = lax.broadcasted_iota(jnp.int32, (rows, rows), 0)
            s = lax.broadcasted_iota(jnp.int32, (rows, rows), 1)
            masks.append(jnp.logical_and(((r ^ s) >> j) == 1, r > s))
        else:
            r = lax.broadcasted_iota(jnp.int32, (rows // 2, rows), 0)
            s = lax.broadcasted_iota(jnp.int32, (rows // 2, rows), 1)
            masks.append((s >> j) == 2 * (r >> j))
    return masks


def _hgrn_intra(q, k, f, v, seg, t, masks, mid_hook=None):
    rows = q.shape[0]
    c = jnp.log2(f)
    scores = None
    for j in range(int(math.log2(seg))):
        m = 1 << j
        if m < 8:
            c3 = c.reshape(rows // 8, 8, DH)
            upper = (t & m) != 0
            if m == 1:
                w = jnp.where(upper, f, 1.0)
                tot = pltpu.roll(c3, 1, 1).reshape(rows, DH)
            else:
                if m == 2:
                    low4 = ((t & 7) < 4).reshape(rows // 8, 8, DH)
                    tot3 = jnp.where(low4, jnp.broadcast_to(c3[:, 1:2, :], c3.shape),
                                     jnp.broadcast_to(c3[:, 5:6, :], c3.shape))
                else:
                    tot3 = jnp.broadcast_to(c3[:, 3:4, :], c3.shape)
                tot = tot3.reshape(rows, DH)
                w = jnp.exp2(jnp.where(upper, c, tot - c))
            part = _dot_nt((q * w).astype(BF16), (k * w).astype(BF16))
            scores = jnp.where(masks[j], part, 0.0 if scores is None else scores)
            c = jnp.where(upper, c + tot, c)
        else:
            c_lo, c_up = _split_halves(c, m)
            tot = _block_row_bcast(c_lo, m, m - 1)
            q_up = _split_halves(q, m)[1]
            k_lo, k_up = _split_halves(k, m)
            qt = (q_up * jnp.exp2(c_up)).astype(BF16)
            kt = _merge_halves(k_lo * jnp.exp2(tot - c_lo), k_up, m).astype(BF16)
            s_lo, s_up = _split_halves(scores, m)
            s_up = jnp.where(masks[j], _dot_nt(qt, kt), s_up)
            scores = _merge_halves(s_lo, s_up, m)
            c = _merge_halves(c_lo, c_up + tot, m)
    if mid_hook is not None:
        mid_hook()
    o = _dot(scores.astype(BF16), v.astype(BF16))
    o = o + jnp.sum(q * k, axis=-1, keepdims=True) * v
    return o, c


def _ret_mask(rows, seg, log_gamma):
    r = lax.broadcasted_iota(jnp.int32, (rows, rows), 0)
    s = lax.broadcasted_iota(jnp.int32, (rows, rows), 1)
    valid = jnp.logical_and(((r ^ s) >> int(math.log2(seg))) == 0, r >= s)
    d = jnp.where(valid, r - s, 0).astype(F32)
    return jnp.where(valid, jnp.exp(d * log_gamma) * QK_SCALE, 0.0)


def _log_gamma(hd):
    return math.log(1.0 - 2.0 ** (-5.0 - hd))


def _head_out(o, w, gate):
    return o * lax.rsqrt(jnp.mean(o * o, axis=-1, keepdims=True) + NORM_EPS) * w * _silu(gate)


def _project(x_ref, nw_ref, win_ref, proj_ref):
    h = _rms(x_ref[...], nw_ref[...]).astype(BF16)
    for g in range(8):
        cols = slice(g * GROUP_W, (g + 1) * GROUP_W)
        proj_ref[:, cols] = _dot(h, win_ref[:, cols])


def _cols(group, hd):
    return slice(group * GROUP_W + hd * DH, group * GROUP_W + (hd + 1) * DH)


def _mix_prompt_kernel(nc, *refs):
    CH = PROMPT_STEP_CHUNKS
    L = PROMPT_CHUNK
    x0_ref = refs[0]
    xn_refs = refs[1:1 + CH]
    (nw_ref, win_ref, cos_ref, sin_ref, rnw_ref, hnw_ref, lbl_ref, wout_ref,
     x1_ref, sret_ref, shg_ref,
     proj_ref, xk_ref, o_ref, srt_ref, sht_ref, dm_ref, qd_ref, kd_ref) = refs[1 + CH:]
    g = pl.program_id(0)
    ci0 = (g * CH) % nc
    t = lax.broadcasted_iota(jnp.int32, (L, DH), 0)

    @pl.when(g == 0)
    def _():
        tf = t.astype(F32)
        for hd in range(HEADS):
            lg = _log_gamma(hd)
            dm_ref[hd] = _ret_mask(L, L, lg)
            qd_ref[hd] = jnp.exp((tf + 1.0) * lg)
            kd_ref[hd] = jnp.exp((L - 1.0 - tf) * lg) * QK_SCALE
        _project(x0_ref, nw_ref, win_ref, proj_ref.at[0])
        xk_ref[0] = x0_ref[...]

    @pl.when(ci0 == 0)
    def _():
        srt_ref[...] = jnp.zeros_like(srt_ref)
        sht_ref[...] = jnp.zeros_like(sht_ref)

    lb = _lower_bound(lbl_ref[...])
    masks = _hgrn_level_masks(L, L)

    def chunk(pr, pw, xk_cur, xk_next, xn_ref, rows):
        x_next = xn_ref[...]
        xk_next[...] = x_next
        h_next = _rms(x_next, nw_ref[...]).astype(BF16)

        def project_piece(pi):
            cols = slice(pi * PROJ_PIECE, (pi + 1) * PROJ_PIECE)
            pw[:, cols] = _dot(h_next, win_ref[:, cols])

        cosf = cos_ref[rows, :]
        sinf = sin_ref[rows, :]

        for hd in range(HEADS):
            q = _rotary(pr[:, _cols(0, hd)], cosf, sinf)
            k = _rotary(pr[:, _cols(1, hd)], cosf, sinf)
            v = pr[:, _cols(2, hd)].astype(BF16)
            scores = _dot_nt(q.astype(BF16), k.astype(BF16)) * dm_ref[hd]
            project_piece(hd)
            o = _dot(scores.astype(BF16), v)
            st = srt_ref[hd]
            o = o + _dot_nt((q * qd_ref[hd]).astype(BF16), st.astype(BF16))
            kh = (k * kd_ref[hd]).astype(BF16)
            srt_ref[hd] = st * math.exp(L * _log_gamma(hd)) + _dot_tn(v, kh)
            o_ref[:, _cols(0, hd)] = _head_out(o, rnw_ref[...], pr[:, _cols(3, hd)]).astype(BF16)

        for hd in range(HEADS):
            q = _silu(pr[:, _cols(4, hd)]) * QK_SCALE
            lbh = lb[:, hd * DH:(hd + 1) * DH]
            f = lbh + (1.0 - lbh) * _sigmoid(pr[:, _cols(5, hd)])
            k = 1.0 - f
            v = pr[:, _cols(6, hd)]
            o, b = _hgrn_intra(q, k, f, v, L, t, masks,
                               functools.partial(project_piece, HEADS + hd))
            blast = b[L - 1:L, :]
            st = sht_ref[hd]
            o = o + _dot_nt((q * jnp.exp2(b)).astype(BF16), st.astype(BF16))
            kh = (k * jnp.exp2(blast - b)).astype(BF16)
            sht_ref[hd] = st * jnp.exp2(blast) + _dot_tn(v.astype(BF16), kh)
            o_ref[:, _cols(1, hd)] = _head_out(o, hnw_ref[...], pr[:, _cols(7, hd)]).astype(BF16)

        x1_ref[rows, :] = xk_cur[...] + _dot(o_ref[...], wout_ref[...])

    for sub in range(CH):
        cur, nxt = sub % 2, (sub + 1) % 2
        chunk(proj_ref.at[cur], proj_ref.at[nxt], xk_ref.at[cur], xk_ref.at[nxt], xn_refs[sub],
              slice(sub * L, (sub + 1) * L))

    @pl.when(ci0 == nc - CH)
    def _():
        for hd in range(HEADS):
            sret_ref[hd] = srt_ref[hd].T
            shg_ref[hd] = sht_ref[hd].T


def _const_spec(shape):
    return pl.BlockSpec(shape, lambda *_: (0,) * len(shape), pipeline_mode=pl.Buffered(1))


def _mix_prompt(x, nw, win, cosf, sinf, rnw, hnw, lbl, wout):
    B, T, D = x.shape
    L = PROMPT_CHUNK
    CH = PROMPT_STEP_CHUNKS
    nc = T // L
    n = B * nc
    spb = nc // CH
    assert nc % CH == 0 and CH % 2 == 0

    def next_chunk_spec(i):
        idx = lambda g: jnp.minimum(CH * g + i + 1, n - 1)
        return pl.BlockSpec((None, L, D), lambda g: (idx(g) // nc, idx(g) % nc, 0))

    state = jax.ShapeDtypeStruct((B, HEADS, DH, DH), F32)
    state_spec = pl.BlockSpec((None, HEADS, DH, DH), lambda g: (g // spb, 0, 0, 0))
    rows_spec = pl.BlockSpec((None, CH * L, D), lambda g: (g // spb, g % spb, 0))
    table_spec = pl.BlockSpec((CH * L, DH), lambda g: (g % spb, 0))
    return pl.pallas_call(
        functools.partial(_mix_prompt_kernel, nc),
        grid=(n // CH,),
        in_specs=[
            pl.BlockSpec((None, L, D), lambda g: (0, 0, 0), pipeline_mode=pl.Buffered(1)),
            *[next_chunk_spec(i) for i in range(CH)],
            _const_spec((1, D)),
            _const_spec((D, IN_COLS)),
            table_spec,
            table_spec,
            _const_spec((1, DH)),
            _const_spec((1, DH)),
            _const_spec(lbl.shape),
            _const_spec((D, D)),
        ],
        out_specs=[rows_spec, state_spec, state_spec],
        out_shape=[jax.ShapeDtypeStruct((B, T, D), F32), state, state],
        scratch_shapes=[
            pltpu.VMEM((2, L, IN_COLS), F32),
            pltpu.VMEM((2, L, D), F32),
            pltpu.VMEM((L, D), BF16),
            pltpu.VMEM((HEADS, DH, DH), F32),
            pltpu.VMEM((HEADS, DH, DH), F32),
            pltpu.VMEM((HEADS, L, L), F32),
            pltpu.VMEM((HEADS, L, DH), F32),
            pltpu.VMEM((HEADS, L, DH), F32),
        ],
        compiler_params=pltpu.CompilerParams(
            dimension_semantics=("arbitrary",), vmem_limit_bytes=VMEM_LIMIT),
        name="mix_prompt",
    )(x, *([x] * CH), nw, win, cosf, sinf, rnw, hnw, lbl, wout)


def _mix_sample_kernel(seg, x_ref, nw_ref, win_ref, cos_ref, sin_ref, rnw_ref, hnw_ref, lbl_ref,
                       wout_ref, sret_in_ref, shg_in_ref,
                       x1_ref, sret_ref, shg_ref,
                       proj_ref, o_ref, qh_ref, kh_ref, dec_ref, ob_ref):
    rows = x_ref.shape[0]
    nsub = sret_in_ref.shape[0]
    j = pl.program_id(1)

    @pl.when(j == 0)
    def _():
        _project(x_ref, nw_ref, win_ref, proj_ref)
        t = lax.broadcasted_iota(jnp.int32, (rows, DH), 0)
        tl = (t & (seg - 1)).astype(F32)
        cosf = cos_ref[...]
        sinf = sin_ref[...]
        lb = _lower_bound(lbl_ref[...])
        masks = _hgrn_level_masks(rows, seg)
        for hd in range(HEADS):
            lg = _log_gamma(hd)
            q = _rotary(proj_ref[:, _cols(0, hd)], cosf, sinf)
            k = _rotary(proj_ref[:, _cols(1, hd)], cosf, sinf)
            scores = _dot_nt(q.astype(BF16), k.astype(BF16)) * _ret_mask(rows, seg, lg)
            o_ref[:, _cols(0, hd)] = _dot(scores.astype(BF16),
                                          proj_ref[:, _cols(2, hd)].astype(BF16))
            qh_ref[:, _cols(0, hd)] = q * jnp.exp((tl + 1.0) * lg)
            kh_ref[:, _cols(0, hd)] = k * (jnp.exp((seg - 1.0 - tl) * lg) * QK_SCALE)
        for hd in range(HEADS):
            q = _silu(proj_ref[:, _cols(4, hd)]) * QK_SCALE
            lbh = lb[:, hd * DH:(hd + 1) * DH]
            f = lbh + (1.0 - lbh) * _sigmoid(proj_ref[:, _cols(5, hd)])
            k = 1.0 - f
            o, b = _hgrn_intra(q, k, f, proj_ref[:, _cols(6, hd)], seg, t, masks)
            blast = _block_row_bcast(b, seg, seg - 1)
            o_ref[:, _cols(1, hd)] = o
            qh_ref[:, _cols(1, hd)] = q * jnp.exp2(b)
            kh_ref[:, _cols(1, hd)] = k * jnp.exp2(blast - b)
            dec_ref[:, hd * DH:(hd + 1) * DH] = jnp.exp2(blast)

    def seq_body(s, carry):
        r = pl.ds(pl.multiple_of((j * nsub + s) * seg, seg), seg)
        qh = qh_ref[r, :].astype(BF16)
        kh = kh_ref[r, :].astype(BF16)
        v_ret = proj_ref[r, 2 * GROUP_W:3 * GROUP_W].astype(BF16)
        v_hg = proj_ref[r, 6 * GROUP_W:7 * GROUP_W].astype(BF16)
        dec = dec_ref[r, :][0:1, :]
        o_parts, new_ret, new_hg = [], [], []
        for hd in range(HEADS):
            st = sret_in_ref[s, hd]
            c = slice(hd * DH, (hd + 1) * DH)
            o_parts.append(_dot(qh[:, c], st.astype(BF16)))
            new_ret.append(st * math.exp(seg * _log_gamma(hd)) + _dot_tn(kh[:, c], v_ret[:, c]))
        for hd in range(HEADS):
            st = shg_in_ref[s, hd]
            c = slice(hd * DH, (hd + 1) * DH)
            c1 = slice(GROUP_W + hd * DH, GROUP_W + (hd + 1) * DH)
            o_parts.append(_dot(qh[:, c1], st.astype(BF16)))
            dcol = jnp.broadcast_to(dec[:, c], (DH, DH)).T
            new_hg.append(st * dcol + _dot_tn(kh[:, c1], v_hg[:, c]))
        o_ref[r, :] = o_ref[r, :] + jnp.concatenate(o_parts, axis=1)
        for hd in range(HEADS):
            sret_ref[s, hd] = new_ret[hd]
            shg_ref[s, hd] = new_hg[hd]
        return carry

    lax.fori_loop(0, nsub, seq_body, 0, unroll=2)

    @pl.when(j == pl.num_programs(1) - 1)
    def _():
        for hd in range(HEADS):
            c0 = _cols(0, hd)
            ob_ref[:, c0] = _head_out(o_ref[:, c0], rnw_ref[...],
                                      proj_ref[:, _cols(3, hd)]).astype(BF16)
            c1 = _cols(1, hd)
            ob_ref[:, c1] = _head_out(o_ref[:, c1], hnw_ref[...],
                                      proj_ref[:, _cols(7, hd)]).astype(BF16)
        x1_ref[...] = x_ref[...] + _dot(ob_ref[...], wout_ref[...])


def _mix_sample(x, nw, win, cosf, sinf, rnw, hnw, lbl, wout, sret, shg):
    nb, seg, D = x.shape
    rows = SAMPLE_ROWS
    nsub = SAMPLE_SUB_SEQS
    nj = rows // seg // nsub
    x2 = x.reshape(nb * seg, D)
    state = jax.ShapeDtypeStruct((nb, HEADS, DH, DH), F32)
    state_spec = pl.BlockSpec((nsub, HEADS, DH, DH), lambda i, j: (i * nj + j, 0, 0, 0))
    x1, sret_new, shg_new = pl.pallas_call(
        functools.partial(_mix_sample_kernel, seg),
        grid=(nb * seg // rows, nj),
        in_specs=[
            pl.BlockSpec((rows, D), lambda i, j: (i, 0)),
            _const_spec((1, D)),
            _const_spec((D, IN_COLS)),
            _const_spec((rows, DH)),
            _const_spec((rows, DH)),
            _const_spec((1, DH)),
            _const_spec((1, DH)),
            _const_spec(lbl.shape),
            _const_spec((D, D)),
            state_spec,
            state_spec,
        ],
        out_specs=[pl.BlockSpec((rows, D), lambda i, j: (i, 0)), state_spec, state_spec],
        out_shape=[jax.ShapeDtypeStruct((nb * seg, D), F32), state, state],
        scratch_shapes=[
            pltpu.VMEM((rows, IN_COLS), F32),
            pltpu.VMEM((rows, D), F32),
            pltpu.VMEM((rows, D), F32),
            pltpu.VMEM((rows, D), F32),
            pltpu.VMEM((rows, GROUP_W), F32),
            pltpu.VMEM((rows, D), BF16),
        ],
        compiler_params=pltpu.CompilerParams(
            dimension_semantics=("arbitrary", "arbitrary"), vmem_limit_bytes=VMEM_LIMIT),
        name="mix_sample",
    )(x2, nw, win, cosf, sinf, rnw, hnw, lbl, wout, sret, shg)
    return x1, sret_new, shg_new


def _ffn_kernel(x_ref, nw_ref, wup_ref, wdn_ref, fw_ref, y_ref):
    x = x_ref[...]
    h = _rms(x, nw_ref[...]).astype(BF16)
    acc = x
    step = D_MODEL
    for g in range(D_FF // step):
        u = jnp.maximum(_dot(h, wup_ref[:, g * step:(g + 1) * step]), 0.0)
        acc = acc + _dot((u * u).astype(BF16), wdn_ref[g * step:(g + 1) * step, :])
    y_ref[...] = _rms(acc, fw_ref[...])


def _ffn(x, nw, wup, wdn, fw):
    n, D = x.shape
    rows = min(FFN_ROWS, n)
    return pl.pallas_call(
        _ffn_kernel,
        grid=(n // rows,),
        in_specs=[
            pl.BlockSpec((rows, D), lambda i: (i, 0)),
            _const_spec((1, D)),
            _const_spec((D, D_FF)),
            _const_spec((D_FF, D)),
            _const_spec((1, D)),
        ],
        out_specs=pl.BlockSpec((rows, D), lambda i: (i, 0)),
        out_shape=jax.ShapeDtypeStruct((n, D), F32),
        compiler_params=pltpu.CompilerParams(
            dimension_semantics=("arbitrary",), vmem_limit_bytes=VMEM_LIMIT),
        name="ffn",
    )(x, nw, wup, wdn, fw)


def _rope_tables(pos):
    half = DH // 2
    inv_freq = ROPE_BASE ** (-jnp.arange(half, dtype=F32) / half)
    ang = pos.astype(F32)[:, None] * inv_freq[None, :]
    cos, sin = jnp.cos(ang), jnp.sin(ang)
    return jnp.concatenate([cos, cos], axis=-1), jnp.concatenate([-sin, sin], axis=-1)


def kernel(x_prompt, x_sample, state_ret, state_hgrn, norm_mix_w, w_in, ret_norm_w, hgrn_norm_w,
           lb_logits, w_out, norm_ffn_w, w_up, w_down, final_norm_w):
    B, T, D = x_prompt.shape
    nb, seg, _ = x_sample.shape
    assert w_in.shape[0] == 1, "one layer"

    nw = norm_mix_w[0].reshape(1, D)
    win = w_in[0].astype(BF16)
    wout = w_out[0].astype(BF16)
    rnw = ret_norm_w[0].reshape(1, DH)
    hnw = hgrn_norm_w[0].reshape(1, DH)
    lbl = lb_logits.astype(F32)
    fnw = norm_ffn_w[0].reshape(1, D)
    wup = w_up[0].astype(BF16)
    wdn = w_down[0].astype(BF16)
    fw = final_norm_w.reshape(1, D)

    cos_p, sin_p = _rope_tables(jnp.arange(T, dtype=jnp.int32))
    cos_s, sin_s = _rope_tables(PAST_LEN + jnp.arange(seg, dtype=jnp.int32))
    cos_s = jnp.tile(cos_s, (SAMPLE_ROWS // seg, 1))
    sin_s = jnp.tile(sin_s, (SAMPLE_ROWS // seg, 1))

    xp1, sret_p, shg_p = _mix_prompt(x_prompt, nw, win, cos_p, sin_p, rnw, hnw, lbl, wout)
    xs1, sret_s, shg_s = _mix_sample(x_sample, nw, win, cos_s, sin_s, rnw, hnw, lbl, wout,
                                     state_ret[0], state_hgrn[0])

    y_p = _ffn(xp1.reshape(B * T, D), fnw, wup, wdn, fw).reshape(B, T, D)
    y_s = _ffn(xs1, fnw, wup, wdn, fw).reshape(nb, seg, D)
    return (y_p, y_s, sret_p[None], shg_p[None], sret_s[None], shg_s[None])
```

```python
import functools
import math

import jax
import jax.numpy as jnp
from jax import lax
from jax.experimental import pallas as pl
from jax.experimental.pallas import tpu as pltpu

D_MODEL = 1024
HEADS = 4
DH = 128
GROUP_W = HEADS * DH
IN_COLS = 8 * GROUP_W
D_FF = 4 * D_MODEL
ROPE_BASE = 10000.0
NORM_EPS = 1e-6
QK_SCALE = DH ** -0.5
PAST_LEN = 16384

PROMPT_CHUNK = 256
PROMPT_STEP_CHUNKS = 2
SAMPLE_ROWS = 256
SAMPLE_SUB_SEQS = 8
FFN_ROWS = 512
PROJ_PIECE = 512
VMEM_LIMIT = 56 * 1024 * 1024

F32 = jnp.float32
BF16 = jnp.bfloat16


def _dot(a, b):
    return jnp.dot(a, b, preferred_element_type=F32)


def _dot_nt(a, b):
    return lax.dot_general(a, b, (((1,), (1,)), ((), ())), preferred_element_type=F32)


def _dot_tn(a, b):
    return lax.dot_general(a, b, (((0,), (0,)), ((), ())), preferred_element_type=F32)


def _rms(x, w):
    ms = jnp.mean(x * x, axis=-1, keepdims=True)
    return x * lax.rsqrt(ms + NORM_EPS) * w


def _sigmoid(x):
    return 1.0 / (1.0 + jnp.exp(-x))


def _silu(x):
    return x * _sigmoid(x)


def _lower_bound(lbl):
    mx = jnp.max(lbl, axis=0, keepdims=True)
    e = jnp.exp(lbl - mx)
    return e[0:1, :] / jnp.sum(e, axis=0, keepdims=True)


def _rotary(x, cosf, sinf):
    return x * cosf + pltpu.roll(x, DH // 2, 1) * sinf


def _split_halves(x, m):
    blk = 2 * m
    nb = x.shape[0] // blk
    lo = [x[b * blk:b * blk + m] for b in range(nb)]
    up = [x[b * blk + m:(b + 1) * blk] for b in range(nb)]
    cat = lambda ps: ps[0] if len(ps) == 1 else jnp.concatenate(ps, axis=0)
    return cat(lo), cat(up)


def _merge_halves(lo, up, m):
    nb = lo.shape[0] // m
    pieces = []
    for b in range(nb):
        pieces += [lo[b * m:(b + 1) * m], up[b * m:(b + 1) * m]]
    return jnp.concatenate(pieces, axis=0)


def _block_row_bcast(x, m, row):
    nb = x.shape[0] // m
    pieces = [jnp.broadcast_to(x[b * m + row:b * m + row + 1], (m, x.shape[1])) for b in range(nb)]
    return pieces[0] if nb == 1 else jnp.concatenate(pieces, axis=0)


def _hgrn_level_masks(rows, seg):
    masks = []
    for j in range(int(math.log2(seg))):
        if (1 << j) < 8:
            r = lax.broadcasted_iota(jnp.int32, (rows, rows), 0)
            s = lax.broadcasted_iota(jnp.int32, (rows, rows), 1)
            masks.append(jnp.logical_and(((r ^ s) >> j) == 1, r > s))
        else:
            r = lax.broadcasted_iota(jnp.int32, (rows // 2, rows), 0)
            s = lax.broadcasted_iota(jnp.int32, (rows // 2, rows), 1)
            masks.append((s >> j) == 2 * (r >> j))
    return masks


def _hgrn_intra(q, k, f, v, seg, t, masks, mid_hook=None):
    rows = q.shape[0]
    c = jnp.log2(f)
    scores = None
    for j in range(int(math.log2(seg))):
        m = 1 << j
        if m < 8:
            c3 = c.reshape(rows // 8, 8, DH)
            upper = (t & m) != 0
            if m == 1:
                w = jnp.where(upper, f, 1.0)
                tot = pltpu.roll(c3, 1, 1).reshape(rows, DH)
            else:
                if m == 2:
                    low4 = ((t & 7) < 4).reshape(rows // 8, 8, DH)
                    tot3 = jnp.where(low4, jnp.broadcast_to(c3[:, 1:2, :], c3.shape),
                                     jnp.broadcast_to(c3[:, 5:6, :], c3.shape))
                else:
                    tot3 = jnp.broadcast_to(c3[:, 3:4, :], c3.shape)
                tot = tot3.reshape(rows, DH)
                w = jnp.exp2(jnp.where(upper, c, tot - c))
            part = _dot_nt((q * w).astype(BF16), (k * w).astype(BF16))
            scores = jnp.where(masks[j], part, 0.0 if scores is None else scores)
            c = jnp.where(upper, c + tot, c)
        else:
            c_lo, c_up = _split_halves(c, m)
            tot = _block_row_bcast(c_lo, m, m - 1)
            q_up = _split_halves(q, m)[1]
            k_lo, k_up = _split_halves(k, m)
            qt = (q_up * jnp.exp2(c_up)).astype(BF16)
            kt = _merge_halves(k_lo * jnp.exp2(tot - c_lo), k_up, m).astype(BF16)
            s_lo, s_up = _split_halves(scores, m)
            s_up = jnp.where(masks[j], _dot_nt(qt, kt), s_up)
            scores = _merge_halves(s_lo, s_up, m)
            c = _merge_halves(c_lo, c_up + tot, m)
    if mid_hook is not None:
        mid_hook()
    o = _dot(scores.astype(BF16), v.astype(BF16))
    o = o + jnp.sum(q * k, axis=-1, keepdims=True) * v
    return o, c


def _ret_mask(rows, seg, log_gamma):
    r = lax.broadcasted_iota(jnp.int32, (rows, rows), 0)
    s = lax.broadcasted_iota(jnp.int32, (rows, rows), 1)
    valid = jnp.logical_and(((r ^ s) >> int(math.log2(seg))) == 0, r >= s)
    d = jnp.where(valid, r - s, 0).astype(F32)
    return jnp.where(valid, jnp.exp(d * log_gamma) * QK_SCALE, 0.0)


def _log_gamma(hd):
    return math.log(1.0 - 2.0 ** (-5.0 - hd))


def _head_out(o, w, gate):
    return o * lax.rsqrt(jnp.mean(o * o, axis=-1, keepdims=True) + NORM_EPS) * w * _silu(gate)


def _project(x_ref, nw_ref, win_ref, proj_ref):
    h = _rms(x_ref[...], nw_ref[...]).astype(BF16)
    for g in range(8):
        cols = slice(g * GROUP_W, (g + 1) * GROUP_W)
        proj_ref[:, cols] = _dot(h, win_ref[:, cols])


def _cols(group, hd):
    return slice(group * GROUP_W + hd * DH, group * GROUP_W + (hd + 1) * DH)


def _mix_prompt_kernel(nc, *refs):
    CH = PROMPT_STEP_CHUNKS
    L = PROMPT_CHUNK
    x0_ref = refs[0]
    xn_refs = refs[1:1 + CH]
    (nw_ref, win_ref, cos_ref, sin_ref, rnw_ref, hnw_ref, lbl_ref, wout_ref,
     x1_ref, sret_ref, shg_ref,
     proj_ref, xk_ref, o_ref, srt_ref, sht_ref, dm_ref, qd_ref, kd_ref) = refs[1 + CH:]
    g = pl.program_id(0)
    ci0 = (g * CH) % nc
    t = lax.broadcasted_iota(jnp.int32, (L, DH), 0)

    @pl.when(g == 0)
    def _():
        tf = t.astype(F32)
        for hd in range(HEADS):
            lg = _log_gamma(hd)
            dm_ref[hd] = _ret_mask(L, L, lg)
            qd_ref[hd] = jnp.exp((tf + 1.0) * lg)
            kd_ref[hd] = jnp.exp((L - 1.0 - tf) * lg) * QK_SCALE
        _project(x0_ref, nw_ref, win_ref, proj_ref.at[0])
        xk_ref[0] = x0_ref[...]

    @pl.when(ci0 == 0)
    def _():
        srt_ref[...] = jnp.zeros_like(srt_ref)
        sht_ref[...] = jnp.zeros_like(sht_ref)

    lb = _lower_bound(lbl_ref[...])
    masks = _hgrn_level_masks(L, L)

    def chunk(pr, pw, xk_cur, xk_next, xn_ref, rows):
        x_next = xn_ref[...]
        xk_next[...] = x_next
        h_next = _rms(x_next, nw_ref[...]).astype(BF16)

        def project_piece(pi):
            cols = slice(pi * PROJ_PIECE, (pi + 1) * PROJ_PIECE)
            pw[:, cols] = _dot(h_next, win_ref[:, cols])

        cosf = cos_ref[rows, :]
        sinf = sin_ref[rows, :]

        for hd in range(HEADS):
            q = _rotary(pr[:, _cols(0, hd)], cosf, sinf)
            k = _rotary(pr[:, _cols(1, hd)], cosf, sinf)
            v = pr[:, _cols(2, hd)].astype(BF16)
            scores = _dot_nt(q.astype(BF16), k.astype(BF16)) * dm_ref[hd]
            project_piece(hd)
            o = _dot(scores.astype(BF16), v)
            st = srt_ref[hd]
            o = o + _dot_nt((q * qd_ref[hd]).astype(BF16), st.astype(BF16))
            kh = (k * kd_ref[hd]).astype(BF16)
            srt_ref[hd] = st * math.exp(L * _log_gamma(hd)) + _dot_tn(v, kh)
            o_ref[:, _cols(0, hd)] = _head_out(o, rnw_ref[...], pr[:, _cols(3, hd)]).astype(BF16)

        for hd in range(HEADS):
            q = _silu(pr[:, _cols(4, hd)]) * QK_SCALE
            lbh = lb[:, hd * DH:(hd + 1) * DH]
            f = lbh + (1.0 - lbh) * _sigmoid(pr[:, _cols(5, hd)])
            k = 1.0 - f
            v = pr[:, _cols(6, hd)]
            o, b = _hgrn_intra(q, k, f, v, L, t, masks,
                               functools.partial(project_piece, HEADS + hd))
            blast = b[L - 1:L, :]
            st = sht_ref[hd]
            o = o + _dot_nt((q * jnp.exp2(b)).astype(BF16), st.astype(BF16))
            kh = (k * jnp.exp2(blast - b)).astype(BF16)
            sht_ref[hd] = st * jnp.exp2(blast) + _dot_tn(v.astype(BF16), kh)
            o_ref[:, _cols(1, hd)] = _head_out(o, hnw_ref[...], pr[:, _cols(7, hd)]).astype(BF16)

        x1_ref[rows, :] = xk_cur[...] + _dot(o_ref[...], wout_ref[...])

    for sub in range(CH):
        cur, nxt = sub % 2, (sub + 1) % 2
        chunk(proj_ref.at[cur], proj_ref.at[nxt], xk_ref.at[cur], xk_ref.at[nxt], xn_refs[sub],
              slice(sub * L, (sub + 1) * L))

    @pl.when(ci0 == nc - CH)
    def _():
        for hd in range(HEADS):
            sret_ref[hd] = srt_ref[hd].T
            shg_ref[hd] = sht_ref[hd].T


def _const_spec(shape):
    return pl.BlockSpec(shape, lambda *_: (0,) * len(shape), pipeline_mode=pl.Buffered(1))


def _mix_prompt(x, nw, win, cosf, sinf, rnw, hnw, lbl, wout):
    B, T, D = x.shape
    L = PROMPT_CHUNK
    CH = PROMPT_STEP_CHUNKS
    nc = T // L
    n = B * nc
    spb = nc // CH
    assert nc % CH == 0 and CH % 2 == 0

    def next_chunk_spec(i):
        idx = lambda g: jnp.minimum(CH * g + i + 1, n - 1)
        return pl.BlockSpec((None, L, D), lambda g: (idx(g) // nc, idx(g) % nc, 0))

    state = jax.ShapeDtypeStruct((B, HEADS, DH, DH), F32)
    state_spec = pl.BlockSpec((None, HEADS, DH, DH), lambda g: (g // spb, 0, 0, 0))
    rows_spec = pl.BlockSpec((None, CH * L, D), lambda g: (g // spb, g % spb, 0))
    table_spec = pl.BlockSpec((CH * L, DH), lambda g: (g % spb, 0))
    return pl.pallas_call(
        functools.partial(_mix_prompt_kernel, nc),
        grid=(n // CH,),
        in_specs=[
            pl.BlockSpec((None, L, D), lambda g: (0, 0, 0), pipeline_mode=pl.Buffered(1)),
            *[next_chunk_spec(i) for i in range(CH)],
            _const_spec((1, D)),
            _const_spec((D, IN_COLS)),
            table_spec,
            table_spec,
            _const_spec((1, DH)),
            _const_spec((1, DH)),
            _const_spec(lbl.shape),
            _const_spec((D, D)),
        ],
        out_specs=[rows_spec, state_spec, state_spec],
        out_shape=[jax.ShapeDtypeStruct((B, T, D), F32), state, state],
        scratch_shapes=[
            pltpu.VMEM((2, L, IN_COLS), F32),
            pltpu.VMEM((2, L, D), F32),
            pltpu.VMEM((L, D), BF16),
            pltpu.VMEM((HEADS, DH, DH), F32),
            pltpu.VMEM((HEADS, DH, DH), F32),
            pltpu.VMEM((HEADS, L, L), F32),
            pltpu.VMEM((HEADS, L, DH), F32),
            pltpu.VMEM((HEADS, L, DH), F32),
        ],
        compiler_params=pltpu.CompilerParams(
            dimension_semantics=("arbitrary",), vmem_limit_bytes=VMEM_LIMIT),
        name="mix_prompt",
    )(x, *([x] * CH), nw, win, cosf, sinf, rnw, hnw, lbl, wout)


def _mix_sample_kernel(seg, x_ref, nw_ref, win_ref, cos_ref, sin_ref, rnw_ref, hnw_ref, lbl_ref,
                       wout_ref, sret_in_ref, shg_in_ref,
                       x1_ref, sret_ref, shg_ref,
                       proj_ref, o_ref, qh_ref, kh_ref, dec_ref, ob_ref):
    rows = x_ref.shape[0]
    nsub = sret_in_ref.shape[0]
    j = pl.program_id(1)

    @pl.when(j == 0)
    def _():
        _project(x_ref, nw_ref, win_ref, proj_ref)
        t = lax.broadcasted_iota(jnp.int32, (rows, DH), 0)
        tl = (t & (seg - 1)).astype(F32)
        cosf = cos_ref[...]
        sinf = sin_ref[...]
        lb = _lower_bound(lbl_ref[...])
        masks = _hgrn_level_masks(rows, seg)
        for hd in range(HEADS):
            lg = _log_gamma(hd)
            q = _rotary(proj_ref[:, _cols(0, hd)], cosf, sinf)
            k = _rotary(proj_ref[:, _cols(1, hd)], cosf, sinf)
            scores = _dot_nt(q.astype(BF16), k.astype(BF16)) * _ret_mask(rows, seg, lg)
            o_ref[:, _cols(0, hd)] = _dot(scores.astype(BF16),
                                          proj_ref[:, _cols(2, hd)].astype(BF16))
            qh_ref[:, _cols(0, hd)] = q * jnp.exp((tl + 1.0) * lg)
            kh_ref[:, _cols(0, hd)] = k * (jnp.exp((seg - 1.0 - tl) * lg) * QK_SCALE)
        for hd in range(HEADS):
            q = _silu(proj_ref[:, _cols(4, hd)]) * QK_SCALE
            lbh = lb[:, hd * DH:(hd + 1) * DH]
            f = lbh + (1.0 - lbh) * _sigmoid(proj_ref[:, _cols(5, hd)])
            k = 1.0 - f
            o, b = _hgrn_intra(q, k, f, proj_ref[:, _cols(6, hd)], seg, t, masks)
            blast = _block_row_bcast(b, seg, seg - 1)
            o_ref[:, _cols(1, hd)] = o
            qh_ref[:, _cols(1, hd)] = q * jnp.exp2(b)
            kh_ref[:, _cols(1, hd)] = k * jnp.exp2(blast - b)
            dec_ref[:, hd * DH:(hd + 1) * DH] = jnp.exp2(blast)

    def seq_body(s, carry):
        r = pl.ds(pl.multiple_of((j * nsub + s) * seg, seg), seg)
        qh = qh_ref[r, :].astype(BF16)
        kh = kh_ref[r, :].astype(BF16)
        v_ret = proj_ref[r, 2 * GROUP_W:3 * GROUP_W].astype(BF16)
        v_hg = proj_ref[r, 6 * GROUP_W:7 * GROUP_W].astype(BF16)
        dec = dec_ref[r, :][0:1, :]
        o_parts, new_ret, new_hg = [], [], []
        for hd in range(HEADS):
            st = sret_in_ref[s, hd]
            c = slice(hd * DH, (hd + 1) * DH)
            o_parts.append(_dot(qh[:, c], st.astype(BF16)))
            new_ret.append(st * math.exp(seg * _log_gamma(hd)) + _dot_tn(kh[:, c], v_ret[:, c]))
        for hd in range(HEADS):
            st = shg_in_ref[s, hd]
            c = slice(hd * DH, (hd + 1) * DH)
            c1 = slice(GROUP_W + hd * DH, GROUP_W + (hd + 1) * DH)
            o_parts.append(_dot(qh[:, c1], st.astype(BF16)))
            dcol = jnp.broadcast_to(dec[:, c], (DH, DH)).T
            new_hg.append(st * dcol + _dot_tn(kh[:, c1], v_hg[:, c]))
        o_ref[r, :] = o_ref[r, :] + jnp.concatenate(o_parts, axis=1)
        for hd in range(HEADS):
            sret_ref[s, hd] = new_ret[hd]
            shg_ref[s, hd] = new_hg[hd]
        return carry

    lax.fori_loop(0, nsub, seq_body, 0, unroll=2)

    @pl.when(j == pl.num_programs(1) - 1)
    def _():
        for hd in range(HEADS):
            c0 = _cols(0, hd)
            ob_ref[:, c0] = _head_out(o_ref[:, c0], rnw_ref[...],
                                      proj_ref[:, _cols(3, hd)]).astype(BF16)
            c1 = _cols(1, hd)
            ob_ref[:, c1] = _head_out(o_ref[:, c1], hnw_ref[...],
                                      proj_ref[:, _cols(7, hd)]).astype(BF16)
        x1_ref[...] = x_ref[...] + _dot(ob_ref[...], wout_ref[...])


def _mix_sample(x, nw, win, cosf, sinf, rnw, hnw, lbl, wout, sret, shg):
    nb, seg, D = x.shape
    rows = SAMPLE_ROWS
    nsub = SAMPLE_SUB_SEQS
    nj = rows // seg // nsub
    x2 = x.reshape(nb * seg, D)
    state = jax.ShapeDtypeStruct((nb, HEADS, DH, DH), F32)
    state_spec = pl.BlockSpec((nsub, HEADS, DH, DH), lambda i, j: (i * nj + j, 0, 0, 0))
    x1, sret_new, shg_new = pl.pallas_call(
        functools.partial(_mix_sample_kernel, seg),
        grid=(nb * seg // rows, nj),
        in_specs=[
            pl.BlockSpec((rows, D), lambda i, j: (i, 0)),
            _const_spec((1, D)),
            _const_spec((D, IN_COLS)),
            _const_spec((rows, DH)),
            _const_spec((rows, DH)),
            _const_spec((1, DH)),
            _const_spec((1, DH)),
            _const_spec(lbl.shape),
            _const_spec((D, D)),
            state_spec,
            state_spec,
        ],
        out_specs=[pl.BlockSpec((rows, D), lambda i, j: (i, 0)), state_spec, state_spec],
        out_shape=[jax.ShapeDtypeStruct((nb * seg, D), F32), state, state],
        scratch_shapes=[
            pltpu.VMEM((rows, IN_COLS), F32),
            pltpu.VMEM((rows, D), F32),
            pltpu.VMEM((rows, D), F32),
            pltpu.VMEM((rows, D), F32),
            pltpu.VMEM((rows, GROUP_W), F32),
            pltpu.VMEM((rows, D), BF16),
        ],
        compiler_params=pltpu.CompilerParams(
            dimension_semantics=("arbitrary", "arbitrary"), vmem_limit_bytes=VMEM_LIMIT),
        name="mix_sample",
    )(x2, nw, win, cosf, sinf, rnw, hnw, lbl, wout, sret, shg)
    return x1, sret_new, shg_new


def _ffn_kernel(x_ref, nw_ref, wup_ref, wdn_ref, fw_ref, y_ref):
    x = x_ref[...]
    h = _rms(x, nw_ref[...])
    acc = x
    step = D_MODEL
    for g in range(D_FF // step):
        u = jnp.maximum(_dot(h, wup_ref[:, g * step:(g + 1) * step]), 0.0)
        acc = acc + _dot(u * u, wdn_ref[g * step:(g + 1) * step, :])
    y_ref[...] = _rms(acc, fw_ref[...])


def _ffn(x, nw, wup, wdn, fw):
    n, D = x.shape
    rows = min(FFN_ROWS, n)
    return pl.pallas_call(
        _ffn_kernel,
        grid=(n // rows,),
        in_specs=[
            pl.BlockSpec((rows, D), lambda i: (i, 0)),
            _const_spec((1, D)),
            _const_spec((D, D_FF)),
            _const_spec((D_FF, D)),
            _const_spec((1, D)),
        ],
        out_specs=pl.BlockSpec((rows, D), lambda i: (i, 0)),
        out_shape=jax.ShapeDtypeStruct((n, D), F32),
        compiler_params=pltpu.CompilerParams(
            dimension_semantics=("arbitrary",), vmem_limit_bytes=VMEM_LIMIT),
        name="ffn",
    )(x, nw, wup, wdn, fw)


def _rope_tables(pos):
    half = DH // 2
    inv_freq = ROPE_BASE ** (-jnp.arange(half, dtype=F32) / half)
    ang = pos.astype(F32)[:, None] * inv_freq[None, :]
    cos, sin = jnp.cos(ang), jnp.sin(ang)
    return jnp.concatenate([cos, cos], axis=-1), jnp.concatenate([-sin, sin], axis=-1)


def kernel(x_prompt, x_sample, state_ret, state_hgrn, norm_mix_w, w_in, ret_norm_w, hgrn_norm_w,
           lb_logits, w_out, norm_ffn_w, w_up, w_down, final_norm_w):
    B, T, D = x_prompt.shape
    nb, seg, _ = x_sample.shape
    assert w_in.shape[0] == 1, "one layer"

    nw = norm_mix_w[0].reshape(1, D)
    win = w_in[0].astype(BF16)
    wout = w_out[0].astype(BF16)
    rnw = ret_norm_w[0].reshape(1, DH)
    hnw = hgrn_norm_w[0].reshape(1, DH)
    lbl = lb_logits.astype(F32)
    fnw = norm_ffn_w[0].reshape(1, D)
    wup = w_up[0]
    wdn = w_down[0]
    fw = final_norm_w.reshape(1, D)

    cos_p, sin_p = _rope_tables(jnp.arange(T, dtype=jnp.int32))
    cos_s, sin_s = _rope_tables(PAST_LEN + jnp.arange(seg, dtype=jnp.int32))
    cos_s = jnp.tile(cos_s, (SAMPLE_ROWS // seg, 1))
    sin_s = jnp.tile(sin_s, (SAMPLE_ROWS // seg, 1))

    xp1, sret_p, shg_p = _mix_prompt(x_prompt, nw, win, cos_p, sin_p, rnw, hnw, lbl, wout)
    xs1, sret_s, shg_s = _mix_sample(x_sample, nw, win, cos_s, sin_s, rnw, hnw, lbl, wout,
                                     state_ret[0], state_hgrn[0])

    y_p = _ffn(xp1.reshape(B * T, D), fnw, wup, wdn, fw).reshape(B, T, D)
    y_s = _ffn(xs1, fnw, wup, wdn, fw).reshape(nb, seg, D)
    return (y_p, y_s, sret_p[None], shg_p[None], sret_s[None], shg_s[None])
```

```python
import functools
import math

import jax
import jax.numpy as jnp
from jax import lax
from jax.experimental import pallas as pl
from jax.experimental.pallas import tpu as pltpu

D_MODEL = 1024
HEADS = 4
DH = 128
GROUP_W = HEADS * DH
IN_COLS = 8 * GROUP_W
D_FF = 4 * D_MODEL
ROPE_BASE = 10000.0
NORM_EPS = 1e-6
QK_SCALE = DH ** -0.5
PAST_LEN = 16384

PROMPT_CHUNK = 256
PROMPT_STEP_CHUNKS = 2
SAMPLE_ROWS = 256
SAMPLE_SUB_SEQS = 4
FFN_ROWS = 512
PROJ_PIECE = 512
VMEM_LIMIT = 56 * 1024 * 1024

F32 = jnp.float32
BF16 = jnp.bfloat16


def _dot(a, b):
    return jnp.dot(a, b, preferred_element_type=F32)


def _dot_nt(a, b):
    return lax.dot_general(a, b, (((1,), (1,)), ((), ())), preferred_element_type=F32)


def _dot_tn(a, b):
    return lax.dot_general(a, b, (((0,), (0,)), ((), ())), preferred_element_type=F32)


def _rms(x, w):
    ms = jnp.mean(x * x, axis=-1, keepdims=True)
    return x * lax.rsqrt(ms + NORM_EPS) * w


def _sigmoid(x):
    return 1.0 / (1.0 + jnp.exp(-x))


def _silu(x):
    return x * _sigmoid(x)


def _lower_bound(lbl):
    mx = jnp.max(lbl, axis=0, keepdims=True)
    e = jnp.exp(lbl - mx)
    return e[0:1, :] / jnp.sum(e, axis=0, keepdims=True)


def _rotary(x, cosf, sinf):
    return x * cosf + pltpu.roll(x, DH // 2, 1) * sinf


def _split_halves(x, m):
    blk = 2 * m
    nb = x.shape[0] // blk
    lo = [x[b * blk:b * blk + m] for b in range(nb)]
    up = [x[b * blk + m:(b + 1) * blk] for b in range(nb)]
    cat = lambda ps: ps[0] if len(ps) == 1 else jnp.concatenate(ps, axis=0)
    return cat(lo), cat(up)


def _merge_halves(lo, up, m):
    nb = lo.shape[0] // m
    pieces = []
    for b in range(nb):
        pieces += [lo[b * m:(b + 1) * m], up[b * m:(b + 1) * m]]
    return jnp.concatenate(pieces, axis=0)


def _block_row_bcast(x, m, row):
    nb = x.shape[0] // m
    pieces = [jnp.broadcast_to(x[b * m + row:b * m + row + 1], (m, x.shape[1])) for b in range(nb)]
    return pieces[0] if nb == 1 else jnp.concatenate(pieces, axis=0)


def _hgrn_level_masks(rows, seg):
    masks = []
    for j in range(int(math.log2(seg))):
        if (1 << j) < 8:
            r = lax.broadcasted_iota(jnp.int32, (rows, rows), 0)
            s = lax.broadcasted_iota(jnp.int32, (rows, rows), 1)
            masks.append(jnp.logical_and(((r ^ s) >> j) == 1, r > s))
        else:
            r = lax.broadcasted_iota(jnp.int32, (rows // 2, rows), 0)
            s = lax.broadcasted_iota(jnp.int32, (rows // 2, rows), 1)
            masks.append((s >> j) == 2 * (r >> j))
    return masks


def _hgrn_intra(q, k, f, v, seg, t, masks, mid_hook=None):
    rows = q.shape[0]
    c = jnp.log2(f)
    scores = None
    for j in range(int(math.log2(seg))):
        m = 1 << j
        if m < 8:
            c3 = c.reshape(rows // 8, 8, DH)
            upper = (t & m) != 0
            if m == 1:
                w = jnp.where(upper, f, 1.0)
                tot = pltpu.roll(c3, 1, 1).reshape(rows, DH)
            else:
                if m == 2:
                    low4 = ((t & 7) < 4).reshape(rows // 8, 8, DH)
                    tot3 = jnp.where(low4, jnp.broadcast_to(c3[:, 1:2, :], c3.shape),
                                     jnp.broadcast_to(c3[:, 5:6, :], c3.shape))
                else:
                    tot3 = jnp.broadcast_to(c3[:, 3:4, :], c3.shape)
                tot = tot3.reshape(rows, DH)
                w = jnp.exp2(jnp.where(upper, c, tot - c))
            part = _dot_nt((q * w).astype(BF16), (k * w).astype(BF16))
            scores = jnp.where(masks[j], part, 0.0 if scores is None else scores)
            c = jnp.where(upper, c + tot, c)
        else:
            c_lo, c_up = _split_halves(c, m)
            tot = _block_row_bcast(c_lo, m, m - 1)
            q_up = _split_halves(q, m)[1]
            k_lo, k_up = _split_halves(k, m)
            qt = (q_up * jnp.exp2(c_up)).astype(BF16)
            kt = _merge_halves(k_lo * jnp.exp2(tot - c_lo), k_up, m).astype(BF16)
            s_lo, s_up = _split_halves(scores, m)
            s_up = jnp.where(masks[j], _dot_nt(qt, kt), s_up)
            scores = _merge_halves(s_lo, s_up, m)
            c = _merge_halves(c_lo, c_up + tot, m)
    if mid_hook is not None:
        mid_hook()
    o = _dot(scores.astype(BF16), v.astype(BF16))
    o = o + jnp.sum(q * k, axis=-1, keepdims=True) * v
    return o, c


def _ret_mask(rows, seg, log_gamma):
    r = lax.broadcasted_iota(jnp.int32, (rows, rows), 0)
    s = lax.broadcasted_iota(jnp.int32, (rows, rows), 1)
    valid = jnp.logical_and(((r ^ s) >> int(math.log2(seg))) == 0, r >= s)
    d = jnp.where(valid, r - s, 0).astype(F32)
    return jnp.where(valid, jnp.exp(d * log_gamma) * QK_SCALE, 0.0)


def _log_gamma(hd):
    return math.log(1.0 - 2.0 ** (-5.0 - hd))


def _head_out(o, w, gate):
    return o * lax.rsqrt(jnp.mean(o * o, axis=-1, keepdims=True) + NORM_EPS) * w * _silu(gate)


def _project(x_ref, nw_ref, win_ref, proj_ref):
    h = _rms(x_ref[...], nw_ref[...]).astype(BF16)
    for g in range(8):
        cols = slice(g * GROUP_W, (g + 1) * GROUP_W)
        proj_ref[:, cols] = _dot(h, win_ref[:, cols])


def _ffn_rows(x, nw_ref, wup_ref, wdn_ref, fw_ref):
    h = _rms(x, nw_ref[...]).astype(BF16)
    acc = x
    step = D_MODEL
    for g in range(D_FF // step):
        u = jnp.maximum(_dot(h, wup_ref[:, g * step:(g + 1) * step]), 0.0)
        acc = acc + _dot((u * u).astype(BF16), wdn_ref[g * step:(g + 1) * step, :])
    return _rms(acc, fw_ref[...])


def _cols(group, hd):
    return slice(group * GROUP_W + hd * DH, group * GROUP_W + (hd + 1) * DH)


def _mix_prompt_kernel(nc, *refs):
    CH = PROMPT_STEP_CHUNKS
    L = PROMPT_CHUNK
    x0_ref = refs[0]
    xn_refs = refs[1:1 + CH]
    (nw_ref, win_ref, cos_ref, sin_ref, rnw_ref, hnw_ref, lbl_ref, wout_ref,
     x1_ref, sret_ref, shg_ref,
     proj_ref, xk_ref, o_ref, srt_ref, sht_ref, dm_ref, qd_ref, kd_ref) = refs[1 + CH:]
    g = pl.program_id(0)
    ci0 = (g * CH) % nc
    t = lax.broadcasted_iota(jnp.int32, (L, DH), 0)

    @pl.when(g == 0)
    def _():
        tf = t.astype(F32)
        for hd in range(HEADS):
            lg = _log_gamma(hd)
            dm_ref[hd] = _ret_mask(L, L, lg)
            qd_ref[hd] = jnp.exp((tf + 1.0) * lg)
            kd_ref[hd] = jnp.exp((L - 1.0 - tf) * lg) * QK_SCALE
        _project(x0_ref, nw_ref, win_ref, proj_ref.at[0])
        xk_ref[0] = x0_ref[...]

    @pl.when(ci0 == 0)
    def _():
        srt_ref[...] = jnp.zeros_like(srt_ref)
        sht_ref[...] = jnp.zeros_like(sht_ref)

    lb = _lower_bound(lbl_ref[...])
    masks = _hgrn_level_masks(L, L)

    def chunk(pr, pw, xk_cur, xk_next, xn_ref, rows):
        x_next = xn_ref[...]
        xk_next[...] = x_next
        h_next = _rms(x_next, nw_ref[...]).astype(BF16)

        def project_piece(pi):
            cols = slice(pi * PROJ_PIECE, (pi + 1) * PROJ_PIECE)
            pw[:, cols] = _dot(h_next, win_ref[:, cols])

        cosf = cos_ref[rows, :]
        sinf = sin_ref[rows, :]

        for hd in range(HEADS):
            q = _rotary(pr[:, _cols(0, hd)], cosf, sinf)
            k = _rotary(pr[:, _cols(1, hd)], cosf, sinf)
            v = pr[:, _cols(2, hd)].astype(BF16)
            scores = _dot_nt(q.astype(BF16), k.astype(BF16)) * dm_ref[hd]
            project_piece(hd)
            o = _dot(scores.astype(BF16), v)
            st = srt_ref[hd]
            o = o + _dot_nt((q * qd_ref[hd]).astype(BF16), st.astype(BF16))
            kh = (k * kd_ref[hd]).astype(BF16)
            srt_ref[hd] = st * math.exp(L * _log_gamma(hd)) + _dot_tn(v, kh)
            o_ref[:, _cols(0, hd)] = _head_out(o, rnw_ref[...], pr[:, _cols(3, hd)]).astype(BF16)

        for hd in range(HEADS):
            q = _silu(pr[:, _cols(4, hd)]) * QK_SCALE
            lbh = lb[:, hd * DH:(hd + 1) * DH]
            f = lbh + (1.0 - lbh) * _sigmoid(pr[:, _cols(5, hd)])
            k = 1.0 - f
            v = pr[:, _cols(6, hd)]
            o, b = _hgrn_intra(q, k, f, v, L, t, masks,
                               functools.partial(project_piece, HEADS + hd))
            blast = b[L - 1:L, :]
            st = sht_ref[hd]
            o = o + _dot_nt((q * jnp.exp2(b)).astype(BF16), st.astype(BF16))
            kh = (k * jnp.exp2(blast - b)).astype(BF16)
            sht_ref[hd] = st * jnp.exp2(blast) + _dot_tn(v.astype(BF16), kh)
            o_ref[:, _cols(1, hd)] = _head_out(o, hnw_ref[...], pr[:, _cols(7, hd)]).astype(BF16)

        x1_ref[rows, :] = xk_cur[...] + _dot(o_ref[...], wout_ref[...])

    for sub in range(CH):
        cur, nxt = sub % 2, (sub + 1) % 2
        chunk(proj_ref.at[cur], proj_ref.at[nxt], xk_ref.at[cur], xk_ref.at[nxt], xn_refs[sub],
              slice(sub * L, (sub + 1) * L))

    @pl.when(ci0 == nc - CH)
    def _():
        for hd in range(HEADS):
            sret_ref[hd] = srt_ref[hd].T
            shg_ref[hd] = sht_ref[hd].T


def _const_spec(shape):
    return pl.BlockSpec(shape, lambda *_: (0,) * len(shape), pipeline_mode=pl.Buffered(1))


def _mix_prompt(x, nw, win, cosf, sinf, rnw, hnw, lbl, wout):
    B, T, D = x.shape
    L = PROMPT_CHUNK
    CH = PROMPT_STEP_CHUNKS
    nc = T // L
    n = B * nc
    spb = nc // CH
    assert nc % CH == 0 and CH % 2 == 0

    def next_chunk_spec(i):
        idx = lambda g: jnp.minimum(CH * g + i + 1, n - 1)
        return pl.BlockSpec((None, L, D), lambda g: (idx(g) // nc, idx(g) % nc, 0))

    state = jax.ShapeDtypeStruct((B, HEADS, DH, DH), F32)
    state_spec = pl.BlockSpec((None, HEADS, DH, DH), lambda g: (g // spb, 0, 0, 0))
    rows_spec = pl.BlockSpec((None, CH * L, D), lambda g: (g // spb, g % spb, 0))
    table_spec = pl.BlockSpec((CH * L, DH), lambda g: (g % spb, 0))
    return pl.pallas_call(
        functools.partial(_mix_prompt_kernel, nc),
        grid=(n // CH,),
        in_specs=[
            pl.BlockSpec((None, L, D), lambda g: (0, 0, 0), pipeline_mode=pl.Buffered(1)),
            *[next_chunk_spec(i) for i in range(CH)],
            _const_spec((1, D)),
            _const_spec((D, IN_COLS)),
            table_spec,
            table_spec,
            _const_spec((1, DH)),
            _const_spec((1, DH)),
            _const_spec(lbl.shape),
            _const_spec((D, D)),
        ],
        out_specs=[rows_spec, state_spec, state_spec],
        out_shape=[jax.ShapeDtypeStruct((B, T, D), F32), state, state],
        scratch_shapes=[
            pltpu.VMEM((2, L, IN_COLS), F32),
            pltpu.VMEM((2, L, D), F32),
            pltpu.VMEM((L, D), BF16),
            pltpu.VMEM((HEADS, DH, DH), F32),
            pltpu.VMEM((HEADS, DH, DH), F32),
            pltpu.VMEM((HEADS, L, L), F32),
            pltpu.VMEM((HEADS, L, DH), F32),
            pltpu.VMEM((HEADS, L, DH), F32),
        ],
        compiler_params=pltpu.CompilerParams(
            dimension_semantics=("arbitrary",), vmem_limit_bytes=VMEM_LIMIT),
        name="mix_prompt",
    )(x, *([x] * CH), nw, win, cosf, sinf, rnw, hnw, lbl, wout)


def _mix_sample_kernel(seg, x_ref, nw_ref, win_ref, cos_ref, sin_ref, rnw_ref, hnw_ref, lbl_ref,
                       wout_ref, fnw_ref, wup_ref, wdn_ref, fw_ref, sret_in_ref, shg_in_ref,
                       y_ref, sret_ref, shg_ref,
                       proj_ref, o_ref, qh_ref, kh_ref, dec_ref, ob_ref):
    rows = x_ref.shape[0]
    nsub = sret_in_ref.shape[0]
    j = pl.program_id(1)

    @pl.when(j == 0)
    def _():
        _project(x_ref, nw_ref, win_ref, proj_ref)
        t = lax.broadcasted_iota(jnp.int32, (rows, DH), 0)
        tl = (t & (seg - 1)).astype(F32)
        cosf = cos_ref[...]
        sinf = sin_ref[...]
        lb = _lower_bound(lbl_ref[...])
        masks = _hgrn_level_masks(rows, seg)
        for hd in range(HEADS):
            lg = _log_gamma(hd)
            q = _rotary(proj_ref[:, _cols(0, hd)], cosf, sinf)
            k = _rotary(proj_ref[:, _cols(1, hd)], cosf, sinf)
            scores = _dot_nt(q.astype(BF16), k.astype(BF16)) * _ret_mask(rows, seg, lg)
            o_ref[:, _cols(0, hd)] = _dot(scores.astype(BF16),
                                          proj_ref[:, _cols(2, hd)].astype(BF16))
            qh_ref[:, _cols(0, hd)] = q * jnp.exp((tl + 1.0) * lg)
            kh_ref[:, _cols(0, hd)] = k * (jnp.exp((seg - 1.0 - tl) * lg) * QK_SCALE)
        for hd in range(HEADS):
            q = _silu(proj_ref[:, _cols(4, hd)]) * QK_SCALE
            lbh = lb[:, hd * DH:(hd + 1) * DH]
            f = lbh + (1.0 - lbh) * _sigmoid(proj_ref[:, _cols(5, hd)])
            k = 1.0 - f
            o, b = _hgrn_intra(q, k, f, proj_ref[:, _cols(6, hd)], seg, t, masks)
            blast = _block_row_bcast(b, seg, seg - 1)
            o_ref[:, _cols(1, hd)] = o
            qh_ref[:, _cols(1, hd)] = q * jnp.exp2(b)
            kh_ref[:, _cols(1, hd)] = k * jnp.exp2(blast - b)
            dec_ref[:, hd * DH:(hd + 1) * DH] = jnp.exp2(blast)

    def seq_body(s, carry):
        r = pl.ds(pl.multiple_of((j * nsub + s) * seg, seg), seg)
        qh = qh_ref[r, :].astype(BF16)
        kh = kh_ref[r, :].astype(BF16)
        v_ret = proj_ref[r, 2 * GROUP_W:3 * GROUP_W].astype(BF16)
        v_hg = proj_ref[r, 6 * GROUP_W:7 * GROUP_W].astype(BF16)
        dec = dec_ref[r, :][0:1, :]
        o_parts, new_ret, new_hg = [], [], []
        for hd in range(HEADS):
            st = sret_in_ref[s, hd]
            c = slice(hd * DH, (hd + 1) * DH)
            o_parts.append(_dot(qh[:, c], st.astype(BF16)))
            new_ret.append(st * math.exp(seg * _log_gamma(hd)) + _dot_tn(kh[:, c], v_ret[:, c]))
        for hd in range(HEADS):
            st = shg_in_ref[s, hd]
            c = slice(hd * DH, (hd + 1) * DH)
            c1 = slice(GROUP_W + hd * DH, GROUP_W + (hd + 1) * DH)
            o_parts.append(_dot(qh[:, c1], st.astype(BF16)))
            dcol = jnp.broadcast_to(dec[:, c], (DH, DH)).T
            new_hg.append(st * dcol + _dot_tn(kh[:, c1], v_hg[:, c]))
        o_ref[r, :] = o_ref[r, :] + jnp.concatenate(o_parts, axis=1)
        for hd in range(HEADS):
            sret_ref[s, hd] = new_ret[hd]
            shg_ref[s, hd] = new_hg[hd]
        return carry

    lax.fori_loop(0, nsub, seq_body, 0, unroll=2)

    @pl.when(j == pl.num_programs(1) - 1)
    def _():
        for hd in range(HEADS):
            c0 = _cols(0, hd)
            ob_ref[:, c0] = _head_out(o_ref[:, c0], rnw_ref[...],
                                      proj_ref[:, _cols(3, hd)]).astype(BF16)
            c1 = _cols(1, hd)
            ob_ref[:, c1] = _head_out(o_ref[:, c1], hnw_ref[...],
                                      proj_ref[:, _cols(7, hd)]).astype(BF16)
        x1 = x_ref[...] + _dot(ob_ref[...], wout_ref[...])
        y_ref[...] = _ffn_rows(x1, fnw_ref, wup_ref, wdn_ref, fw_ref)


def _mix_sample(x, nw, win, cosf, sinf, rnw, hnw, lbl, wout, fnw, wup, wdn, fw, sret, shg):
    nb, seg, D = x.shape
    rows = SAMPLE_ROWS
    nsub = SAMPLE_SUB_SEQS
    nj = rows // seg // nsub
    x2 = x.reshape(nb * seg, D)
    state = jax.ShapeDtypeStruct((nb, HEADS, DH, DH), F32)
    state_spec = pl.BlockSpec((nsub, HEADS, DH, DH), lambda i, j: (i * nj + j, 0, 0, 0))
    y, sret_new, shg_new = pl.pallas_call(
        functools.partial(_mix_sample_kernel, seg),
        grid=(nb * seg // rows, nj),
        in_specs=[
            pl.BlockSpec((rows, D), lambda i, j: (i, 0)),
            _const_spec((1, D)),
            _const_spec((D, IN_COLS)),
            _const_spec((rows, DH)),
            _const_spec((rows, DH)),
            _const_spec((1, DH)),
            _const_spec((1, DH)),
            _const_spec(lbl.shape),
            _const_spec((D, D)),
            _const_spec((1, D)),
            _const_spec((D, D_FF)),
            _const_spec((D_FF, D)),
            _const_spec((1, D)),
            state_spec,
            state_spec,
        ],
        out_specs=[pl.BlockSpec((rows, D), lambda i, j: (i, 0)), state_spec, state_spec],
        out_shape=[jax.ShapeDtypeStruct((nb * seg, D), F32), state, state],
        scratch_shapes=[
            pltpu.VMEM((rows, IN_COLS), F32),
            pltpu.VMEM((rows, D), F32),
            pltpu.VMEM((rows, D), F32),
            pltpu.VMEM((rows, D), F32),
            pltpu.VMEM((rows, GROUP_W), F32),
            pltpu.VMEM((rows, D), BF16),
        ],
        compiler_params=pltpu.CompilerParams(
            dimension_semantics=("arbitrary", "arbitrary"), vmem_limit_bytes=VMEM_LIMIT),
        name="mix_sample",
    )(x2, nw, win, cosf, sinf, rnw, hnw, lbl, wout, fnw, wup, wdn, fw, sret, shg)
    return y, sret_new, shg_new


def _ffn_kernel(x_ref, nw_ref, wup_ref, wdn_ref, fw_ref, y_ref):
    y_ref[...] = _ffn_rows(x_ref[...], nw_ref, wup_ref, wdn_ref, fw_ref)


def _ffn(x, nw, wup, wdn, fw):
    n, D = x.shape
    rows = min(FFN_ROWS, n)
    return pl.pallas_call(
        _ffn_kernel,
        grid=(n // rows,),
        in_specs=[
            pl.BlockSpec((rows, D), lambda i: (i, 0)),
            _const_spec((1, D)),
            _const_spec((D, D_FF)),
            _const_spec((D_FF, D)),
            _const_spec((1, D)),
        ],
        out_specs=pl.BlockSpec((rows, D), lambda i: (i, 0)),
        out_shape=jax.ShapeDtypeStruct((n, D), F32),
        compiler_params=pltpu.CompilerParams(
            dimension_semantics=("arbitrary",), vmem_limit_bytes=VMEM_LIMIT),
        name="ffn",
    )(x, nw, wup, wdn, fw)


def _rope_tables(pos):
    half = DH // 2
    inv_freq = ROPE_BASE ** (-jnp.arange(half, dtype=F32) / half)
    ang = pos.astype(F32)[:, None] * inv_freq[None, :]
    cos, sin = jnp.cos(ang), jnp.sin(ang)
    return jnp.concatenate([cos, cos], axis=-1), jnp.concatenate([-sin, sin], axis=-1)


def kernel(x_prompt, x_sample, state_ret, state_hgrn, norm_mix_w, w_in, ret_norm_w, hgrn_norm_w,
           lb_logits, w_out, norm_ffn_w, w_up, w_down, final_norm_w):
    B, T, D = x_prompt.shape
    nb, seg, _ = x_sample.shape
    assert w_in.shape[0] == 1, "one layer"

    nw = norm_mix_w[0].reshape(1, D)
    win = w_in[0].astype(BF16)
    wout = w_out[0].astype(BF16)
    rnw = ret_norm_w[0].reshape(1, DH)
    hnw = hgrn_norm_w[0].reshape(1, DH)
    lbl = lb_logits.astype(F32)
    fnw = norm_ffn_w[0].reshape(1, D)
    wup = w_up[0].astype(BF16)
    wdn = w_down[0].astype(BF16)
    fw = final_norm_w.reshape(1, D)

    cos_p, sin_p = _rope_tables(jnp.arange(T, dtype=jnp.int32))
    cos_s, sin_s = _rope_tables(PAST_LEN + jnp.arange(seg, dtype=jnp.int32))
    cos_s = jnp.tile(cos_s, (SAMPLE_ROWS // seg, 1))
    sin_s = jnp.tile(sin_s, (SAMPLE_ROWS // seg, 1))

    xp1, sret_p, shg_p = _mix_prompt(x_prompt, nw, win, cos_p, sin_p, rnw, hnw, lbl, wout)
    y_s, sret_s, shg_s = _mix_sample(x_sample, nw, win, cos_s, sin_s, rnw, hnw, lbl, wout,
                                     fnw, wup, wdn, fw, state_ret[0], state_hgrn[0])

    y_p = _ffn(xp1.reshape(B * T, D), fnw, wup, wdn, fw).reshape(B, T, D)
    y_s = y_s.reshape(nb, seg, D)
    return (y_p, y_s, sret_p[None], shg_p[None], sret_s[None], shg_s[None])
```

```python
import functools
import math

import jax
import jax.numpy as jnp
from jax import lax
from jax.experimental import pallas as pl
from jax.experimental.pallas import tpu as pltpu

D_MODEL = 1024
HEADS = 4
DH = 128
GROUP_W = HEADS * DH
IN_COLS = 8 * GROUP_W
D_FF = 4 * D_MODEL
ROPE_BASE = 10000.0
NORM_EPS = 1e-6
QK_SCALE = DH ** -0.5
PAST_LEN = 16384

PROMPT_CHUNK = 256
PROMPT_STEP_CHUNKS = 2
SAMPLE_ROWS = 256
SAMPLE_SUB_SEQS = 8
SAMPLE_STATE_BUFFERS = 2
FFN_ROWS = 1024
FFN_SUB_ROWS = 512
PROJ_PIECE = 512
VMEM_LIMIT = 56 * 1024 * 1024

F32 = jnp.float32
BF16 = jnp.bfloat16


def _dot(a, b):
    return jnp.dot(a, b, preferred_element_type=F32)


def _dot_nt(a, b):
    return lax.dot_general(a, b, (((1,), (1,)), ((), ())), preferred_element_type=F32)


def _dot_tn(a, b):
    return lax.dot_general(a, b, (((0,), (0,)), ((), ())), preferred_element_type=F32)


def _rms(x, w):
    ms = jnp.mean(x * x, axis=-1, keepdims=True)
    return x * lax.rsqrt(ms + NORM_EPS) * w


def _sigmoid(x):
    return 1.0 / (1.0 + jnp.exp(-x))


def _silu(x):
    return x * _sigmoid(x)


def _lower_bound(lbl):
    mx = jnp.max(lbl, axis=0, keepdims=True)
    e = jnp.exp(lbl - mx)
    return e[0:1, :] / jnp.sum(e, axis=0, keepdims=True)


def _rotary(x, cosf, sinf):
    return x * cosf + pltpu.roll(x, DH // 2, 1) * sinf


def _split_halves(x, m):
    blk = 2 * m
    nb = x.shape[0] // blk
    lo = [x[b * blk:b * blk + m] for b in range(nb)]
    up = [x[b * blk + m:(b + 1) * blk] for b in range(nb)]
    cat = lambda ps: ps[0] if len(ps) == 1 else jnp.concatenate(ps, axis=0)
    return cat(lo), cat(up)


def _merge_halves(lo, up, m):
    nb = lo.shape[0] // m
    pieces = []
    for b in range(nb):
        pieces += [lo[b * m:(b + 1) * m], up[b * m:(b + 1) * m]]
    return jnp.concatenate(pieces, axis=0)


def _block_row_bcast(x, m, row):
    nb = x.shape[0] // m
    pieces = [jnp.broadcast_to(x[b * m + row:b * m + row + 1], (m, x.shape[1])) for b in range(nb)]
    return pieces[0] if nb == 1 else jnp.concatenate(pieces, axis=0)


def _hgrn_level_masks(rows, seg):
    masks = []
    for j in range(int(math.log2(seg))):
        if (1 << j) < 8:
            r = lax.broadcasted_iota(jnp.int32, (rows, rows), 0)
            s = lax.broadcasted_iota(jnp.int32, (rows, rows), 1)
            masks.append(jnp.logical_and(((r ^ s) >> j) == 1, r > s))
        else:
            r = lax.broadcasted_iota(jnp.int32, (rows // 2, rows), 0)
            s = lax.broadcasted_iota(jnp.int32, (rows // 2, rows), 1)
            masks.append((s >> j) == 2 * (r >> j))
    return masks


def _hgrn_intra(q, k, f, v, seg, t, masks, mid_hook=None):
    rows = q.shape[0]
    c = jnp.log2(f)
    scores = None
    for j in range(int(math.log2(seg))):
        m = 1 << j
        if m < 8:
            c3 = c.reshape(rows // 8, 8, DH)
            upper = (t & m) != 0
            if m == 1:
                w = jnp.where(upper, f, 1.0)
                tot = pltpu.roll(c3, 1, 1).reshape(rows, DH)
            else:
                if m == 2:
                    low4 = ((t & 7) < 4).reshape(rows // 8, 8, DH)
                    tot3 = jnp.where(low4, jnp.broadcast_to(c3[:, 1:2, :], c3.shape),
                                     jnp.broadcast_to(c3[:, 5:6, :], c3.shape))
                else:
                    tot3 = jnp.broadcast_to(c3[:, 3:4, :], c3.shape)
                tot = tot3.reshape(rows, DH)
                w = jnp.exp2(jnp.where(upper, c, tot - c))
            part = _dot_nt((q * w).astype(BF16), (k * w).astype(BF16))
            scores = jnp.where(masks[j], part, 0.0 if scores is None else scores)
            c = jnp.where(upper, c + tot, c)
        else:
            c_lo, c_up = _split_halves(c, m)
            tot = _block_row_bcast(c_lo, m, m - 1)
            q_up = _split_halves(q, m)[1]
            k_lo, k_up = _split_halves(k, m)
            qt = (q_up * jnp.exp2(c_up)).astype(BF16)
            kt = _merge_halves(k_lo * jnp.exp2(tot - c_lo), k_up, m).astype(BF16)
            s_lo, s_up = _split_halves(scores, m)
            s_up = jnp.where(masks[j], _dot_nt(qt, kt), s_up)
            scores = _merge_halves(s_lo, s_up, m)
            c = _merge_halves(c_lo, c_up + tot, m)
    if mid_hook is not None:
        mid_hook()
    o = _dot(scores.astype(BF16), v.astype(BF16))
    o = o + jnp.sum(q * k, axis=-1, keepdims=True) * v
    return o, c


def _ret_mask(rows, seg, log_gamma):
    r = lax.broadcasted_iota(jnp.int32, (rows, rows), 0)
    s = lax.broadcasted_iota(jnp.int32, (rows, rows), 1)
    valid = jnp.logical_and(((r ^ s) >> int(math.log2(seg))) == 0, r >= s)
    d = jnp.where(valid, r - s, 0).astype(F32)
    return jnp.where(valid, jnp.exp(d * log_gamma) * QK_SCALE, 0.0)


def _log_gamma(hd):
    return math.log(1.0 - 2.0 ** (-5.0 - hd))


def _head_out(o, w, gate):
    return o * lax.rsqrt(jnp.mean(o * o, axis=-1, keepdims=True) + NORM_EPS) * w * _silu(gate)


def _project(x_ref, nw_ref, win_ref, proj_ref):
    h = _rms(x_ref[...], nw_ref[...]).astype(BF16)
    for g in range(8):
        cols = slice(g * GROUP_W, (g + 1) * GROUP_W)
        proj_ref[:, cols] = _dot(h, win_ref[:, cols])


def _cols(group, hd):
    return slice(group * GROUP_W + hd * DH, group * GROUP_W + (hd + 1) * DH)


def _mix_prompt_kernel(nc, *refs):
    CH = PROMPT_STEP_CHUNKS
    L = PROMPT_CHUNK
    x0_ref = refs[0]
    xn_refs = refs[1:1 + CH]
    (xc_ref, nw_ref, win_ref, cos_ref, sin_ref, rnw_ref, hnw_ref, lbl_ref, wout_ref,
     x1_ref, sret_ref, shg_ref,
     proj_ref, o_ref, srt_ref, sht_ref, dm_ref, qd_ref, kd_ref) = refs[1 + CH:]
    g = pl.program_id(0)
    ci0 = (g * CH) % nc
    t = lax.broadcasted_iota(jnp.int32, (L, DH), 0)

    @pl.when(g == 0)
    def _():
        tf = t.astype(F32)
        for hd in range(HEADS):
            lg = _log_gamma(hd)
            dm_ref[hd] = _ret_mask(L, L, lg)
            qd_ref[hd] = jnp.exp((tf + 1.0) * lg)
            kd_ref[hd] = jnp.exp((L - 1.0 - tf) * lg) * QK_SCALE
        _project(x0_ref, nw_ref, win_ref, proj_ref.at[0])

    @pl.when(ci0 == 0)
    def _():
        srt_ref[...] = jnp.zeros_like(srt_ref)
        sht_ref[...] = jnp.zeros_like(sht_ref)

    lb = _lower_bound(lbl_ref[...])
    masks = _hgrn_level_masks(L, L)

    def chunk(pr, pw, xn_ref, rows):
        h_next = _rms(xn_ref[...], nw_ref[...]).astype(BF16)

        def project_piece(pi):
            cols = slice(pi * PROJ_PIECE, (pi + 1) * PROJ_PIECE)
            pw[:, cols] = _dot(h_next, win_ref[:, cols])

        cosf = cos_ref[rows, :]
        sinf = sin_ref[rows, :]

        for hd in range(HEADS):
            q = _rotary(pr[:, _cols(0, hd)], cosf, sinf)
            k = _rotary(pr[:, _cols(1, hd)], cosf, sinf)
            v = pr[:, _cols(2, hd)].astype(BF16)
            scores = _dot_nt(q.astype(BF16), k.astype(BF16)) * dm_ref[hd]
            project_piece(hd)
            o = _dot(scores.astype(BF16), v)
            st = srt_ref[hd]
            o = o + _dot_nt((q * qd_ref[hd]).astype(BF16), st.astype(BF16))
            kh = (k * kd_ref[hd]).astype(BF16)
            srt_ref[hd] = st * math.exp(L * _log_gamma(hd)) + _dot_tn(v, kh)
            o_ref[:, _cols(0, hd)] = _head_out(o, rnw_ref[...], pr[:, _cols(3, hd)]).astype(BF16)

        for hd in range(HEADS):
            q = _silu(pr[:, _cols(4, hd)]) * QK_SCALE
            lbh = lb[:, hd * DH:(hd + 1) * DH]
            f = lbh + (1.0 - lbh) * _sigmoid(pr[:, _cols(5, hd)])
            k = 1.0 - f
            v = pr[:, _cols(6, hd)]
            o, b = _hgrn_intra(q, k, f, v, L, t, masks,
                               functools.partial(project_piece, HEADS + hd))
            blast = b[L - 1:L, :]
            st = sht_ref[hd]
            o = o + _dot_nt((q * jnp.exp2(b)).astype(BF16), st.astype(BF16))
            kh = (k * jnp.exp2(blast - b)).astype(BF16)
            sht_ref[hd] = st * jnp.exp2(blast) + _dot_tn(v.astype(BF16), kh)
            o_ref[:, _cols(1, hd)] = _head_out(o, hnw_ref[...], pr[:, _cols(7, hd)]).astype(BF16)

        x1_ref[rows, :] = xc_ref[rows, :] + _dot(o_ref[...], wout_ref[...])

    for sub in range(CH):
        chunk(proj_ref.at[sub % 2], proj_ref.at[(sub + 1) % 2], xn_refs[sub],
              slice(sub * L, (sub + 1) * L))

    @pl.when(ci0 == nc - CH)
    def _():
        for hd in range(HEADS):
            sret_ref[hd] = srt_ref[hd].T
            shg_ref[hd] = sht_ref[hd].T


def _const_spec(shape):
    return pl.BlockSpec(shape, lambda *_: (0,) * len(shape), pipeline_mode=pl.Buffered(1))


def _mix_prompt(x, nw, win, cosf, sinf, rnw, hnw, lbl, wout):
    B, T, D = x.shape
    L = PROMPT_CHUNK
    CH = PROMPT_STEP_CHUNKS
    nc = T // L
    n = B * nc
    spb = nc // CH
    assert nc % CH == 0 and CH % 2 == 0

    def next_chunk_spec(i):
        idx = lambda g: jnp.minimum(CH * g + i + 1, n - 1)
        return pl.BlockSpec((None, L, D), lambda g: (idx(g) // nc, idx(g) % nc, 0))

    state = jax.ShapeDtypeStruct((B, HEADS, DH, DH), F32)
    state_spec = pl.BlockSpec((None, HEADS, DH, DH), lambda g: (g // spb, 0, 0, 0))
    rows_spec = pl.BlockSpec((None, CH * L, D), lambda g: (g // spb, g % spb, 0))
    table_spec = pl.BlockSpec((CH * L, DH), lambda g: (g % spb, 0))
    return pl.pallas_call(
        functools.partial(_mix_prompt_kernel, nc),
        grid=(n // CH,),
        in_specs=[
            pl.BlockSpec((None, L, D), lambda g: (0, 0, 0), pipeline_mode=pl.Buffered(1)),
            *[next_chunk_spec(i) for i in range(CH)],
            rows_spec,
            _const_spec((1, D)),
            _const_spec((D, IN_COLS)),
            table_spec,
            table_spec,
            _const_spec((1, DH)),
            _const_spec((1, DH)),
            _const_spec(lbl.shape),
            _const_spec((D, D)),
        ],
        out_specs=[rows_spec, state_spec, state_spec],
        out_shape=[jax.ShapeDtypeStruct((B, T, D), F32), state, state],
        scratch_shapes=[
            pltpu.VMEM((2, L, IN_COLS), F32),
            pltpu.VMEM((L, D), BF16),
            pltpu.VMEM((HEADS, DH, DH), F32),
            pltpu.VMEM((HEADS, DH, DH), F32),
            pltpu.VMEM((HEADS, L, L), F32),
            pltpu.VMEM((HEADS, L, DH), F32),
            pltpu.VMEM((HEADS, L, DH), F32),
        ],
        compiler_params=pltpu.CompilerParams(
            dimension_semantics=("arbitrary",), vmem_limit_bytes=VMEM_LIMIT),
        name="mix_prompt",
    )(x, *([x] * CH), x, nw, win, cosf, sinf, rnw, hnw, lbl, wout)


def _mix_sample_kernel(seg, x_ref, nw_ref, win_ref, cos_ref, sin_ref, rnw_ref, hnw_ref, lbl_ref,
                       wout_ref, sret_in_ref, shg_in_ref,
                       x1_ref, sret_ref, shg_ref,
                       proj_ref, o_ref, qh_ref, kh_ref, dec_ref, ob_ref):
    rows = x_ref.shape[0]
    nsub = sret_in_ref.shape[0]
    j = pl.program_id(1)

    @pl.when(j == 0)
    def _():
        _project(x_ref, nw_ref, win_ref, proj_ref)
        t = lax.broadcasted_iota(jnp.int32, (rows, DH), 0)
        tl = (t & (seg - 1)).astype(F32)
        cosf = cos_ref[...]
        sinf = sin_ref[...]
        lb = _lower_bound(lbl_ref[...])
        masks = _hgrn_level_masks(rows, seg)
        for hd in range(HEADS):
            lg = _log_gamma(hd)
            q = _rotary(proj_ref[:, _cols(0, hd)], cosf, sinf)
            k = _rotary(proj_ref[:, _cols(1, hd)], cosf, sinf)
            scores = _dot_nt(q.astype(BF16), k.astype(BF16)) * _ret_mask(rows, seg, lg)
            o_ref[:, _cols(0, hd)] = _dot(scores.astype(BF16),
                                          proj_ref[:, _cols(2, hd)].astype(BF16))
            qh_ref[:, _cols(0, hd)] = q * jnp.exp((tl + 1.0) * lg)
            kh_ref[:, _cols(0, hd)] = k * (jnp.exp((seg - 1.0 - tl) * lg) * QK_SCALE)
        for hd in range(HEADS):
            q = _silu(proj_ref[:, _cols(4, hd)]) * QK_SCALE
            lbh = lb[:, hd * DH:(hd + 1) * DH]
            f = lbh + (1.0 - lbh) * _sigmoid(proj_ref[:, _cols(5, hd)])
            k = 1.0 - f
            o, b = _hgrn_intra(q, k, f, proj_ref[:, _cols(6, hd)], seg, t, masks)
            blast = _block_row_bcast(b, seg, seg - 1)
            o_ref[:, _cols(1, hd)] = o
            qh_ref[:, _cols(1, hd)] = q * jnp.exp2(b)
            kh_ref[:, _cols(1, hd)] = k * jnp.exp2(blast - b)
            dec_ref[:, hd * DH:(hd + 1) * DH] = jnp.exp2(blast)

    def seq_body(s, carry):
        r = pl.ds(pl.multiple_of((j * nsub + s) * seg, seg), seg)
        qh = qh_ref[r, :].astype(BF16)
        kh = kh_ref[r, :].astype(BF16)
        v_ret = proj_ref[r, 2 * GROUP_W:3 * GROUP_W].astype(BF16)
        v_hg = proj_ref[r, 6 * GROUP_W:7 * GROUP_W].astype(BF16)
        dec = dec_ref[r, :][0:1, :]
        o_parts, new_ret, new_hg = [], [], []
        for hd in range(HEADS):
            st = sret_in_ref[s, hd]
            c = slice(hd * DH, (hd + 1) * DH)
            o_parts.append(_dot(qh[:, c], st.astype(BF16)))
            new_ret.append(st * math.exp(seg * _log_gamma(hd)) + _dot_tn(kh[:, c], v_ret[:, c]))
        for hd in range(HEADS):
            st = shg_in_ref[s, hd]
            c = slice(hd * DH, (hd + 1) * DH)
            c1 = slice(GROUP_W + hd * DH, GROUP_W + (hd + 1) * DH)
            o_parts.append(_dot(qh[:, c1], st.astype(BF16)))
            dcol = jnp.broadcast_to(dec[:, c], (DH, DH)).T
            new_hg.append(st * dcol + _dot_tn(kh[:, c1], v_hg[:, c]))
        o_ref[r, :] = o_ref[r, :] + jnp.concatenate(o_parts, axis=1)
        for hd in range(HEADS):
            sret_ref[s, hd] = new_ret[hd]
            shg_ref[s, hd] = new_hg[hd]
        return carry

    lax.fori_loop(0, nsub, seq_body, 0, unroll=2)

    @pl.when(j == pl.num_programs(1) - 1)
    def _():
        for hd in range(HEADS):
            c0 = _cols(0, hd)
            ob_ref[:, c0] = _head_out(o_ref[:, c0], rnw_ref[...],
                                      proj_ref[:, _cols(3, hd)]).astype(BF16)
            c1 = _cols(1, hd)
            ob_ref[:, c1] = _head_out(o_ref[:, c1], hnw_ref[...],
                                      proj_ref[:, _cols(7, hd)]).astype(BF16)
        x1_ref[...] = x_ref[...] + _dot(ob_ref[...], wout_ref[...])


def _mix_sample(x, nw, win, cosf, sinf, rnw, hnw, lbl, wout, sret, shg):
    nb, seg, D = x.shape
    rows = SAMPLE_ROWS
    nsub = SAMPLE_SUB_SEQS
    nj = rows // seg // nsub
    x2 = x.reshape(nb * seg, D)
    state = jax.ShapeDtypeStruct((nb, HEADS, DH, DH), F32)
    state_map = lambda i, j: (i * nj + j, 0, 0, 0)
    state_in_spec = pl.BlockSpec((nsub, HEADS, DH, DH), state_map,
                                 pipeline_mode=pl.Buffered(SAMPLE_STATE_BUFFERS))
    state_out_spec = pl.BlockSpec((nsub, HEADS, DH, DH), state_map)
    x1, sret_new, shg_new = pl.pallas_call(
        functools.partial(_mix_sample_kernel, seg),
        grid=(nb * seg // rows, nj),
        in_specs=[
            pl.BlockSpec((rows, D), lambda i, j: (i, 0)),
            _const_spec((1, D)),
            _const_spec((D, IN_COLS)),
            _const_spec((rows, DH)),
            _const_spec((rows, DH)),
            _const_spec((1, DH)),
            _const_spec((1, DH)),
            _const_spec(lbl.shape),
            _const_spec((D, D)),
            state_in_spec,
            state_in_spec,
        ],
        out_specs=[pl.BlockSpec((rows, D), lambda i, j: (i, 0)), state_out_spec, state_out_spec],
        out_shape=[jax.ShapeDtypeStruct((nb * seg, D), F32), state, state],
        scratch_shapes=[
            pltpu.VMEM((rows, IN_COLS), F32),
            pltpu.VMEM((rows, D), F32),
            pltpu.VMEM((rows, D), F32),
            pltpu.VMEM((rows, D), F32),
            pltpu.VMEM((rows, GROUP_W), F32),
            pltpu.VMEM((rows, D), BF16),
        ],
        compiler_params=pltpu.CompilerParams(
            dimension_semantics=("arbitrary", "arbitrary"), vmem_limit_bytes=VMEM_LIMIT),
        name="mix_sample",
    )(x2, nw, win, cosf, sinf, rnw, hnw, lbl, wout, sret, shg)
    return x1, sret_new, shg_new


def _ffn_kernel(x_ref, nw_ref, wup_ref, wdn_ref, fw_ref, y_ref):
    subs = [slice(r * FFN_SUB_ROWS, (r + 1) * FFN_SUB_ROWS)
            for r in range(x_ref.shape[0] // FFN_SUB_ROWS)]
    hs = [_rms(x_ref[rs, :], nw_ref[...]).astype(BF16) for rs in subs]
    accs = [x_ref[rs, :] for rs in subs]
    step = D_MODEL
    for g in range(D_FF // step):
        us = [jnp.maximum(_dot(h, wup_ref[:, g * step:(g + 1) * step]), 0.0) for h in hs]
        accs = [acc + _dot((u * u).astype(BF16), wdn_ref[g * step:(g + 1) * step, :])
                for acc, u in zip(accs, us)]
    for rs, acc in zip(subs, accs):
        y_ref[rs, :] = _rms(acc, fw_ref[...])


def _ffn(x, nw, wup, wdn, fw):
    n, D = x.shape
    rows = min(FFN_ROWS, n)
    return pl.pallas_call(
        _ffn_kernel,
        grid=(n // rows,),
        in_specs=[
            pl.BlockSpec((rows, D), lambda i: (i, 0)),
            _const_spec((1, D)),
            _const_spec((D, D_FF)),
            _const_spec((D_FF, D)),
            _const_spec((1, D)),
        ],
        out_specs=pl.BlockSpec((rows, D), lambda i: (i, 0)),
        out_shape=jax.ShapeDtypeStruct((n, D), F32),
        compiler_params=pltpu.CompilerParams(
            dimension_semantics=("arbitrary",), vmem_limit_bytes=VMEM_LIMIT),
        name="ffn",
    )(x, nw, wup, wdn, fw)


def _rope_tables(pos):
    half = DH // 2
    inv_freq = ROPE_BASE ** (-jnp.arange(half, dtype=F32) / half)
    ang = pos.astype(F32)[:, None] * inv_freq[None, :]
    cos, sin = jnp.cos(ang), jnp.sin(ang)
    return jnp.concatenate([cos, cos], axis=-1), jnp.concatenate([-sin, sin], axis=-1)


def kernel(x_prompt, x_sample, state_ret, state_hgrn, norm_mix_w, w_in, ret_norm_w, hgrn_norm_w,
           lb_logits, w_out, norm_ffn_w, w_up, w_down, final_norm_w):
    B, T, D = x_prompt.shape
    nb, seg, _ = x_sample.shape
    assert w_in.shape[0] == 1, "one layer"

    nw = norm_mix_w[0].reshape(1, D)
    win = w_in[0].astype(BF16)
    wout = w_out[0].astype(BF16)
    rnw = ret_norm_w[0].reshape(1, DH)
    hnw = hgrn_norm_w[0].reshape(1, DH)
    lbl = lb_logits.astype(F32)
    fnw = norm_ffn_w[0].reshape(1, D)
    wup = w_up[0].astype(BF16)
    wdn = w_down[0].astype(BF16)
    fw = final_norm_w.reshape(1, D)

    cos_p, sin_p = _rope_tables(jnp.arange(T, dtype=jnp.int32))
    cos_s, sin_s = _rope_tables(PAST_LEN + jnp.arange(seg, dtype=jnp.int32))
    cos_s = jnp.tile(cos_s, (SAMPLE_ROWS // seg, 1))
    sin_s = jnp.tile(sin_s, (SAMPLE_ROWS // seg, 1))

    xp1, sret_p, shg_p = _mix_prompt(x_prompt, nw, win, cos_p, sin_p, rnw, hnw, lbl, wout)
    xs1, sret_s, shg_s = _mix_sample(x_sample, nw, win, cos_s, sin_s, rnw, hnw, lbl, wout,
                                     state_ret[0], state_hgrn[0])

    y_p = _ffn(xp1.reshape(B * T, D), fnw, wup, wdn, fw).reshape(B, T, D)
    y_s = _ffn(xs1, fnw, wup, wdn, fw).reshape(nb, seg, D)
    return (y_p, y_s, sret_p[None], shg_p[None], sret_s[None], shg_s[None])
```

```python
import functools
import math

import jax
import jax.numpy as jnp
from jax import lax
from jax.experimental import pallas as pl
from jax.experimental.pallas import tpu as pltpu

D_MODEL = 1024
HEADS = 4
DH = 128
GROUP_W = HEADS * DH
IN_COLS = 8 * GROUP_W
D_FF = 4 * D_MODEL
ROPE_BASE = 10000.0
NORM_EPS = 1e-6
QK_SCALE = DH ** -0.5
PAST_LEN = 16384

PROMPT_CHUNK = 256
PROMPT_STEP_CHUNKS = 2
SAMPLE_ROWS = 256
SAMPLE_SUB_SEQS = 8
SAMPLE_STATE_BUFFERS = 2
FFN_ROWS = 1024
FFN_SUB_ROWS = 512
PROJ_PIECE = 512
CAST_CHUNK_BYTES = 2 * 1024 * 1024
VMEM_LIMIT = 56 * 1024 * 1024

F32 = jnp.float32
BF16 = jnp.bfloat16


def _dot(a, b):
    return jnp.dot(a, b, preferred_element_type=F32)


def _dot_nt(a, b):
    return lax.dot_general(a, b, (((1,), (1,)), ((), ())), preferred_element_type=F32)


def _dot_tn(a, b):
    return lax.dot_general(a, b, (((0,), (0,)), ((), ())), preferred_element_type=F32)


def _rms(x, w):
    ms = jnp.mean(x * x, axis=-1, keepdims=True)
    return x * lax.rsqrt(ms + NORM_EPS) * w


def _sigmoid(x):
    return 1.0 / (1.0 + jnp.exp(-x))


def _silu(x):
    return x * _sigmoid(x)


def _lower_bound(lbl):
    mx = jnp.max(lbl, axis=0, keepdims=True)
    e = jnp.exp(lbl - mx)
    return e[0:1, :] / jnp.sum(e, axis=0, keepdims=True)


def _rotary(x, cosf, sinf):
    return x * cosf + pltpu.roll(x, DH // 2, 1) * sinf


def _split_halves(x, m):
    blk = 2 * m
    nb = x.shape[0] // blk
    lo = [x[b * blk:b * blk + m] for b in range(nb)]
    up = [x[b * blk + m:(b + 1) * blk] for b in range(nb)]
    cat = lambda ps: ps[0] if len(ps) == 1 else jnp.concatenate(ps, axis=0)
    return cat(lo), cat(up)


def _merge_halves(lo, up, m):
    nb = lo.shape[0] // m
    pieces = []
    for b in range(nb):
        pieces += [lo[b * m:(b + 1) * m], up[b * m:(b + 1) * m]]
    return jnp.concatenate(pieces, axis=0)


def _block_row_bcast(x, m, row):
    nb = x.shape[0] // m
    pieces = [jnp.broadcast_to(x[b * m + row:b * m + row + 1], (m, x.shape[1])) for b in range(nb)]
    return pieces[0] if nb == 1 else jnp.concatenate(pieces, axis=0)


def _hgrn_level_masks(rows, seg):
    masks = []
    for j in range(int(math.log2(seg))):
        if (1 << j) < 8:
            r = lax.broadcasted_iota(jnp.int32, (rows, rows), 0)
            s = lax.broadcasted_iota(jnp.int32, (rows, rows), 1)
            masks.append(jnp.logical_and(((r ^ s) >> j) == 1, r > s))
        else:
            r = lax.broadcasted_iota(jnp.int32, (rows // 2, rows), 0)
            s = lax.broadcasted_iota(jnp.int32, (rows // 2, rows), 1)
            masks.append((s >> j) == 2 * (r >> j))
    return masks


def _hgrn_intra(q, k, f, v, seg, t, masks, mid_hook=None):
    rows = q.shape[0]
    c = jnp.log2(f)
    scores = None
    for j in range(int(math.log2(seg))):
        m = 1 << j
        if m < 8:
            c3 = c.reshape(rows // 8, 8, DH)
            upper = (t & m) != 0
            if m == 1:
                w = jnp.where(upper, f, 1.0)
                tot = pltpu.roll(c3, 1, 1).reshape(rows, DH)
            else:
                if m == 2:
                    low4 = ((t & 7) < 4).reshape(rows // 8, 8, DH)
                    tot3 = jnp.where(low4, jnp.broadcast_to(c3[:, 1:2, :], c3.shape),
                                     jnp.broadcast_to(c3[:, 5:6, :], c3.shape))
                else:
                    tot3 = jnp.broadcast_to(c3[:, 3:4, :], c3.shape)
                tot = tot3.reshape(rows, DH)
                w = jnp.exp2(jnp.where(upper, c, tot - c))
            part = _dot_nt((q * w).astype(BF16), (k * w).astype(BF16))
            scores = jnp.where(masks[j], part, 0.0 if scores is None else scores)
            c = jnp.where(upper, c + tot, c)
        else:
            c_lo, c_up = _split_halves(c, m)
            tot = _block_row_bcast(c_lo, m, m - 1)
            q_up = _split_halves(q, m)[1]
            k_lo, k_up = _split_halves(k, m)
            qt = (q_up * jnp.exp2(c_up)).astype(BF16)
            kt = _merge_halves(k_lo * jnp.exp2(tot - c_lo), k_up, m).astype(BF16)
            s_lo, s_up = _split_halves(scores, m)
            s_up = jnp.where(masks[j], _dot_nt(qt, kt), s_up)
            scores = _merge_halves(s_lo, s_up, m)
            c = _merge_halves(c_lo, c_up + tot, m)
    if mid_hook is not None:
        mid_hook()
    o = _dot(scores.astype(BF16), v.astype(BF16))
    o = o + jnp.sum(q * k, axis=-1, keepdims=True) * v
    return o, c


def _ret_mask(rows, seg, log_gamma):
    r = lax.broadcasted_iota(jnp.int32, (rows, rows), 0)
    s = lax.broadcasted_iota(jnp.int32, (rows, rows), 1)
    valid = jnp.logical_and(((r ^ s) >> int(math.log2(seg))) == 0, r >= s)
    d = jnp.where(valid, r - s, 0).astype(F32)
    return jnp.where(valid, jnp.exp(d * log_gamma) * QK_SCALE, 0.0)


def _log_gamma(hd):
    return math.log(1.0 - 2.0 ** (-5.0 - hd))


def _head_out(o, w, gate):
    return o * lax.rsqrt(jnp.mean(o * o, axis=-1, keepdims=True) + NORM_EPS) * w * _silu(gate)


def _project(x_ref, nw_ref, win_ref, proj_ref):
    h = _rms(x_ref[...], nw_ref[...]).astype(BF16)
    for g in range(8):
        cols = slice(g * GROUP_W, (g + 1) * GROUP_W)
        proj_ref[:, cols] = _dot(h, win_ref[:, cols])


def _cast_weight(w_hbm, w_bf_ref, stage_ref, sem, chunk_rows):
    rows, width = w_hbm.shape
    n = rows // chunk_rows

    def copy(i):
        return pltpu.make_async_copy(
            w_hbm.at[pl.ds(i * chunk_rows, chunk_rows), :],
            stage_ref.at[i % 2, pl.ds(0, chunk_rows), pl.ds(0, width)],
            sem.at[i % 2])

    copy(0).start()
    for i in range(n):
        if i + 1 < n:
            copy(i + 1).start()
        copy(i).wait()
        w_bf_ref[pl.ds(i * chunk_rows, chunk_rows), :] = (
            stage_ref[i % 2, 0:chunk_rows, 0:width].astype(BF16))


def _cols(group, hd):
    return slice(group * GROUP_W + hd * DH, group * GROUP_W + (hd + 1) * DH)


def _mix_prompt_kernel(nc, *refs):
    CH = PROMPT_STEP_CHUNKS
    L = PROMPT_CHUNK
    x0_ref = refs[0]
    xn_refs = refs[1:1 + CH]
    (xc_ref, nw_ref, win_hbm, cos_ref, sin_ref, rnw_ref, hnw_ref, lbl_ref, wout_hbm,
     x1_ref, sret_ref, shg_ref, win_ref, wout_ref,
     proj_ref, o_ref, srt_ref, sht_ref, dm_ref, qd_ref, kd_ref, stage_ref, sem) = refs[1 + CH:]
    g = pl.program_id(0)
    ci0 = (g * CH) % nc
    t = lax.broadcasted_iota(jnp.int32, (L, DH), 0)

    @pl.when(g == 0)
    def _():
        _cast_weight(win_hbm, win_ref, stage_ref, sem, stage_ref.shape[1])
        _cast_weight(wout_hbm, wout_ref, stage_ref, sem, stage_ref.shape[1])
        tf = t.astype(F32)
        for hd in range(HEADS):
            lg = _log_gamma(hd)
            dm_ref[hd] = _ret_mask(L, L, lg)
            qd_ref[hd] = jnp.exp((tf + 1.0) * lg)
            kd_ref[hd] = jnp.exp((L - 1.0 - tf) * lg) * QK_SCALE
        _project(x0_ref, nw_ref, win_ref, proj_ref.at[0])

    @pl.when(ci0 == 0)
    def _():
        srt_ref[...] = jnp.zeros_like(srt_ref)
        sht_ref[...] = jnp.zeros_like(sht_ref)

    lb = _lower_bound(lbl_ref[...])
    masks = _hgrn_level_masks(L, L)

    def chunk(pr, pw, xn_ref, rows):
        h_next = _rms(xn_ref[...], nw_ref[...]).astype(BF16)

        def project_piece(pi):
            cols = slice(pi * PROJ_PIECE, (pi + 1) * PROJ_PIECE)
            pw[:, cols] = _dot(h_next, win_ref[:, cols])

        cosf = cos_ref[rows, :]
        sinf = sin_ref[rows, :]

        for hd in range(HEADS):
            q = _rotary(pr[:, _cols(0, hd)], cosf, sinf)
            k = _rotary(pr[:, _cols(1, hd)], cosf, sinf)
            v = pr[:, _cols(2, hd)].astype(BF16)
            scores = _dot_nt(q.astype(BF16), k.astype(BF16)) * dm_ref[hd]
            project_piece(hd)
            o = _dot(scores.astype(BF16), v)
            st = srt_ref[hd]
            o = o + _dot_nt((q * qd_ref[hd]).astype(BF16), st.astype(BF16))
            kh = (k * kd_ref[hd]).astype(BF16)
            srt_ref[hd] = st * math.exp(L * _log_gamma(hd)) + _dot_tn(v, kh)
            o_ref[:, _cols(0, hd)] = _head_out(o, rnw_ref[...], pr[:, _cols(3, hd)]).astype(BF16)

        for hd in range(HEADS):
            q = _silu(pr[:, _cols(4, hd)]) * QK_SCALE
            lbh = lb[:, hd * DH:(hd + 1) * DH]
            f = lbh + (1.0 - lbh) * _sigmoid(pr[:, _cols(5, hd)])
            k = 1.0 - f
            v = pr[:, _cols(6, hd)]
            o, b = _hgrn_intra(q, k, f, v, L, t, masks,
                               functools.partial(project_piece, HEADS + hd))
            blast = b[L - 1:L, :]
            st = sht_ref[hd]
            o = o + _dot_nt((q * jnp.exp2(b)).astype(BF16), st.astype(BF16))
            kh = (k * jnp.exp2(blast - b)).astype(BF16)
            sht_ref[hd] = st * jnp.exp2(blast) + _dot_tn(v.astype(BF16), kh)
            o_ref[:, _cols(1, hd)] = _head_out(o, hnw_ref[...], pr[:, _cols(7, hd)]).astype(BF16)

        x1_ref[rows, :] = xc_ref[rows, :] + _dot(o_ref[...], wout_ref[...])

    for sub in range(CH):
        chunk(proj_ref.at[sub % 2], proj_ref.at[(sub + 1) % 2], xn_refs[sub],
              slice(sub * L, (sub + 1) * L))

    @pl.when(ci0 == nc - CH)
    def _():
        for hd in range(HEADS):
            sret_ref[hd] = srt_ref[hd].T
            shg_ref[hd] = sht_ref[hd].T


def _const_spec(shape):
    return pl.BlockSpec(shape, lambda *_: (0,) * len(shape), pipeline_mode=pl.Buffered(1))


def _mix_prompt(x, nw, win_f32, cosf, sinf, rnw, hnw, lbl, wout_f32):
    B, T, D = x.shape
    L = PROMPT_CHUNK
    CH = PROMPT_STEP_CHUNKS
    nc = T // L
    n = B * nc
    spb = nc // CH
    assert nc % CH == 0 and CH % 2 == 0

    def next_chunk_spec(i):
        idx = lambda g: jnp.minimum(CH * g + i + 1, n - 1)
        return pl.BlockSpec((None, L, D), lambda g: (idx(g) // nc, idx(g) % nc, 0))

    state = jax.ShapeDtypeStruct((B, HEADS, DH, DH), F32)
    state_spec = pl.BlockSpec((None, HEADS, DH, DH), lambda g: (g // spb, 0, 0, 0))
    rows_spec = pl.BlockSpec((None, CH * L, D), lambda g: (g // spb, g % spb, 0))
    table_spec = pl.BlockSpec((CH * L, DH), lambda g: (g % spb, 0))
    return pl.pallas_call(
        functools.partial(_mix_prompt_kernel, nc),
        grid=(n // CH,),
        in_specs=[
            pl.BlockSpec((None, L, D), lambda g: (0, 0, 0), pipeline_mode=pl.Buffered(1)),
            *[next_chunk_spec(i) for i in range(CH)],
            rows_spec,
            _const_spec((1, D)),
            pl.BlockSpec(memory_space=pl.ANY),
            table_spec,
            table_spec,
            _const_spec((1, DH)),
            _const_spec((1, DH)),
            _const_spec(lbl.shape),
            pl.BlockSpec(memory_space=pl.ANY),
        ],
        out_specs=[rows_spec, state_spec, state_spec,
                   pl.BlockSpec((D, IN_COLS), lambda g: (0, 0), pipeline_mode=pl.Buffered(1)),
                   pl.BlockSpec((D, D), lambda g: (0, 0), pipeline_mode=pl.Buffered(1))],
        out_shape=[jax.ShapeDtypeStruct((B, T, D), F32), state, state,
                   jax.ShapeDtypeStruct((D, IN_COLS), BF16), jax.ShapeDtypeStruct((D, D), BF16)],
        scratch_shapes=[
            pltpu.VMEM((2, L, IN_COLS), F32),
            pltpu.VMEM((L, D), BF16),
            pltpu.VMEM((HEADS, DH, DH), F32),
            pltpu.VMEM((HEADS, DH, DH), F32),
            pltpu.VMEM((HEADS, L, L), F32),
            pltpu.VMEM((HEADS, L, DH), F32),
            pltpu.VMEM((HEADS, L, DH), F32),
            pltpu.VMEM((2, CAST_CHUNK_BYTES // (4 * IN_COLS), IN_COLS), F32),
            pltpu.SemaphoreType.DMA((2,)),
        ],
        compiler_params=pltpu.CompilerParams(
            dimension_semantics=("arbitrary",), vmem_limit_bytes=VMEM_LIMIT),
        name="mix_prompt",
    )(x, *([x] * CH), x, nw, win_f32, cosf, sinf, rnw, hnw, lbl, wout_f32)


def _mix_sample_kernel(seg, x_ref, nw_ref, win_ref, cos_ref, sin_ref, rnw_ref, hnw_ref, lbl_ref,
                       wout_ref, sret_in_ref, shg_in_ref,
                       x1_ref, sret_ref, shg_ref,
                       proj_ref, o_ref, qh_ref, kh_ref, dec_ref, ob_ref):
    rows = x_ref.shape[0]
    nsub = sret_in_ref.shape[0]
    j = pl.program_id(1)

    @pl.when(j == 0)
    def _():
        _project(x_ref, nw_ref, win_ref, proj_ref)
        t = lax.broadcasted_iota(jnp.int32, (rows, DH), 0)
        tl = (t & (seg - 1)).astype(F32)
        cosf = cos_ref[...]
        sinf = sin_ref[...]
        lb = _lower_bound(lbl_ref[...])
        masks = _hgrn_level_masks(rows, seg)
        for hd in range(HEADS):
            lg = _log_gamma(hd)
            q = _rotary(proj_ref[:, _cols(0, hd)], cosf, sinf)
            k = _rotary(proj_ref[:, _cols(1, hd)], cosf, sinf)
            scores = _dot_nt(q.astype(BF16), k.astype(BF16)) * _ret_mask(rows, seg, lg)
            o_ref[:, _cols(0, hd)] = _dot(scores.astype(BF16),
                                          proj_ref[:, _cols(2, hd)].astype(BF16))
            qh_ref[:, _cols(0, hd)] = q * jnp.exp((tl + 1.0) * lg)
            kh_ref[:, _cols(0, hd)] = k * (jnp.exp((seg - 1.0 - tl) * lg) * QK_SCALE)
        for hd in range(HEADS):
            q = _silu(proj_ref[:, _cols(4, hd)]) * QK_SCALE
            lbh = lb[:, hd * DH:(hd + 1) * DH]
            f = lbh + (1.0 - lbh) * _sigmoid(proj_ref[:, _cols(5, hd)])
            k = 1.0 - f
            o, b = _hgrn_intra(q, k, f, proj_ref[:, _cols(6, hd)], seg, t, masks)
            blast = _block_row_bcast(b, seg, seg - 1)
            o_ref[:, _cols(1, hd)] = o
            qh_ref[:, _cols(1, hd)] = q * jnp.exp2(b)
            kh_ref[:, _cols(1, hd)] = k * jnp.exp2(blast - b)
            dec_ref[:, hd * DH:(hd + 1) * DH] = jnp.exp2(blast)

    def seq_body(s, carry):
        r = pl.ds(pl.multiple_of((j * nsub + s) * seg, seg), seg)
        qh = qh_ref[r, :].astype(BF16)
        kh = kh_ref[r, :].astype(BF16)
        v_ret = proj_ref[r, 2 * GROUP_W:3 * GROUP_W].astype(BF16)
        v_hg = proj_ref[r, 6 * GROUP_W:7 * GROUP_W].astype(BF16)
        dec = dec_ref[r, :][0:1, :]
        o_parts, new_ret, new_hg = [], [], []
        for hd in range(HEADS):
            st = sret_in_ref[s, hd]
            c = slice(hd * DH, (hd + 1) * DH)
            o_parts.append(_dot(qh[:, c], st.astype(BF16)))
            new_ret.append(st * math.exp(seg * _log_gamma(hd)) + _dot_tn(kh[:, c], v_ret[:, c]))
        for hd in range(HEADS):
            st = shg_in_ref[s, hd]
            c = slice(hd * DH, (hd + 1) * DH)
            c1 = slice(GROUP_W + hd * DH, GROUP_W + (hd + 1) * DH)
            o_parts.append(_dot(qh[:, c1], st.astype(BF16)))
            dcol = jnp.broadcast_to(dec[:, c], (DH, DH)).T
            new_hg.append(st * dcol + _dot_tn(kh[:, c1], v_hg[:, c]))
        o_ref[r, :] = o_ref[r, :] + jnp.concatenate(o_parts, axis=1)
        for hd in range(HEADS):
            sret_ref[s, hd] = new_ret[hd]
            shg_ref[s, hd] = new_hg[hd]
        return carry

    lax.fori_loop(0, nsub, seq_body, 0, unroll=2)

    @pl.when(j == pl.num_programs(1) - 1)
    def _():
        for hd in range(HEADS):
            c0 = _cols(0, hd)
            ob_ref[:, c0] = _head_out(o_ref[:, c0], rnw_ref[...],
                                      proj_ref[:, _cols(3, hd)]).astype(BF16)
            c1 = _cols(1, hd)
            ob_ref[:, c1] = _head_out(o_ref[:, c1], hnw_ref[...],
                                      proj_ref[:, _cols(7, hd)]).astype(BF16)
        x1_ref[...] = x_ref[...] + _dot(ob_ref[...], wout_ref[...])


def _mix_sample(x, nw, win, cosf, sinf, rnw, hnw, lbl, wout, sret, shg):
    nb, seg, D = x.shape
    rows = SAMPLE_ROWS
    nsub = SAMPLE_SUB_SEQS
    nj = rows // seg // nsub
    x2 = x.reshape(nb * seg, D)
    state = jax.ShapeDtypeStruct((nb, HEADS, DH, DH), F32)
    state_map = lambda i, j: (i * nj + j, 0, 0, 0)
    state_in_spec = pl.BlockSpec((nsub, HEADS, DH, DH), state_map,
                                 pipeline_mode=pl.Buffered(SAMPLE_STATE_BUFFERS))
    state_out_spec = pl.BlockSpec((nsub, HEADS, DH, DH), state_map)
    x1, sret_new, shg_new = pl.pallas_call(
        functools.partial(_mix_sample_kernel, seg),
        grid=(nb * seg // rows, nj),
        in_specs=[
            pl.BlockSpec((rows, D), lambda i, j: (i, 0)),
            _const_spec((1, D)),
            _const_spec((D, IN_COLS)),
            _const_spec((rows, DH)),
            _const_spec((rows, DH)),
            _const_spec((1, DH)),
            _const_spec((1, DH)),
            _const_spec(lbl.shape),
            _const_spec((D, D)),
            state_in_spec,
            state_in_spec,
        ],
        out_specs=[pl.BlockSpec((rows, D), lambda i, j: (i, 0)), state_out_spec, state_out_spec],
        out_shape=[jax.ShapeDtypeStruct((nb * seg, D), F32), state, state],
        scratch_shapes=[
            pltpu.VMEM((rows, IN_COLS), F32),
            pltpu.VMEM((rows, D), F32),
            pltpu.VMEM((rows, D), F32),
            pltpu.VMEM((rows, D), F32),
            pltpu.VMEM((rows, GROUP_W), F32),
            pltpu.VMEM((rows, D), BF16),
        ],
        compiler_params=pltpu.CompilerParams(
            dimension_semantics=("arbitrary", "arbitrary"), vmem_limit_bytes=VMEM_LIMIT),
        name="mix_sample",
    )(x2, nw, win, cosf, sinf, rnw, hnw, lbl, wout, sret, shg)
    return x1, sret_new, shg_new


def _ffn_kernel(x_ref, nw_ref, wup_ref, wdn_ref, fw_ref, y_ref):
    subs = [slice(r * FFN_SUB_ROWS, (r + 1) * FFN_SUB_ROWS)
            for r in range(x_ref.shape[0] // FFN_SUB_ROWS)]
    hs = [_rms(x_ref[rs, :], nw_ref[...]).astype(BF16) for rs in subs]
    accs = [x_ref[rs, :] for rs in subs]
    step = D_MODEL
    for g in range(D_FF // step):
        us = [jnp.maximum(_dot(h, wup_ref[:, g * step:(g + 1) * step]), 0.0) for h in hs]
        accs = [acc + _dot((u * u).astype(BF16), wdn_ref[g * step:(g + 1) * step, :])
                for acc, u in zip(accs, us)]
    for rs, acc in zip(subs, accs):
        y_ref[rs, :] = _rms(acc, fw_ref[...])


def _ffn_cast_kernel(x_ref, nw_ref, wup_hbm, wdn_hbm, fw_ref, y_ref, wup_ref, wdn_ref,
                     stage_up_ref, stage_dn_ref, sem):
    @pl.when(pl.program_id(0) == 0)
    def _():
        _cast_weight(wup_hbm, wup_ref, stage_up_ref, sem, stage_up_ref.shape[1])
        _cast_weight(wdn_hbm, wdn_ref, stage_dn_ref, sem, stage_dn_ref.shape[1])

    _ffn_kernel(x_ref, nw_ref, wup_ref, wdn_ref, fw_ref, y_ref)


def _ffn_cast(x, nw, wup_f32, wdn_f32, fw):
    n, D = x.shape
    rows = min(FFN_ROWS, n)
    resident = lambda shape: pl.BlockSpec(shape, lambda i: (0, 0), pipeline_mode=pl.Buffered(1))
    return pl.pallas_call(
        _ffn_cast_kernel,
        grid=(n // rows,),
        in_specs=[
            pl.BlockSpec((rows, D), lambda i: (i, 0)),
            _const_spec((1, D)),
            pl.BlockSpec(memory_space=pl.ANY),
            pl.BlockSpec(memory_space=pl.ANY),
            _const_spec((1, D)),
        ],
        out_specs=[pl.BlockSpec((rows, D), lambda i: (i, 0)),
                   resident((D, D_FF)), resident((D_FF, D))],
        out_shape=[jax.ShapeDtypeStruct((n, D), F32),
                   jax.ShapeDtypeStruct((D, D_FF), BF16),
                   jax.ShapeDtypeStruct((D_FF, D), BF16)],
        scratch_shapes=[
            pltpu.VMEM((2, CAST_CHUNK_BYTES // (4 * D_FF), D_FF), F32),
            pltpu.VMEM((2, CAST_CHUNK_BYTES // (4 * D), D), F32),
            pltpu.SemaphoreType.DMA((2,)),
        ],
        compiler_params=pltpu.CompilerParams(
            dimension_semantics=("arbitrary",), vmem_limit_bytes=VMEM_LIMIT),
        name="ffn_cast",
    )(x, nw, wup_f32, wdn_f32, fw)


def _ffn(x, nw, wup, wdn, fw):
    n, D = x.shape
    rows = min(FFN_ROWS, n)
    return pl.pallas_call(
        _ffn_kernel,
        grid=(n // rows,),
        in_specs=[
            pl.BlockSpec((rows, D), lambda i: (i, 0)),
            _const_spec((1, D)),
            _const_spec((D, D_FF)),
            _const_spec((D_FF, D)),
            _const_spec((1, D)),
        ],
        out_specs=pl.BlockSpec((rows, D), lambda i: (i, 0)),
        out_shape=jax.ShapeDtypeStruct((n, D), F32),
        compiler_params=pltpu.CompilerParams(
            dimension_semantics=("arbitrary",), vmem_limit_bytes=VMEM_LIMIT),
        name="ffn",
    )(x, nw, wup, wdn, fw)


def _rope_tables(pos):
    half = DH // 2
    inv_freq = ROPE_BASE ** (-jnp.arange(half, dtype=F32) / half)
    ang = pos.astype(F32)[:, None] * inv_freq[None, :]
    cos, sin = jnp.cos(ang), jnp.sin(ang)
    return jnp.concatenate([cos, cos], axis=-1), jnp.concatenate([-sin, sin], axis=-1)


def kernel(x_prompt, x_sample, state_ret, state_hgrn, norm_mix_w, w_in, ret_norm_w, hgrn_norm_w,
           lb_logits, w_out, norm_ffn_w, w_up, w_down, final_norm_w):
    B, T, D = x_prompt.shape
    nb, seg, _ = x_sample.shape
    assert w_in.shape[0] == 1, "one layer"

    nw = norm_mix_w[0].reshape(1, D)
    rnw = ret_norm_w[0].reshape(1, DH)
    hnw = hgrn_norm_w[0].reshape(1, DH)
    lbl = lb_logits.astype(F32)
    fnw = norm_ffn_w[0].reshape(1, D)
    fw = final_norm_w.reshape(1, D)

    cos_p, sin_p = _rope_tables(jnp.arange(T, dtype=jnp.int32))
    cos_s, sin_s = _rope_tables(PAST_LEN + jnp.arange(seg, dtype=jnp.int32))
    cos_s = jnp.tile(cos_s, (SAMPLE_ROWS // seg, 1))
    sin_s = jnp.tile(sin_s, (SAMPLE_ROWS // seg, 1))

    xp1, sret_p, shg_p, win, wout = _mix_prompt(x_prompt, nw, w_in[0], cos_p, sin_p, rnw, hnw, lbl,
                                                w_out[0])
    xs1, sret_s, shg_s = _mix_sample(x_sample, nw, win, cos_s, sin_s, rnw, hnw, lbl, wout,
                                     state_ret[0], state_hgrn[0])

    y_p, wup, wdn = _ffn_cast(xp1.reshape(B * T, D), fnw, w_up[0], w_down[0], fw)
    y_p = y_p.reshape(B, T, D)
    y_s = _ffn(xs1, fnw, wup, wdn, fw).reshape(nb, seg, D)
    return (y_p, y_s, sret_p[None], shg_p[None], sret_s[None], shg_s[None])
```

```python
import functools
import math

import jax
import jax.numpy as jnp
from jax import lax
from jax.experimental import pallas as pl
from jax.experimental.pallas import tpu as pltpu

D_MODEL = 1024
HEADS = 4
DH = 128
GROUP_W = HEADS * DH
IN_COLS = 8 * GROUP_W
D_FF = 4 * D_MODEL
ROPE_BASE = 10000.0
NORM_EPS = 1e-6
QK_SCALE = DH ** -0.5
PAST_LEN = 16384

PROMPT_CHUNK = 256
PROMPT_STEP_CHUNKS = 2
SAMPLE_ROWS = 256
SAMPLE_SUB_SEQS = 8
SAMPLE_STATE_BUFFERS = 2
FFN_ROWS = 1024
FFN_SUB_ROWS = 512
PROJ_PIECE = 512
CAST_BUFFERS = 4
CAST_ROWS = 512
VMEM_LIMIT = 56 * 1024 * 1024

F32 = jnp.float32
BF16 = jnp.bfloat16


def _dot(a, b):
    return jnp.dot(a, b, preferred_element_type=F32)


def _dot_nt(a, b):
    return lax.dot_general(a, b, (((1,), (1,)), ((), ())), preferred_element_type=F32)


def _dot_tn(a, b):
    return lax.dot_general(a, b, (((0,), (0,)), ((), ())), preferred_element_type=F32)


def _rms(x, w):
    ms = jnp.mean(x * x, axis=-1, keepdims=True)
    return x * lax.rsqrt(ms + NORM_EPS) * w


def _sigmoid(x):
    return 1.0 / (1.0 + jnp.exp(-x))


def _silu(x):
    return x * _sigmoid(x)


def _lower_bound(lbl):
    mx = jnp.max(lbl, axis=0, keepdims=True)
    e = jnp.exp(lbl - mx)
    return e[0:1, :] / jnp.sum(e, axis=0, keepdims=True)


def _rotary(x, cosf, sinf):
    return x * cosf + pltpu.roll(x, DH // 2, 1) * sinf


def _split_halves(x, m):
    blk = 2 * m
    nb = x.shape[0] // blk
    lo = [x[b * blk:b * blk + m] for b in range(nb)]
    up = [x[b * blk + m:(b + 1) * blk] for b in range(nb)]
    cat = lambda ps: ps[0] if len(ps) == 1 else jnp.concatenate(ps, axis=0)
    return cat(lo), cat(up)


def _merge_halves(lo, up, m):
    nb = lo.shape[0] // m
    pieces = []
    for b in range(nb):
        pieces += [lo[b * m:(b + 1) * m], up[b * m:(b + 1) * m]]
    return jnp.concatenate(pieces, axis=0)


def _block_row_bcast(x, m, row):
    nb = x.shape[0] // m
    pieces = [jnp.broadcast_to(x[b * m + row:b * m + row + 1], (m, x.shape[1])) for b in range(nb)]
    return pieces[0] if nb == 1 else jnp.concatenate(pieces, axis=0)


def _hgrn_level_masks(rows, seg):
    masks = []
    for j in range(int(math.log2(seg))):
        if (1 << j) < 8:
            r = lax.broadcasted_iota(jnp.int32, (rows, rows), 0)
            s = lax.broadcasted_iota(jnp.int32, (rows, rows), 1)
            masks.append(jnp.logical_and(((r ^ s) >> j) == 1, r > s))
        else:
            r = lax.broadcasted_iota(jnp.int32, (rows // 2, rows), 0)
            s = lax.broadcasted_iota(jnp.int32, (rows // 2, rows), 1)
            masks.append((s >> j) == 2 * (r >> j))
    return masks


def _hgrn_intra(q, k, f, v, seg, t, masks, mid_hook=None):
    rows = q.shape[0]
    c = jnp.log2(f)
    scores = None
    for j in range(int(math.log2(seg))):
        m = 1 << j
        if m < 8:
            c3 = c.reshape(rows // 8, 8, DH)
            upper = (t & m) != 0
            if m == 1:
                w = jnp.where(upper, f, 1.0)
                tot = pltpu.roll(c3, 1, 1).reshape(rows, DH)
            else:
                if m == 2:
                    low4 = ((t & 7) < 4).reshape(rows // 8, 8, DH)
                    tot3 = jnp.where(low4, jnp.broadcast_to(c3[:, 1:2, :], c3.shape),
                                     jnp.broadcast_to(c3[:, 5:6, :], c3.shape))
                else:
                    tot3 = jnp.broadcast_to(c3[:, 3:4, :], c3.shape)
                tot = tot3.reshape(rows, DH)
                w = jnp.exp2(jnp.where(upper, c, tot - c))
            part = _dot_nt((q * w).astype(BF16), (k * w).astype(BF16))
            scores = jnp.where(masks[j], part, 0.0 if scores is None else scores)
            c = jnp.where(upper, c + tot, c)
        else:
            c_lo, c_up = _split_halves(c, m)
            tot = _block_row_bcast(c_lo, m, m - 1)
            q_up = _split_halves(q, m)[1]
            k_lo, k_up = _split_halves(k, m)
            qt = (q_up * jnp.exp2(c_up)).astype(BF16)
            kt = _merge_halves(k_lo * jnp.exp2(tot - c_lo), k_up, m).astype(BF16)
            s_lo, s_up = _split_halves(scores, m)
            s_up = jnp.where(masks[j], _dot_nt(qt, kt), s_up)
            scores = _merge_halves(s_lo, s_up, m)
            c = _merge_halves(c_lo, c_up + tot, m)
    if mid_hook is not None:
        mid_hook()
    o = _dot(scores.astype(BF16), v.astype(BF16))
    o = o + jnp.sum(q * k, axis=-1, keepdims=True) * v
    return o, c


def _ret_mask(rows, seg, log_gamma):
    r = lax.broadcasted_iota(jnp.int32, (rows, rows), 0)
    s = lax.broadcasted_iota(jnp.int32, (rows, rows), 1)
    valid = jnp.logical_and(((r ^ s) >> int(math.log2(seg))) == 0, r >= s)
    d = jnp.where(valid, r - s, 0).astype(F32)
    return jnp.where(valid, jnp.exp(d * log_gamma) * QK_SCALE, 0.0)


def _log_gamma(hd):
    return math.log(1.0 - 2.0 ** (-5.0 - hd))


def _head_out(o, w, gate):
    return o * lax.rsqrt(jnp.mean(o * o, axis=-1, keepdims=True) + NORM_EPS) * w * _silu(gate)


def _project(x_ref, nw_ref, win_ref, proj_ref):
    h = _rms(x_ref[...], nw_ref[...]).astype(BF16)
    for g in range(8):
        cols = slice(g * GROUP_W, (g + 1) * GROUP_W)
        proj_ref[:, cols] = _dot(h, win_ref[:, cols])


def _cast_weight(w_hbm, w_bf_ref, stage_ref, sem):
    nbuf, brows, bcols = stage_ref.shape
    rows, width = w_hbm.shape
    k = width // bcols
    crows = brows // k
    n = rows // crows

    def copies(i):
        return [pltpu.make_async_copy(
            w_hbm.at[pl.ds(i * crows, crows), pl.ds(c * bcols, bcols)],
            stage_ref.at[i % nbuf, pl.ds(c * crows, crows), :],
            sem.at[i % nbuf, c]) for c in range(k)]

    for i in range(min(nbuf - 1, n)):
        for cp in copies(i):
            cp.start()
    for i in range(n):
        if i + nbuf - 1 < n:
            for cp in copies(i + nbuf - 1):
                cp.start()
        for c, cp in enumerate(copies(i)):
            cp.wait()
            w_bf_ref[pl.ds(i * crows, crows), pl.ds(c * bcols, bcols)] = (
                stage_ref[i % nbuf, c * crows:(c + 1) * crows, :].astype(BF16))


def _cols(group, hd):
    return slice(group * GROUP_W + hd * DH, group * GROUP_W + (hd + 1) * DH)


def _mix_prompt_kernel(nc, *refs):
    CH = PROMPT_STEP_CHUNKS
    L = PROMPT_CHUNK
    x0_ref = refs[0]
    xn_refs = refs[1:1 + CH]
    (xc_ref, nw_ref, win_hbm, cos_ref, sin_ref, rnw_ref, hnw_ref, lbl_ref, wout_hbm,
     x1_ref, sret_ref, shg_ref, win_ref, wout_ref,
     proj_ref, o_ref, srt_ref, sht_ref, dm_ref, qd_ref, kd_ref, stage_ref, sem) = refs[1 + CH:]
    g = pl.program_id(0)
    ci0 = (g * CH) % nc
    t = lax.broadcasted_iota(jnp.int32, (L, DH), 0)

    @pl.when(g == 0)
    def _():
        _cast_weight(win_hbm, win_ref, stage_ref, sem)
        _cast_weight(wout_hbm, wout_ref, stage_ref, sem)
        tf = t.astype(F32)
        for hd in range(HEADS):
            lg = _log_gamma(hd)
            dm_ref[hd] = _ret_mask(L, L, lg)
            qd_ref[hd] = jnp.exp((tf + 1.0) * lg)
            kd_ref[hd] = jnp.exp((L - 1.0 - tf) * lg) * QK_SCALE
        _project(x0_ref, nw_ref, win_ref, proj_ref.at[0])

    @pl.when(ci0 == 0)
    def _():
        srt_ref[...] = jnp.zeros_like(srt_ref)
        sht_ref[...] = jnp.zeros_like(sht_ref)

    lb = _lower_bound(lbl_ref[...])
    masks = _hgrn_level_masks(L, L)

    def chunk(pr, pw, xn_ref, rows):
        h_next = _rms(xn_ref[...], nw_ref[...]).astype(BF16)

        def project_piece(pi):
            cols = slice(pi * PROJ_PIECE, (pi + 1) * PROJ_PIECE)
            pw[:, cols] = _dot(h_next, win_ref[:, cols])

        cosf = cos_ref[rows, :]
        sinf = sin_ref[rows, :]

        for hd in range(HEADS):
            q = _rotary(pr[:, _cols(0, hd)], cosf, sinf)
            k = _rotary(pr[:, _cols(1, hd)], cosf, sinf)
            v = pr[:, _cols(2, hd)].astype(BF16)
            scores = _dot_nt(q.astype(BF16), k.astype(BF16)) * dm_ref[hd]
            project_piece(hd)
            o = _dot(scores.astype(BF16), v)
            st = srt_ref[hd]
            o = o + _dot_nt((q * qd_ref[hd]).astype(BF16), st.astype(BF16))
            kh = (k * kd_ref[hd]).astype(BF16)
            srt_ref[hd] = st * math.exp(L * _log_gamma(hd)) + _dot_tn(v, kh)
            o_ref[:, _cols(0, hd)] = _head_out(o, rnw_ref[...], pr[:, _cols(3, hd)]).astype(BF16)

        for hd in range(HEADS):
            q = _silu(pr[:, _cols(4, hd)]) * QK_SCALE
            lbh = lb[:, hd * DH:(hd + 1) * DH]
            f = lbh + (1.0 - lbh) * _sigmoid(pr[:, _cols(5, hd)])
            k = 1.0 - f
            v = pr[:, _cols(6, hd)]
            o, b = _hgrn_intra(q, k, f, v, L, t, masks,
                               functools.partial(project_piece, HEADS + hd))
            blast = b[L - 1:L, :]
            st = sht_ref[hd]
            o = o + _dot_nt((q * jnp.exp2(b)).astype(BF16), st.astype(BF16))
            kh = (k * jnp.exp2(blast - b)).astype(BF16)
            sht_ref[hd] = st * jnp.exp2(blast) + _dot_tn(v.astype(BF16), kh)
            o_ref[:, _cols(1, hd)] = _head_out(o, hnw_ref[...], pr[:, _cols(7, hd)]).astype(BF16)

        x1_ref[rows, :] = xc_ref[rows, :] + _dot(o_ref[...], wout_ref[...])

    for sub in range(CH):
        chunk(proj_ref.at[sub % 2], proj_ref.at[(sub + 1) % 2], xn_refs[sub],
              slice(sub * L, (sub + 1) * L))

    @pl.when(ci0 == nc - CH)
    def _():
        for hd in range(HEADS):
            sret_ref[hd] = srt_ref[hd].T
            shg_ref[hd] = sht_ref[hd].T


def _const_spec(shape):
    return pl.BlockSpec(shape, lambda *_: (0,) * len(shape), pipeline_mode=pl.Buffered(1))


def _mix_prompt(x, nw, win_f32, cosf, sinf, rnw, hnw, lbl, wout_f32):
    B, T, D = x.shape
    L = PROMPT_CHUNK
    CH = PROMPT_STEP_CHUNKS
    nc = T // L
    n = B * nc
    spb = nc // CH
    assert nc % CH == 0 and CH % 2 == 0

    def next_chunk_spec(i):
        idx = lambda g: jnp.minimum(CH * g + i + 1, n - 1)
        return pl.BlockSpec((None, L, D), lambda g: (idx(g) // nc, idx(g) % nc, 0))

    state = jax.ShapeDtypeStruct((B, HEADS, DH, DH), F32)
    state_spec = pl.BlockSpec((None, HEADS, DH, DH), lambda g: (g // spb, 0, 0, 0))
    rows_spec = pl.BlockSpec((None, CH * L, D), lambda g: (g // spb, g % spb, 0))
    table_spec = pl.BlockSpec((CH * L, DH), lambda g: (g % spb, 0))
    return pl.pallas_call(
        functools.partial(_mix_prompt_kernel, nc),
        grid=(n // CH,),
        in_specs=[
            pl.BlockSpec((None, L, D), lambda g: (0, 0, 0), pipeline_mode=pl.Buffered(1)),
            *[next_chunk_spec(i) for i in range(CH)],
            rows_spec,
            _const_spec((1, D)),
            pl.BlockSpec(memory_space=pl.ANY),
            table_spec,
            table_spec,
            _const_spec((1, DH)),
            _const_spec((1, DH)),
            _const_spec(lbl.shape),
            pl.BlockSpec(memory_space=pl.ANY),
        ],
        out_specs=[rows_spec, state_spec, state_spec,
                   pl.BlockSpec((D, IN_COLS), lambda g: (0, 0), pipeline_mode=pl.Buffered(1)),
                   pl.BlockSpec((D, D), lambda g: (0, 0), pipeline_mode=pl.Buffered(1))],
        out_shape=[jax.ShapeDtypeStruct((B, T, D), F32), state, state,
                   jax.ShapeDtypeStruct((D, IN_COLS), BF16), jax.ShapeDtypeStruct((D, D), BF16)],
        scratch_shapes=[
            pltpu.VMEM((2, L, IN_COLS), F32),
            pltpu.VMEM((L, D), BF16),
            pltpu.VMEM((HEADS, DH, DH), F32),
            pltpu.VMEM((HEADS, DH, DH), F32),
            pltpu.VMEM((HEADS, L, L), F32),
            pltpu.VMEM((HEADS, L, DH), F32),
            pltpu.VMEM((HEADS, L, DH), F32),
            pltpu.VMEM((CAST_BUFFERS, CAST_ROWS, D), F32),
            pltpu.SemaphoreType.DMA((CAST_BUFFERS, IN_COLS // D)),
        ],
        compiler_params=pltpu.CompilerParams(
            dimension_semantics=("arbitrary",), vmem_limit_bytes=VMEM_LIMIT),
        name="mix_prompt",
    )(x, *([x] * CH), x, nw, win_f32, cosf, sinf, rnw, hnw, lbl, wout_f32)


def _mix_sample_kernel(seg, x_ref, nw_ref, win_ref, cos_ref, sin_ref, rnw_ref, hnw_ref, lbl_ref,
                       wout_ref, sret_in_ref, shg_in_ref,
                       x1_ref, sret_ref, shg_ref,
                       proj_ref, o_ref, qh_ref, kh_ref, dec_ref, ob_ref):
    rows = x_ref.shape[0]
    nsub = sret_in_ref.shape[0]
    j = pl.program_id(1)

    @pl.when(j == 0)
    def _():
        _project(x_ref, nw_ref, win_ref, proj_ref)
        t = lax.broadcasted_iota(jnp.int32, (rows, DH), 0)
        tl = (t & (seg - 1)).astype(F32)
        cosf = cos_ref[...]
        sinf = sin_ref[...]
        lb = _lower_bound(lbl_ref[...])
        masks = _hgrn_level_masks(rows, seg)
        for hd in range(HEADS):
            lg = _log_gamma(hd)
            q = _rotary(proj_ref[:, _cols(0, hd)], cosf, sinf)
            k = _rotary(proj_ref[:, _cols(1, hd)], cosf, sinf)
            scores = _dot_nt(q.astype(BF16), k.astype(BF16)) * _ret_mask(rows, seg, lg)
            o_ref[:, _cols(0, hd)] = _dot(scores.astype(BF16),
                                          proj_ref[:, _cols(2, hd)].astype(BF16))
            qh_ref[:, _cols(0, hd)] = q * jnp.exp((tl + 1.0) * lg)
            kh_ref[:, _cols(0, hd)] = k * (jnp.exp((seg - 1.0 - tl) * lg) * QK_SCALE)
        for hd in range(HEADS):
            q = _silu(proj_ref[:, _cols(4, hd)]) * QK_SCALE
            lbh = lb[:, hd * DH:(hd + 1) * DH]
            f = lbh + (1.0 - lbh) * _sigmoid(proj_ref[:, _cols(5, hd)])
            k = 1.0 - f
            o, b = _hgrn_intra(q, k, f, proj_ref[:, _cols(6, hd)], seg, t, masks)
            blast = _block_row_bcast(b, seg, seg - 1)
            o_ref[:, _cols(1, hd)] = o
            qh_ref[:, _cols(1, hd)] = q * jnp.exp2(b)
            kh_ref[:, _cols(1, hd)] = k * jnp.exp2(blast - b)
            dec_ref[:, hd * DH:(hd + 1) * DH] = jnp.exp2(blast)

    def seq_body(s, carry):
        r = pl.ds(pl.multiple_of((j * nsub + s) * seg, seg), seg)
        qh = qh_ref[r, :].astype(BF16)
        kh = kh_ref[r, :].astype(BF16)
        v_ret = proj_ref[r, 2 * GROUP_W:3 * GROUP_W].astype(BF16)
        v_hg = proj_ref[r, 6 * GROUP_W:7 * GROUP_W].astype(BF16)
        dec = dec_ref[r, :][0:1, :]
        o_parts, new_ret, new_hg = [], [], []
        for hd in range(HEADS):
            st = sret_in_ref[s, hd]
            c = slice(hd * DH, (hd + 1) * DH)
            o_parts.append(_dot(qh[:, c], st.astype(BF16)))
            new_ret.append(st * math.exp(seg * _log_gamma(hd)) + _dot_tn(kh[:, c], v_ret[:, c]))
        for hd in range(HEADS):
            st = shg_in_ref[s, hd]
            c = slice(hd * DH, (hd + 1) * DH)
            c1 = slice(GROUP_W + hd * DH, GROUP_W + (hd + 1) * DH)
            o_parts.append(_dot(qh[:, c1], st.astype(BF16)))
            dcol = jnp.broadcast_to(dec[:, c], (DH, DH)).T
            new_hg.append(st * dcol + _dot_tn(kh[:, c1], v_hg[:, c]))
        o_ref[r, :] = o_ref[r, :] + jnp.concatenate(o_parts, axis=1)
        for hd in range(HEADS):
            sret_ref[s, hd] = new_ret[hd]
            shg_ref[s, hd] = new_hg[hd]
        return carry

    lax.fori_loop(0, nsub, seq_body, 0, unroll=2)

    @pl.when(j == pl.num_programs(1) - 1)
    def _():
        for hd in range(HEADS):
            c0 = _cols(0, hd)
            ob_ref[:, c0] = _head_out(o_ref[:, c0], rnw_ref[...],
                                      proj_ref[:, _cols(3, hd)]).astype(BF16)
            c1 = _cols(1, hd)
            ob_ref[:, c1] = _head_out(o_ref[:, c1], hnw_ref[...],
                                      proj_ref[:, _cols(7, hd)]).astype(BF16)
        x1_ref[...] = x_ref[...] + _dot(ob_ref[...], wout_ref[...])


def _mix_sample(x, nw, win, cosf, sinf, rnw, hnw, lbl, wout, sret, shg):
    nb, seg, D = x.shape
    rows = SAMPLE_ROWS
    nsub = SAMPLE_SUB_SEQS
    nj = rows // seg // nsub
    x2 = x.reshape(nb * seg, D)
    state = jax.ShapeDtypeStruct((nb, HEADS, DH, DH), F32)
    state_map = lambda i, j: (i * nj + j, 0, 0, 0)
    state_in_spec = pl.BlockSpec((nsub, HEADS, DH, DH), state_map,
                                 pipeline_mode=pl.Buffered(SAMPLE_STATE_BUFFERS))
    state_out_spec = pl.BlockSpec((nsub, HEADS, DH, DH), state_map)
    x1, sret_new, shg_new = pl.pallas_call(
        functools.partial(_mix_sample_kernel, seg),
        grid=(nb * seg // rows, nj),
        in_specs=[
            pl.BlockSpec((rows, D), lambda i, j: (i, 0)),
            _const_spec((1, D)),
            _const_spec((D, IN_COLS)),
            _const_spec((rows, DH)),
            _const_spec((rows, DH)),
            _const_spec((1, DH)),
            _const_spec((1, DH)),
            _const_spec(lbl.shape),
            _const_spec((D, D)),
            state_in_spec,
            state_in_spec,
        ],
        out_specs=[pl.BlockSpec((rows, D), lambda i, j: (i, 0)), state_out_spec, state_out_spec],
        out_shape=[jax.ShapeDtypeStruct((nb * seg, D), F32), state, state],
        scratch_shapes=[
            pltpu.VMEM((rows, IN_COLS), F32),
            pltpu.VMEM((rows, D), F32),
            pltpu.VMEM((rows, D), F32),
            pltpu.VMEM((rows, D), F32),
            pltpu.VMEM((rows, GROUP_W), F32),
            pltpu.VMEM((rows, D), BF16),
        ],
        compiler_params=pltpu.CompilerParams(
            dimension_semantics=("arbitrary", "arbitrary"), vmem_limit_bytes=VMEM_LIMIT),
        name="mix_sample",
    )(x2, nw, win, cosf, sinf, rnw, hnw, lbl, wout, sret, shg)
    return x1, sret_new, shg_new


def _ffn_kernel(x_ref, nw_ref, wup_ref, wdn_ref, fw_ref, y_ref):
    subs = [slice(r * FFN_SUB_ROWS, (r + 1) * FFN_SUB_ROWS)
            for r in range(x_ref.shape[0] // FFN_SUB_ROWS)]
    hs = [_rms(x_ref[rs, :], nw_ref[...]).astype(BF16) for rs in subs]
    accs = [x_ref[rs, :] for rs in subs]
    step = D_MODEL
    for g in range(D_FF // step):
        us = [jnp.maximum(_dot(h, wup_ref[:, g * step:(g + 1) * step]), 0.0) for h in hs]
        accs = [acc + _dot((u * u).astype(BF16), wdn_ref[g * step:(g + 1) * step, :])
                for acc, u in zip(accs, us)]
    for rs, acc in zip(subs, accs):
        y_ref[rs, :] = _rms(acc, fw_ref[...])


def _ffn_cast_kernel(x_ref, nw_ref, wup_hbm, wdn_hbm, fw_ref, y_ref, wup_ref, wdn_ref,
                     stage_ref, sem):
    @pl.when(pl.program_id(0) == 0)
    def _():
        _cast_weight(wup_hbm, wup_ref, stage_ref, sem)
        _cast_weight(wdn_hbm, wdn_ref, stage_ref, sem)

    _ffn_kernel(x_ref, nw_ref, wup_ref, wdn_ref, fw_ref, y_ref)


def _ffn_cast(x, nw, wup_f32, wdn_f32, fw):
    n, D = x.shape
    rows = min(FFN_ROWS, n)
    resident = lambda shape: pl.BlockSpec(shape, lambda i: (0, 0), pipeline_mode=pl.Buffered(1))
    return pl.pallas_call(
        _ffn_cast_kernel,
        grid=(n // rows,),
        in_specs=[
            pl.BlockSpec((rows, D), lambda i: (i, 0)),
            _const_spec((1, D)),
            pl.BlockSpec(memory_space=pl.ANY),
            pl.BlockSpec(memory_space=pl.ANY),
            _const_spec((1, D)),
        ],
        out_specs=[pl.BlockSpec((rows, D), lambda i: (i, 0)),
                   resident((D, D_FF)), resident((D_FF, D))],
        out_shape=[jax.ShapeDtypeStruct((n, D), F32),
                   jax.ShapeDtypeStruct((D, D_FF), BF16),
                   jax.ShapeDtypeStruct((D_FF, D), BF16)],
        scratch_shapes=[
            pltpu.VMEM((CAST_BUFFERS, CAST_ROWS, D), F32),
            pltpu.SemaphoreType.DMA((CAST_BUFFERS, D_FF // D)),
        ],
        compiler_params=pltpu.CompilerParams(
            dimension_semantics=("arbitrary",), vmem_limit_bytes=VMEM_LIMIT),
        name="ffn_cast",
    )(x, nw, wup_f32, wdn_f32, fw)


def _ffn(x, nw, wup, wdn, fw):
    n, D = x.shape
    rows = min(FFN_ROWS, n)
    return pl.pallas_call(
        _ffn_kernel,
        grid=(n // rows,),
        in_specs=[
            pl.BlockSpec((rows, D), lambda i: (i, 0)),
            _const_spec((1, D)),
            _const_spec((D, D_FF)),
            _const_spec((D_FF, D)),
            _const_spec((1, D)),
        ],
        out_specs=pl.BlockSpec((rows, D), lambda i: (i, 0)),
        out_shape=jax.ShapeDtypeStruct((n, D), F32),
        compiler_params=pltpu.CompilerParams(
            dimension_semantics=("arbitrary",), vmem_limit_bytes=VMEM_LIMIT),
        name="ffn",
    )(x, nw, wup, wdn, fw)


def _rope_tables(pos):
    half = DH // 2
    inv_freq = ROPE_BASE ** (-jnp.arange(half, dtype=F32) / half)
    ang = pos.astype(F32)[:, None] * inv_freq[None, :]
    cos, sin = jnp.cos(ang), jnp.sin(ang)
    return jnp.concatenate([cos, cos], axis=-1), jnp.concatenate([-sin, sin], axis=-1)


def kernel(x_prompt, x_sample, state_ret, state_hgrn, norm_mix_w, w_in, ret_norm_w, hgrn_norm_w,
           lb_logits, w_out, norm_ffn_w, w_up, w_down, final_norm_w):
    B, T, D = x_prompt.shape
    nb, seg, _ = x_sample.shape
    assert w_in.shape[0] == 1, "one layer"

    nw = norm_mix_w[0].reshape(1, D)
    rnw = ret_norm_w[0].reshape(1, DH)
    hnw = hgrn_norm_w[0].reshape(1, DH)
    lbl = lb_logits.astype(F32)
    fnw = norm_ffn_w[0].reshape(1, D)
    fw = final_norm_w.reshape(1, D)

    cos_p, sin_p = _rope_tables(jnp.arange(T, dtype=jnp.int32))
    cos_s, sin_s = _rope_tables(PAST_LEN + jnp.arange(seg, dtype=jnp.int32))
    cos_s = jnp.tile(cos_s, (SAMPLE_ROWS // seg, 1))
    sin_s = jnp.tile(sin_s, (SAMPLE_ROWS // seg, 1))

    xp1, sret_p, shg_p, win, wout = _mix_prompt(x_prompt, nw, w_in[0], cos_p, sin_p, rnw, hnw, lbl,
                                                w_out[0])
    xs1, sret_s, shg_s = _mix_sample(x_sample, nw, win, cos_s, sin_s, rnw, hnw, lbl, wout,
                                     state_ret[0], state_hgrn[0])

    y_p, wup, wdn = _ffn_cast(xp1.reshape(B * T, D), fnw, w_up[0], w_down[0], fw)
    y_p = y_p.reshape(B, T, D)
    y_s = _ffn(xs1, fnw, wup, wdn, fw).reshape(nb, seg, D)
    return (y_p, y_s, sret_p[None], shg_p[None], sret_s[None], shg_s[None])
```

```python
import functools
import math

import jax
import jax.numpy as jnp
from jax import lax
from jax.experimental import pallas as pl
from jax.experimental.pallas import tpu as pltpu

D_MODEL = 1024
HEADS = 4
DH = 128
GROUP_W = HEADS * DH
IN_COLS = 8 * GROUP_W
D_FF = 4 * D_MODEL
ROPE_BASE = 10000.0
NORM_EPS = 1e-6
QK_SCALE = DH ** -0.5
PAST_LEN = 16384

PROMPT_CHUNK = 256
PROMPT_STEP_CHUNKS = 2
SAMPLE_ROWS = 256
SAMPLE_SUB_SEQS = 8
SAMPLE_STATE_BUFFERS = 2
FFN_ROWS = 1024
FFN_SUB_ROWS = 512
PROJ_PIECE = 512
CAST_BUFFERS = 4
CAST_ROWS = 512
VMEM_LIMIT = 56 * 1024 * 1024
FFN_VMEM_LIMIT = 60 * 1024 * 1024

F32 = jnp.float32
BF16 = jnp.bfloat16


def _dot(a, b):
    return jnp.dot(a, b, preferred_element_type=F32)


def _dot_nt(a, b):
    return lax.dot_general(a, b, (((1,), (1,)), ((), ())), preferred_element_type=F32)


def _dot_tn(a, b):
    return lax.dot_general(a, b, (((0,), (0,)), ((), ())), preferred_element_type=F32)


def _rms(x, w):
    ms = jnp.mean(x * x, axis=-1, keepdims=True)
    return x * lax.rsqrt(ms + NORM_EPS) * w


def _sigmoid(x):
    return 0.5 * jnp.tanh(0.5 * x) + 0.5


def _silu(x):
    return x * _sigmoid(x)


def _lower_bound(lbl):
    mx = jnp.max(lbl, axis=0, keepdims=True)
    e = jnp.exp(lbl - mx)
    return e[0:1, :] / jnp.sum(e, axis=0, keepdims=True)


def _rotary(x, cosf, sinf):
    return x * cosf + pltpu.roll(x, DH // 2, 1) * sinf


def _split_halves(x, m):
    blk = 2 * m
    nb = x.shape[0] // blk
    lo = [x[b * blk:b * blk + m] for b in range(nb)]
    up = [x[b * blk + m:(b + 1) * blk] for b in range(nb)]
    cat = lambda ps: ps[0] if len(ps) == 1 else jnp.concatenate(ps, axis=0)
    return cat(lo), cat(up)


def _merge_halves(lo, up, m):
    nb = lo.shape[0] // m
    pieces = []
    for b in range(nb):
        pieces += [lo[b * m:(b + 1) * m], up[b * m:(b + 1) * m]]
    return jnp.concatenate(pieces, axis=0)


def _block_row_bcast(x, m, row):
    nb = x.shape[0] // m
    pieces = [jnp.broadcast_to(x[b * m + row:b * m + row + 1], (m, x.shape[1])) for b in range(nb)]
    return pieces[0] if nb == 1 else jnp.concatenate(pieces, axis=0)


def _hgrn_level_masks(rows, seg):
    masks = []
    for j in range(int(math.log2(seg))):
        if (1 << j) < 8:
            r = lax.broadcasted_iota(jnp.int32, (rows, rows), 0)
            s = lax.broadcasted_iota(jnp.int32, (rows, rows), 1)
            masks.append(jnp.logical_and(((r ^ s) >> j) == 1, r > s))
        else:
            r = lax.broadcasted_iota(jnp.int32, (rows // 2, rows), 0)
            s = lax.broadcasted_iota(jnp.int32, (rows // 2, rows), 1)
            masks.append((s >> j) == 2 * (r >> j))
    return masks


def _hgrn_intra(q, k, f, v, seg, t, masks, mid_hook=None):
    rows = q.shape[0]
    c = jnp.log2(f)
    scores = None
    for j in range(int(math.log2(seg))):
        m = 1 << j
        if m < 8:
            c3 = c.reshape(rows // 8, 8, DH)
            upper = (t & m) != 0
            if m == 1:
                w = jnp.where(upper, f, 1.0)
                tot = pltpu.roll(c3, 1, 1).reshape(rows, DH)
            else:
                if m == 2:
                    low4 = ((t & 7) < 4).reshape(rows // 8, 8, DH)
                    tot3 = jnp.where(low4, jnp.broadcast_to(c3[:, 1:2, :], c3.shape),
                                     jnp.broadcast_to(c3[:, 5:6, :], c3.shape))
                else:
                    tot3 = jnp.broadcast_to(c3[:, 3:4, :], c3.shape)
                tot = tot3.reshape(rows, DH)
                w = jnp.exp2(jnp.where(upper, c, tot - c))
            part = _dot_nt((q * w).astype(BF16), (k * w).astype(BF16))
            scores = jnp.where(masks[j], part, 0.0 if scores is None else scores)
            c = jnp.where(upper, c + tot, c)
        else:
            c_lo, c_up = _split_halves(c, m)
            tot = _block_row_bcast(c_lo, m, m - 1)
            q_up = _split_halves(q, m)[1]
            k_lo, k_up = _split_halves(k, m)
            qt = (q_up * jnp.exp2(c_up)).astype(BF16)
            kt = _merge_halves(k_lo * jnp.exp2(tot - c_lo), k_up, m).astype(BF16)
            s_lo, s_up = _split_halves(scores, m)
            s_up = jnp.where(masks[j], _dot_nt(qt, kt), s_up)
            scores = _merge_halves(s_lo, s_up, m)
            c = _merge_halves(c_lo, c_up + tot, m)
    if mid_hook is not None:
        mid_hook()
    o = _dot(scores.astype(BF16), v.astype(BF16))
    o = o + jnp.sum(q * k, axis=-1, keepdims=True) * v
    return o, c


def _ret_mask(rows, seg, log_gamma):
    r = lax.broadcasted_iota(jnp.int32, (rows, rows), 0)
    s = lax.broadcasted_iota(jnp.int32, (rows, rows), 1)
    valid = jnp.logical_and(((r ^ s) >> int(math.log2(seg))) == 0, r >= s)
    d = jnp.where(valid, r - s, 0).astype(F32)
    return jnp.where(valid, jnp.exp(d * log_gamma) * QK_SCALE, 0.0)


def _log_gamma(hd):
    return math.log(1.0 - 2.0 ** (-5.0 - hd))


def _head_out(o, w, gate):
    return o * lax.rsqrt(jnp.mean(o * o, axis=-1, keepdims=True) + NORM_EPS) * w * _silu(gate)


def _project(x_ref, nw_ref, win_ref, proj_ref):
    h = _rms(x_ref[...], nw_ref[...]).astype(BF16)
    for g in range(8):
        cols = slice(g * GROUP_W, (g + 1) * GROUP_W)
        proj_ref[:, cols] = _dot(h, win_ref[:, cols])


def _cast_weight(w_hbm, w_bf_ref, stage_ref, sem):
    nbuf, brows, bcols = stage_ref.shape
    rows, width = w_hbm.shape
    k = width // bcols
    crows = brows // k
    n = rows // crows

    def copies(i):
        return [pltpu.make_async_copy(
            w_hbm.at[pl.ds(i * crows, crows), pl.ds(c * bcols, bcols)],
            stage_ref.at[i % nbuf, pl.ds(c * crows, crows), :],
            sem.at[i % nbuf, c]) for c in range(k)]

    for i in range(min(nbuf - 1, n)):
        for cp in copies(i):
            cp.start()
    for i in range(n):
        if i + nbuf - 1 < n:
            for cp in copies(i + nbuf - 1):
                cp.start()
        for c, cp in enumerate(copies(i)):
            cp.wait()
            w_bf_ref[pl.ds(i * crows, crows), pl.ds(c * bcols, bcols)] = (
                stage_ref[i % nbuf, c * crows:(c + 1) * crows, :].astype(BF16))


def _cols(group, hd):
    return slice(group * GROUP_W + hd * DH, group * GROUP_W + (hd + 1) * DH)


def _mix_prompt_kernel(nc, *refs):
    CH = PROMPT_STEP_CHUNKS
    L = PROMPT_CHUNK
    x0_ref = refs[0]
    xn_refs = refs[1:1 + CH]
    (xc_ref, nw_ref, win_hbm, cos_ref, sin_ref, rnw_ref, hnw_ref, lbl_ref, wout_hbm,
     x1_ref, sret_ref, shg_ref, win_ref, wout_ref,
     proj_ref, o_ref, srt_ref, sht_ref, dm_ref, qd_ref, kd_ref, stage_ref, sem) = refs[1 + CH:]
    g = pl.program_id(0)
    ci0 = (g * CH) % nc
    t = lax.broadcasted_iota(jnp.int32, (L, DH), 0)

    @pl.when(g == 0)
    def _():
        _cast_weight(win_hbm, win_ref, stage_ref, sem)
        _cast_weight(wout_hbm, wout_ref, stage_ref, sem)
        tf = t.astype(F32)
        for hd in range(HEADS):
            lg = _log_gamma(hd)
            dm_ref[hd] = _ret_mask(L, L, lg)
            qd_ref[hd] = jnp.exp((tf + 1.0) * lg)
            kd_ref[hd] = jnp.exp((L - 1.0 - tf) * lg) * QK_SCALE
        _project(x0_ref, nw_ref, win_ref, proj_ref.at[0])

    @pl.when(ci0 == 0)
    def _():
        srt_ref[...] = jnp.zeros_like(srt_ref)
        sht_ref[...] = jnp.zeros_like(sht_ref)

    lb = _lower_bound(lbl_ref[...])
    masks = _hgrn_level_masks(L, L)

    def chunk(pr, pw, xn_ref, rows):
        h_next = _rms(xn_ref[...], nw_ref[...]).astype(BF16)

        def project_piece(pi):
            cols = slice(pi * PROJ_PIECE, (pi + 1) * PROJ_PIECE)
            pw[:, cols] = _dot(h_next, win_ref[:, cols])

        cosf = cos_ref[rows, :]
        sinf = sin_ref[rows, :]

        for hd in range(HEADS):
            q = _rotary(pr[:, _cols(0, hd)], cosf, sinf)
            k = _rotary(pr[:, _cols(1, hd)], cosf, sinf)
            v = pr[:, _cols(2, hd)].astype(BF16)
            scores = _dot_nt(q.astype(BF16), k.astype(BF16)) * dm_ref[hd]
            project_piece(hd)
            o = _dot(scores.astype(BF16), v)
            st = srt_ref[hd]
            o = o + _dot_nt((q * qd_ref[hd]).astype(BF16), st.astype(BF16))
            kh = (k * kd_ref[hd]).astype(BF16)
            srt_ref[hd] = st * math.exp(L * _log_gamma(hd)) + _dot_tn(v, kh)
            o_ref[:, _cols(0, hd)] = _head_out(o, rnw_ref[...], pr[:, _cols(3, hd)]).astype(BF16)

        for hd in range(HEADS):
            q = _silu(pr[:, _cols(4, hd)]) * QK_SCALE
            lbh = lb[:, hd * DH:(hd + 1) * DH]
            f = lbh + (1.0 - lbh) * _sigmoid(pr[:, _cols(5, hd)])
            k = 1.0 - f
            v = pr[:, _cols(6, hd)]
            o, b = _hgrn_intra(q, k, f, v, L, t, masks,
                               functools.partial(project_piece, HEADS + hd))
            blast = b[L - 1:L, :]
            st = sht_ref[hd]
            o = o + _dot_nt((q * jnp.exp2(b)).astype(BF16), st.astype(BF16))
            kh = (k * jnp.exp2(blast - b)).astype(BF16)
            sht_ref[hd] = st * jnp.exp2(blast) + _dot_tn(v.astype(BF16), kh)
            o_ref[:, _cols(1, hd)] = _head_out(o, hnw_ref[...], pr[:, _cols(7, hd)]).astype(BF16)

        x1_ref[rows, :] = xc_ref[rows, :] + _dot(o_ref[...], wout_ref[...])

    for sub in range(CH):
        chunk(proj_ref.at[sub % 2], proj_ref.at[(sub + 1) % 2], xn_refs[sub],
              slice(sub * L, (sub + 1) * L))

    @pl.when(ci0 == nc - CH)
    def _():
        for hd in range(HEADS):
            sret_ref[hd] = srt_ref[hd].T
            shg_ref[hd] = sht_ref[hd].T


def _const_spec(shape):
    return pl.BlockSpec(shape, lambda *_: (0,) * len(shape), pipeline_mode=pl.Buffered(1))


def _mix_prompt(x, nw, win_f32, cosf, sinf, rnw, hnw, lbl, wout_f32):
    B, T, D = x.shape
    L = PROMPT_CHUNK
    CH = PROMPT_STEP_CHUNKS
    nc = T // L
    n = B * nc
    spb = nc // CH
    assert nc % CH == 0 and CH % 2 == 0

    def next_chunk_spec(i):
        idx = lambda g: jnp.minimum(CH * g + i + 1, n - 1)
        return pl.BlockSpec((None, L, D), lambda g: (idx(g) // nc, idx(g) % nc, 0))

    state = jax.ShapeDtypeStruct((B, HEADS, DH, DH), F32)
    state_spec = pl.BlockSpec((None, HEADS, DH, DH), lambda g: (g // spb, 0, 0, 0))
    rows_spec = pl.BlockSpec((None, CH * L, D), lambda g: (g // spb, g % spb, 0))
    table_spec = pl.BlockSpec((CH * L, DH), lambda g: (g % spb, 0))
    return pl.pallas_call(
        functools.partial(_mix_prompt_kernel, nc),
        grid=(n // CH,),
        in_specs=[
            pl.BlockSpec((None, L, D), lambda g: (0, 0, 0), pipeline_mode=pl.Buffered(1)),
            *[next_chunk_spec(i) for i in range(CH)],
            rows_spec,
            _const_spec((1, D)),
            pl.BlockSpec(memory_space=pl.ANY),
            table_spec,
            table_spec,
            _const_spec((1, DH)),
            _const_spec((1, DH)),
            _const_spec(lbl.shape),
            pl.BlockSpec(memory_space=pl.ANY),
        ],
        out_specs=[rows_spec, state_spec, state_spec,
                   pl.BlockSpec((D, IN_COLS), lambda g: (0, 0), pipeline_mode=pl.Buffered(1)),
                   pl.BlockSpec((D, D), lambda g: (0, 0), pipeline_mode=pl.Buffered(1))],
        out_shape=[jax.ShapeDtypeStruct((B, T, D), F32), state, state,
                   jax.ShapeDtypeStruct((D, IN_COLS), BF16), jax.ShapeDtypeStruct((D, D), BF16)],
        scratch_shapes=[
            pltpu.VMEM((2, L, IN_COLS), F32),
            pltpu.VMEM((L, D), BF16),
            pltpu.VMEM((HEADS, DH, DH), F32),
            pltpu.VMEM((HEADS, DH, DH), F32),
            pltpu.VMEM((HEADS, L, L), F32),
            pltpu.VMEM((HEADS, L, DH), F32),
            pltpu.VMEM((HEADS, L, DH), F32),
            pltpu.VMEM((CAST_BUFFERS, CAST_ROWS, D), F32),
            pltpu.SemaphoreType.DMA((CAST_BUFFERS, IN_COLS // D)),
        ],
        compiler_params=pltpu.CompilerParams(
            dimension_semantics=("arbitrary",), vmem_limit_bytes=VMEM_LIMIT),
        name="mix_prompt",
    )(x, *([x] * CH), x, nw, win_f32, cosf, sinf, rnw, hnw, lbl, wout_f32)


def _mix_sample_kernel(seg, x_ref, nw_ref, win_ref, cos_ref, sin_ref, rnw_ref, hnw_ref, lbl_ref,
                       wout_ref, sret_in_ref, shg_in_ref,
                       x1_ref, sret_ref, shg_ref,
                       proj_ref, o_ref, qh_ref, kh_ref, dec_ref, ob_ref):
    rows = x_ref.shape[0]
    nsub = sret_in_ref.shape[0]
    j = pl.program_id(1)

    @pl.when(j == 0)
    def _():
        _project(x_ref, nw_ref, win_ref, proj_ref)
        t = lax.broadcasted_iota(jnp.int32, (rows, DH), 0)
        tl = (t & (seg - 1)).astype(F32)
        cosf = cos_ref[...]
        sinf = sin_ref[...]
        lb = _lower_bound(lbl_ref[...])
        masks = _hgrn_level_masks(rows, seg)
        for hd in range(HEADS):
            lg = _log_gamma(hd)
            q = _rotary(proj_ref[:, _cols(0, hd)], cosf, sinf)
            k = _rotary(proj_ref[:, _cols(1, hd)], cosf, sinf)
            scores = _dot_nt(q.astype(BF16), k.astype(BF16)) * _ret_mask(rows, seg, lg)
            o_ref[:, _cols(0, hd)] = _dot(scores.astype(BF16),
                                          proj_ref[:, _cols(2, hd)].astype(BF16))
            qh_ref[:, _cols(0, hd)] = q * jnp.exp((tl + 1.0) * lg)
            kh_ref[:, _cols(0, hd)] = k * (jnp.exp((seg - 1.0 - tl) * lg) * QK_SCALE)
        for hd in range(HEADS):
            q = _silu(proj_ref[:, _cols(4, hd)]) * QK_SCALE
            lbh = lb[:, hd * DH:(hd + 1) * DH]
            f = lbh + (1.0 - lbh) * _sigmoid(proj_ref[:, _cols(5, hd)])
            k = 1.0 - f
            o, b = _hgrn_intra(q, k, f, proj_ref[:, _cols(6, hd)], seg, t, masks)
            blast = _block_row_bcast(b, seg, seg - 1)
            o_ref[:, _cols(1, hd)] = o
            qh_ref[:, _cols(1, hd)] = q * jnp.exp2(b)
            kh_ref[:, _cols(1, hd)] = k * jnp.exp2(blast - b)
            dec_ref[:, hd * DH:(hd + 1) * DH] = jnp.exp2(blast)

    def seq_body(s, carry):
        r = pl.ds(pl.multiple_of((j * nsub + s) * seg, seg), seg)
        qh = qh_ref[r, :].astype(BF16)
        kh = kh_ref[r, :].astype(BF16)
        v_ret = proj_ref[r, 2 * GROUP_W:3 * GROUP_W].astype(BF16)
        v_hg = proj_ref[r, 6 * GROUP_W:7 * GROUP_W].astype(BF16)
        dec = dec_ref[r, :][0:1, :]
        o_parts, new_ret, new_hg = [], [], []
        for hd in range(HEADS):
            st = sret_in_ref[s, hd]
            c = slice(hd * DH, (hd + 1) * DH)
            o_parts.append(_dot(qh[:, c], st.astype(BF16)))
            new_ret.append(st * math.exp(seg * _log_gamma(hd)) + _dot_tn(kh[:, c], v_ret[:, c]))
        for hd in range(HEADS):
            st = shg_in_ref[s, hd]
            c = slice(hd * DH, (hd + 1) * DH)
            c1 = slice(GROUP_W + hd * DH, GROUP_W + (hd + 1) * DH)
            o_parts.append(_dot(qh[:, c1], st.astype(BF16)))
            dcol = jnp.broadcast_to(dec[:, c], (DH, DH)).T
            new_hg.append(st * dcol + _dot_tn(kh[:, c1], v_hg[:, c]))
        o_ref[r, :] = o_ref[r, :] + jnp.concatenate(o_parts, axis=1)
        for hd in range(HEADS):
            sret_ref[s, hd] = new_ret[hd]
            shg_ref[s, hd] = new_hg[hd]
        return carry

    lax.fori_loop(0, nsub, seq_body, 0, unroll=2)

    @pl.when(j == pl.num_programs(1) - 1)
    def _():
        for hd in range(HEADS):
            c0 = _cols(0, hd)
            ob_ref[:, c0] = _head_out(o_ref[:, c0], rnw_ref[...],
                                      proj_ref[:, _cols(3, hd)]).astype(BF16)
            c1 = _cols(1, hd)
            ob_ref[:, c1] = _head_out(o_ref[:, c1], hnw_ref[...],
                                      proj_ref[:, _cols(7, hd)]).astype(BF16)
        x1_ref[...] = x_ref[...] + _dot(ob_ref[...], wout_ref[...])


def _mix_sample(x, nw, win, cosf, sinf, rnw, hnw, lbl, wout, sret, shg):
    nb, seg, D = x.shape
    rows = SAMPLE_ROWS
    nsub = SAMPLE_SUB_SEQS
    nj = rows // seg // nsub
    x2 = x.reshape(nb * seg, D)
    state = jax.ShapeDtypeStruct((nb, HEADS, DH, DH), F32)
    state_map = lambda i, j: (i * nj + j, 0, 0, 0)
    state_in_spec = pl.BlockSpec((nsub, HEADS, DH, DH), state_map,
                                 pipeline_mode=pl.Buffered(SAMPLE_STATE_BUFFERS))
    state_out_spec = pl.BlockSpec((nsub, HEADS, DH, DH), state_map)
    x1, sret_new, shg_new = pl.pallas_call(
        functools.partial(_mix_sample_kernel, seg),
        grid=(nb * seg // rows, nj),
        in_specs=[
            pl.BlockSpec((rows, D), lambda i, j: (i, 0)),
            _const_spec((1, D)),
            _const_spec((D, IN_COLS)),
            _const_spec((rows, DH)),
            _const_spec((rows, DH)),
            _const_spec((1, DH)),
            _const_spec((1, DH)),
            _const_spec(lbl.shape),
            _const_spec((D, D)),
            state_in_spec,
            state_in_spec,
        ],
        out_specs=[pl.BlockSpec((rows, D), lambda i, j: (i, 0)), state_out_spec, state_out_spec],
        out_shape=[jax.ShapeDtypeStruct((nb * seg, D), F32), state, state],
        scratch_shapes=[
            pltpu.VMEM((rows, IN_COLS), F32),
            pltpu.VMEM((rows, D), F32),
            pltpu.VMEM((rows, D), F32),
            pltpu.VMEM((rows, D), F32),
            pltpu.VMEM((rows, GROUP_W), F32),
            pltpu.VMEM((rows, D), BF16),
        ],
        compiler_params=pltpu.CompilerParams(
            dimension_semantics=("arbitrary", "arbitrary"), vmem_limit_bytes=VMEM_LIMIT),
        name="mix_sample",
    )(x2, nw, win, cosf, sinf, rnw, hnw, lbl, wout, sret, shg)
    return x1, sret_new, shg_new


def _ffn_kernel(x_ref, nw_ref, wup_ref, wdn_ref, fw_ref, y_ref):
    subs = [slice(r * FFN_SUB_ROWS, (r + 1) * FFN_SUB_ROWS)
            for r in range(x_ref.shape[0] // FFN_SUB_ROWS)]
    hs = [_rms(x_ref[rs, :], nw_ref[...]).astype(BF16) for rs in subs]
    accs = [x_ref[rs, :] for rs in subs]
    step = D_MODEL
    for g in range(D_FF // step):
        us = [jnp.maximum(_dot(h, wup_ref[:, g * step:(g + 1) * step]), 0.0) for h in hs]
        accs = [acc + _dot((u * u).astype(BF16), wdn_ref[g * step:(g + 1) * step, :])
                for acc, u in zip(accs, us)]
    for rs, acc in zip(subs, accs):
        y_ref[rs, :] = _rms(acc, fw_ref[...])


def _ffn_cast_kernel(n_prompt_steps, xp_ref, xs_ref, nw_ref, wup_hbm, wdn_hbm, fw_ref,
                     yp_ref, ys_ref, wup_ref, wdn_ref, stage_ref, sem):
    i = pl.program_id(0)

    @pl.when(i == 0)
    def _():
        _cast_weight(wup_hbm, wup_ref, stage_ref, sem)
        _cast_weight(wdn_hbm, wdn_ref, stage_ref, sem)

    @pl.when(i < n_prompt_steps)
    def _():
        _ffn_kernel(xp_ref, nw_ref, wup_ref, wdn_ref, fw_ref, yp_ref)

    @pl.when(i >= n_prompt_steps)
    def _():
        _ffn_kernel(xs_ref, nw_ref, wup_ref, wdn_ref, fw_ref, ys_ref)


def _ffn(xp, xs, nw, wup_f32, wdn_f32, fw):
    n_p, D = xp.shape
    n_s = xs.shape[0]
    rows = FFN_ROWS
    assert n_p % rows == 0 and n_s % rows == 0
    p_steps, s_steps = n_p // rows, n_s // rows
    p_spec = pl.BlockSpec((rows, D), lambda i: (jnp.minimum(i, p_steps - 1), 0))
    s_spec = pl.BlockSpec((rows, D), lambda i: (jnp.maximum(i - p_steps, 0), 0),
                          pipeline_mode=pl.Buffered(1))
    return pl.pallas_call(
        functools.partial(_ffn_cast_kernel, p_steps),
        grid=(p_steps + s_steps,),
        in_specs=[
            p_spec,
            s_spec,
            _const_spec((1, D)),
            pl.BlockSpec(memory_space=pl.ANY),
            pl.BlockSpec(memory_space=pl.ANY),
            _const_spec((1, D)),
        ],
        out_specs=[p_spec, s_spec],
        out_shape=[jax.ShapeDtypeStruct((n_p, D), F32), jax.ShapeDtypeStruct((n_s, D), F32)],
        scratch_shapes=[
            pltpu.VMEM((D, D_FF), BF16),
            pltpu.VMEM((D_FF, D), BF16),
            pltpu.VMEM((CAST_BUFFERS, CAST_ROWS, D), F32),
            pltpu.SemaphoreType.DMA((CAST_BUFFERS, D_FF // D)),
        ],
        compiler_params=pltpu.CompilerParams(
            dimension_semantics=("arbitrary",), vmem_limit_bytes=FFN_VMEM_LIMIT),
        name="ffn",
    )(xp, xs, nw, wup_f32, wdn_f32, fw)


def _rope_tables(pos):
    half = DH // 2
    inv_freq = ROPE_BASE ** (-jnp.arange(half, dtype=F32) / half)
    ang = pos.astype(F32)[:, None] * inv_freq[None, :]
    cos, sin = jnp.cos(ang), jnp.sin(ang)
    return jnp.concatenate([cos, cos], axis=-1), jnp.concatenate([-sin, sin], axis=-1)


def kernel(x_prompt, x_sample, state_ret, state_hgrn, norm_mix_w, w_in, ret_norm_w, hgrn_norm_w,
           lb_logits, w_out, norm_ffn_w, w_up, w_down, final_norm_w):
    B, T, D = x_prompt.shape
    nb, seg, _ = x_sample.shape
    assert w_in.shape[0] == 1, "one layer"

    nw = norm_mix_w[0].reshape(1, D)
    rnw = ret_norm_w[0].reshape(1, DH)
    hnw = hgrn_norm_w[0].reshape(1, DH)
    lbl = lb_logits.astype(F32)
    fnw = norm_ffn_w[0].reshape(1, D)
    fw = final_norm_w.reshape(1, D)

    cos_p, sin_p = _rope_tables(jnp.arange(T, dtype=jnp.int32))
    cos_s, sin_s = _rope_tables(PAST_LEN + jnp.arange(seg, dtype=jnp.int32))
    cos_s = jnp.tile(cos_s, (SAMPLE_ROWS // seg, 1))
    sin_s = jnp.tile(sin_s, (SAMPLE_ROWS // seg, 1))

    xp1, sret_p, shg_p, win, wout = _mix_prompt(x_prompt, nw, w_in[0], cos_p, sin_p, rnw, hnw, lbl,
                                                w_out[0])
    xs1, sret_s, shg_s = _mix_sample(x_sample, nw, win, cos_s, sin_s, rnw, hnw, lbl, wout,
                                     state_ret[0], state_hgrn[0])

    y_p, y_s = _ffn(xp1.reshape(B * T, D), xs1, fnw, w_up[0], w_down[0], fw)
    y_p = y_p.reshape(B, T, D)
    y_s = y_s.reshape(nb, seg, D)
    return (y_p, y_s, sret_p[None], shg_p[None], sret_s[None], shg_s[None])
```

```python
import functools
import math

import jax
import jax.numpy as jnp
from jax import lax
from jax.experimental import pallas as pl
from jax.experimental.pallas import tpu as pltpu

D_MODEL = 1024
HEADS = 4
DH = 128
GROUP_W = HEADS * DH
IN_COLS = 8 * GROUP_W
D_FF = 4 * D_MODEL
ROPE_BASE = 10000.0
NORM_EPS = 1e-6
QK_SCALE = DH ** -0.5
PAST_LEN = 16384

PROMPT_CHUNK = 256
PROMPT_STEP_CHUNKS = 2
SAMPLE_ROWS = 256
SAMPLE_SUB_SEQS = 16
SAMPLE_STATE_BUFFERS = 2
FFN_ROWS = 1024
FFN_SUB_ROWS = 512
PROJ_PIECE = 512
CAST_BUFFERS = 4
CAST_ROWS = 512
VMEM_LIMIT = 56 * 1024 * 1024
FFN_VMEM_LIMIT = 60 * 1024 * 1024
SAMPLE_VMEM_LIMIT = 60 * 1024 * 1024

F32 = jnp.float32
BF16 = jnp.bfloat16


def _dot(a, b):
    return jnp.dot(a, b, preferred_element_type=F32)


def _dot_nt(a, b):
    return lax.dot_general(a, b, (((1,), (1,)), ((), ())), preferred_element_type=F32)


def _dot_tn(a, b):
    return lax.dot_general(a, b, (((0,), (0,)), ((), ())), preferred_element_type=F32)


def _rms(x, w):
    ms = jnp.mean(x * x, axis=-1, keepdims=True)
    return x * lax.rsqrt(ms + NORM_EPS) * w


def _sigmoid(x):
    return 0.5 * jnp.tanh(0.5 * x) + 0.5


def _silu(x):
    return x * _sigmoid(x)


def _lower_bound(lbl):
    mx = jnp.max(lbl, axis=0, keepdims=True)
    e = jnp.exp(lbl - mx)
    return e[0:1, :] / jnp.sum(e, axis=0, keepdims=True)


def _rotary(x, cosf, sinf):
    return x * cosf + pltpu.roll(x, DH // 2, 1) * sinf


def _split_halves(x, m):
    blk = 2 * m
    nb = x.shape[0] // blk
    lo = [x[b * blk:b * blk + m] for b in range(nb)]
    up = [x[b * blk + m:(b + 1) * blk] for b in range(nb)]
    cat = lambda ps: ps[0] if len(ps) == 1 else jnp.concatenate(ps, axis=0)
    return cat(lo), cat(up)


def _merge_halves(lo, up, m):
    nb = lo.shape[0] // m
    pieces = []
    for b in range(nb):
        pieces += [lo[b * m:(b + 1) * m], up[b * m:(b + 1) * m]]
    return jnp.concatenate(pieces, axis=0)


def _block_row_bcast(x, m, row):
    nb = x.shape[0] // m
    pieces = [jnp.broadcast_to(x[b * m + row:b * m + row + 1], (m, x.shape[1])) for b in range(nb)]
    return pieces[0] if nb == 1 else jnp.concatenate(pieces, axis=0)


def _hgrn_level_masks(rows, seg):
    masks = []
    for j in range(int(math.log2(seg))):
        if (1 << j) < 8:
            r = lax.broadcasted_iota(jnp.int32, (rows, rows), 0)
            s = lax.broadcasted_iota(jnp.int32, (rows, rows), 1)
            masks.append(jnp.logical_and(((r ^ s) >> j) == 1, r > s))
        else:
            r = lax.broadcasted_iota(jnp.int32, (rows // 2, rows), 0)
            s = lax.broadcasted_iota(jnp.int32, (rows // 2, rows), 1)
            masks.append((s >> j) == 2 * (r >> j))
    return masks


def _hgrn_intra(q, k, f, v, seg, t, masks, mid_hook=None):
    rows = q.shape[0]
    c = jnp.log2(f)
    scores = None
    for j in range(int(math.log2(seg))):
        m = 1 << j
        if m < 8:
            c3 = c.reshape(rows // 8, 8, DH)
            upper = (t & m) != 0
            if m == 1:
                w = jnp.where(upper, f, 1.0)
                tot = pltpu.roll(c3, 1, 1).reshape(rows, DH)
            else:
                if m == 2:
                    low4 = ((t & 7) < 4).reshape(rows // 8, 8, DH)
                    tot3 = jnp.where(low4, jnp.broadcast_to(c3[:, 1:2, :], c3.shape),
                                     jnp.broadcast_to(c3[:, 5:6, :], c3.shape))
                else:
                    tot3 = jnp.broadcast_to(c3[:, 3:4, :], c3.shape)
                tot = tot3.reshape(rows, DH)
                w = jnp.exp2(jnp.where(upper, c, tot - c))
            part = _dot_nt((q * w).astype(BF16), (k * w).astype(BF16))
            scores = jnp.where(masks[j], part, 0.0 if scores is None else scores)
            c = jnp.where(upper, c + tot, c)
        else:
            c_lo, c_up = _split_halves(c, m)
            tot = _block_row_bcast(c_lo, m, m - 1)
            q_up = _split_halves(q, m)[1]
            k_lo, k_up = _split_halves(k, m)
            qt = (q_up * jnp.exp2(c_up)).astype(BF16)
            kt = _merge_halves(k_lo * jnp.exp2(tot - c_lo), k_up, m).astype(BF16)
            s_lo, s_up = _split_halves(scores, m)
            s_up = jnp.where(masks[j], _dot_nt(qt, kt), s_up)
            scores = _merge_halves(s_lo, s_up, m)
            c = _merge_halves(c_lo, c_up + tot, m)
    if mid_hook is not None:
        mid_hook()
    o = _dot(scores.astype(BF16), v.astype(BF16))
    o = o + jnp.sum(q * k, axis=-1, keepdims=True) * v
    return o, c


def _ret_mask(rows, seg, log_gamma):
    r = lax.broadcasted_iota(jnp.int32, (rows, rows), 0)
    s = lax.broadcasted_iota(jnp.int32, (rows, rows), 1)
    valid = jnp.logical_and(((r ^ s) >> int(math.log2(seg))) == 0, r >= s)
    d = jnp.where(valid, r - s, 0).astype(F32)
    return jnp.where(valid, jnp.exp(d * log_gamma) * QK_SCALE, 0.0)


def _log_gamma(hd):
    return math.log(1.0 - 2.0 ** (-5.0 - hd))


def _head_out(o, w, gate):
    return o * lax.rsqrt(jnp.mean(o * o, axis=-1, keepdims=True) + NORM_EPS) * w * _silu(gate)


def _project(x_ref, nw_ref, win_ref, proj_ref):
    h = _rms(x_ref[...], nw_ref[...]).astype(BF16)
    for g in range(8):
        cols = slice(g * GROUP_W, (g + 1) * GROUP_W)
        proj_ref[:, cols] = _dot(h, win_ref[:, cols])


def _cast_weight(w_hbm, w_bf_ref, stage_ref, sem):
    nbuf, brows, bcols = stage_ref.shape
    rows, width = w_hbm.shape
    k = width // bcols
    crows = brows // k
    n = rows // crows

    def copies(i):
        return [pltpu.make_async_copy(
            w_hbm.at[pl.ds(i * crows, crows), pl.ds(c * bcols, bcols)],
            stage_ref.at[i % nbuf, pl.ds(c * crows, crows), :],
            sem.at[i % nbuf, c]) for c in range(k)]

    for i in range(min(nbuf - 1, n)):
        for cp in copies(i):
            cp.start()
    for i in range(n):
        if i + nbuf - 1 < n:
            for cp in copies(i + nbuf - 1):
                cp.start()
        for c, cp in enumerate(copies(i)):
            cp.wait()
            w_bf_ref[pl.ds(i * crows, crows), pl.ds(c * bcols, bcols)] = (
                stage_ref[i % nbuf, c * crows:(c + 1) * crows, :].astype(BF16))


def _cols(group, hd):
    return slice(group * GROUP_W + hd * DH, group * GROUP_W + (hd + 1) * DH)


def _mix_prompt_kernel(nc, *refs):
    CH = PROMPT_STEP_CHUNKS
    L = PROMPT_CHUNK
    x0_ref = refs[0]
    xn_refs = refs[1:1 + CH]
    (xc_ref, nw_ref, win_hbm, cos_ref, sin_ref, rnw_ref, hnw_ref, lbl_ref, wout_hbm,
     x1_ref, sret_ref, shg_ref, win_ref, wout_ref,
     proj_ref, o_ref, srt_ref, sht_ref, dm_ref, qd_ref, kd_ref, stage_ref, sem) = refs[1 + CH:]
    g = pl.program_id(0)
    ci0 = (g * CH) % nc
    t = lax.broadcasted_iota(jnp.int32, (L, DH), 0)

    @pl.when(g == 0)
    def _():
        _cast_weight(win_hbm, win_ref, stage_ref, sem)
        _cast_weight(wout_hbm, wout_ref, stage_ref, sem)
        tf = t.astype(F32)
        for hd in range(HEADS):
            lg = _log_gamma(hd)
            dm_ref[hd] = _ret_mask(L, L, lg)
            qd_ref[hd] = jnp.exp((tf + 1.0) * lg)
            kd_ref[hd] = jnp.exp((L - 1.0 - tf) * lg) * QK_SCALE
        _project(x0_ref, nw_ref, win_ref, proj_ref.at[0])

    @pl.when(ci0 == 0)
    def _():
        srt_ref[...] = jnp.zeros_like(srt_ref)
        sht_ref[...] = jnp.zeros_like(sht_ref)

    lb = _lower_bound(lbl_ref[...])
    masks = _hgrn_level_masks(L, L)

    def chunk(pr, pw, xn_ref, rows):
        h_next = _rms(xn_ref[...], nw_ref[...]).astype(BF16)

        def project_piece(pi):
            cols = slice(pi * PROJ_PIECE, (pi + 1) * PROJ_PIECE)
            pw[:, cols] = _dot(h_next, win_ref[:, cols])

        cosf = cos_ref[rows, :]
        sinf = sin_ref[rows, :]

        for hd in range(HEADS):
            q = _rotary(pr[:, _cols(0, hd)], cosf, sinf)
            k = _rotary(pr[:, _cols(1, hd)], cosf, sinf)
            v = pr[:, _cols(2, hd)].astype(BF16)
            scores = _dot_nt(q.astype(BF16), k.astype(BF16)) * dm_ref[hd]
            project_piece(hd)
            o = _dot(scores.astype(BF16), v)
            st = srt_ref[hd]
            o = o + _dot_nt((q * qd_ref[hd]).astype(BF16), st.astype(BF16))
            kh = (k * kd_ref[hd]).astype(BF16)
            srt_ref[hd] = st * math.exp(L * _log_gamma(hd)) + _dot_tn(v, kh)
            o_ref[:, _cols(0, hd)] = _head_out(o, rnw_ref[...], pr[:, _cols(3, hd)]).astype(BF16)

        for hd in range(HEADS):
            q = _silu(pr[:, _cols(4, hd)]) * QK_SCALE
            lbh = lb[:, hd * DH:(hd + 1) * DH]
            f = lbh + (1.0 - lbh) * _sigmoid(pr[:, _cols(5, hd)])
            k = 1.0 - f
            v = pr[:, _cols(6, hd)]
            o, b = _hgrn_intra(q, k, f, v, L, t, masks,
                               functools.partial(project_piece, HEADS + hd))
            blast = b[L - 1:L, :]
            st = sht_ref[hd]
            o = o + _dot_nt((q * jnp.exp2(b)).astype(BF16), st.astype(BF16))
            kh = (k * jnp.exp2(blast - b)).astype(BF16)
            sht_ref[hd] = st * jnp.exp2(blast) + _dot_tn(v.astype(BF16), kh)
            o_ref[:, _cols(1, hd)] = _head_out(o, hnw_ref[...], pr[:, _cols(7, hd)]).astype(BF16)

        x1_ref[rows, :] = xc_ref[rows, :] + _dot(o_ref[...], wout_ref[...])

    for sub in range(CH):
        chunk(proj_ref.at[sub % 2], proj_ref.at[(sub + 1) % 2], xn_refs[sub],
              slice(sub * L, (sub + 1) * L))

    @pl.when(ci0 == nc - CH)
    def _():
        for hd in range(HEADS):
            sret_ref[hd] = srt_ref[hd].T
            shg_ref[hd] = sht_ref[hd].T


def _const_spec(shape):
    return pl.BlockSpec(shape, lambda *_: (0,) * len(shape), pipeline_mode=pl.Buffered(1))


def _mix_prompt(x, nw, win_f32, cosf, sinf, rnw, hnw, lbl, wout_f32):
    B, T, D = x.shape
    L = PROMPT_CHUNK
    CH = PROMPT_STEP_CHUNKS
    nc = T // L
    n = B * nc
    spb = nc // CH
    assert nc % CH == 0 and CH % 2 == 0

    def next_chunk_spec(i):
        idx = lambda g: jnp.minimum(CH * g + i + 1, n - 1)
        return pl.BlockSpec((None, L, D), lambda g: (idx(g) // nc, idx(g) % nc, 0))

    state = jax.ShapeDtypeStruct((B, HEADS, DH, DH), F32)
    state_spec = pl.BlockSpec((None, HEADS, DH, DH), lambda g: (g // spb, 0, 0, 0))
    rows_spec = pl.BlockSpec((None, CH * L, D), lambda g: (g // spb, g % spb, 0))
    table_spec = pl.BlockSpec((CH * L, DH), lambda g: (g % spb, 0))
    return pl.pallas_call(
        functools.partial(_mix_prompt_kernel, nc),
        grid=(n // CH,),
        in_specs=[
            pl.BlockSpec((None, L, D), lambda g: (0, 0, 0), pipeline_mode=pl.Buffered(1)),
            *[next_chunk_spec(i) for i in range(CH)],
            rows_spec,
            _const_spec((1, D)),
            pl.BlockSpec(memory_space=pl.ANY),
            table_spec,
            table_spec,
            _const_spec((1, DH)),
            _const_spec((1, DH)),
            _const_spec(lbl.shape),
            pl.BlockSpec(memory_space=pl.ANY),
        ],
        out_specs=[rows_spec, state_spec, state_spec,
                   pl.BlockSpec((D, IN_COLS), lambda g: (0, 0), pipeline_mode=pl.Buffered(1)),
                   pl.BlockSpec((D, D), lambda g: (0, 0), pipeline_mode=pl.Buffered(1))],
        out_shape=[jax.ShapeDtypeStruct((B, T, D), F32), state, state,
                   jax.ShapeDtypeStruct((D, IN_COLS), BF16), jax.ShapeDtypeStruct((D, D), BF16)],
        scratch_shapes=[
            pltpu.VMEM((2, L, IN_COLS), F32),
            pltpu.VMEM((L, D), BF16),
            pltpu.VMEM((HEADS, DH, DH), F32),
            pltpu.VMEM((HEADS, DH, DH), F32),
            pltpu.VMEM((HEADS, L, L), F32),
            pltpu.VMEM((HEADS, L, DH), F32),
            pltpu.VMEM((HEADS, L, DH), F32),
            pltpu.VMEM((CAST_BUFFERS, CAST_ROWS, D), F32),
            pltpu.SemaphoreType.DMA((CAST_BUFFERS, IN_COLS // D)),
        ],
        compiler_params=pltpu.CompilerParams(
            dimension_semantics=("arbitrary",), vmem_limit_bytes=VMEM_LIMIT),
        name="mix_prompt",
    )(x, *([x] * CH), x, nw, win_f32, cosf, sinf, rnw, hnw, lbl, wout_f32)


def _mix_sample_kernel(seg, x_ref, nw_ref, win_ref, cos_ref, sin_ref, rnw_ref, hnw_ref, lbl_ref,
                       wout_ref, sret_in_ref, shg_in_ref,
                       x1_ref, sret_ref, shg_ref,
                       proj_ref, o_ref, qh_ref, kh_ref, dec_ref, ob_ref):
    rows = x_ref.shape[0]
    nsub = sret_in_ref.shape[0]
    j = pl.program_id(1)

    @pl.when(j == 0)
    def _():
        _project(x_ref, nw_ref, win_ref, proj_ref)
        t = lax.broadcasted_iota(jnp.int32, (rows, DH), 0)
        tl = (t & (seg - 1)).astype(F32)
        cosf = cos_ref[...]
        sinf = sin_ref[...]
        lb = _lower_bound(lbl_ref[...])
        masks = _hgrn_level_masks(rows, seg)
        for hd in range(HEADS):
            lg = _log_gamma(hd)
            q = _rotary(proj_ref[:, _cols(0, hd)], cosf, sinf)
            k = _rotary(proj_ref[:, _cols(1, hd)], cosf, sinf)
            scores = _dot_nt(q.astype(BF16), k.astype(BF16)) * _ret_mask(rows, seg, lg)
            o_ref[:, _cols(0, hd)] = _dot(scores.astype(BF16),
                                          proj_ref[:, _cols(2, hd)].astype(BF16))
            qh_ref[:, _cols(0, hd)] = q * jnp.exp((tl + 1.0) * lg)
            kh_ref[:, _cols(0, hd)] = k * (jnp.exp((seg - 1.0 - tl) * lg) * QK_SCALE)
        for hd in range(HEADS):
            q = _silu(proj_ref[:, _cols(4, hd)]) * QK_SCALE
            lbh = lb[:, hd * DH:(hd + 1) * DH]
            f = lbh + (1.0 - lbh) * _sigmoid(proj_ref[:, _cols(5, hd)])
            k = 1.0 - f
            o, b = _hgrn_intra(q, k, f, proj_ref[:, _cols(6, hd)], seg, t, masks)
            blast = _block_row_bcast(b, seg, seg - 1)
            o_ref[:, _cols(1, hd)] = o
            qh_ref[:, _cols(1, hd)] = q * jnp.exp2(b)
            kh_ref[:, _cols(1, hd)] = k * jnp.exp2(blast - b)
            dec_ref[:, hd * DH:(hd + 1) * DH] = jnp.exp2(blast)

    def seq_body(s, carry):
        r = pl.ds(pl.multiple_of((j * nsub + s) * seg, seg), seg)
        qh = qh_ref[r, :].astype(BF16)
        kh = kh_ref[r, :].astype(BF16)
        v_ret = proj_ref[r, 2 * GROUP_W:3 * GROUP_W].astype(BF16)
        v_hg = proj_ref[r, 6 * GROUP_W:7 * GROUP_W].astype(BF16)
        dec = dec_ref[r, :][0:1, :]
        o_parts, new_ret, new_hg = [], [], []
        for hd in range(HEADS):
            st = sret_in_ref[s, hd]
            c = slice(hd * DH, (hd + 1) * DH)
            o_parts.append(_dot(qh[:, c], st.astype(BF16)))
            new_ret.append(st * math.exp(seg * _log_gamma(hd)) + _dot_tn(kh[:, c], v_ret[:, c]))
        for hd in range(HEADS):
            st = shg_in_ref[s, hd]
            c = slice(hd * DH, (hd + 1) * DH)
            c1 = slice(GROUP_W + hd * DH, GROUP_W + (hd + 1) * DH)
            o_parts.append(_dot(qh[:, c1], st.astype(BF16)))
            dcol = jnp.broadcast_to(dec[:, c], (DH, DH)).T
            new_hg.append(st * dcol + _dot_tn(kh[:, c1], v_hg[:, c]))
        o_ref[r, :] = o_ref[r, :] + jnp.concatenate(o_parts, axis=1)
        for hd in range(HEADS):
            sret_ref[s, hd] = new_ret[hd]
            shg_ref[s, hd] = new_hg[hd]
        return carry

    lax.fori_loop(0, nsub, seq_body, 0, unroll=2)

    @pl.when(j == pl.num_programs(1) - 1)
    def _():
        for hd in range(HEADS):
            c0 = _cols(0, hd)
            ob_ref[:, c0] = _head_out(o_ref[:, c0], rnw_ref[...],
                                      proj_ref[:, _cols(3, hd)]).astype(BF16)
            c1 = _cols(1, hd)
            ob_ref[:, c1] = _head_out(o_ref[:, c1], hnw_ref[...],
                                      proj_ref[:, _cols(7, hd)]).astype(BF16)
        x1_ref[...] = x_ref[...] + _dot(ob_ref[...], wout_ref[...])


def _mix_sample(x, nw, win, cosf, sinf, rnw, hnw, lbl, wout, sret, shg):
    nb, seg, D = x.shape
    rows = SAMPLE_ROWS
    nsub = SAMPLE_SUB_SEQS
    nj = rows // seg // nsub
    x2 = x.reshape(nb * seg, D)
    state = jax.ShapeDtypeStruct((nb, HEADS, DH, DH), F32)
    state_map = lambda i, j: (i * nj + j, 0, 0, 0)
    state_in_spec = pl.BlockSpec((nsub, HEADS, DH, DH), state_map,
                                 pipeline_mode=pl.Buffered(SAMPLE_STATE_BUFFERS))
    state_out_spec = pl.BlockSpec((nsub, HEADS, DH, DH), state_map)
    x1, sret_new, shg_new = pl.pallas_call(
        functools.partial(_mix_sample_kernel, seg),
        grid=(nb * seg // rows, nj),
        in_specs=[
            pl.BlockSpec((rows, D), lambda i, j: (i, 0)),
            _const_spec((1, D)),
            _const_spec((D, IN_COLS)),
            _const_spec((rows, DH)),
            _const_spec((rows, DH)),
            _const_spec((1, DH)),
            _const_spec((1, DH)),
            _const_spec(lbl.shape),
            _const_spec((D, D)),
            state_in_spec,
            state_in_spec,
        ],
        out_specs=[pl.BlockSpec((rows, D), lambda i, j: (i, 0)), state_out_spec, state_out_spec],
        out_shape=[jax.ShapeDtypeStruct((nb * seg, D), F32), state, state],
        scratch_shapes=[
            pltpu.VMEM((rows, IN_COLS), F32),
            pltpu.VMEM((rows, D), F32),
            pltpu.VMEM((rows, D), F32),
            pltpu.VMEM((rows, D), F32),
            pltpu.VMEM((rows, GROUP_W), F32),
            pltpu.VMEM((rows, D), BF16),
        ],
        compiler_params=pltpu.CompilerParams(
            dimension_semantics=("arbitrary", "arbitrary"), vmem_limit_bytes=SAMPLE_VMEM_LIMIT),
        name="mix_sample",
    )(x2, nw, win, cosf, sinf, rnw, hnw, lbl, wout, sret, shg)
    return x1, sret_new, shg_new


def _ffn_kernel(x_ref, nw_ref, wup_ref, wdn_ref, fw_ref, y_ref):
    subs = [slice(r * FFN_SUB_ROWS, (r + 1) * FFN_SUB_ROWS)
            for r in range(x_ref.shape[0] // FFN_SUB_ROWS)]
    hs = [_rms(x_ref[rs, :], nw_ref[...]).astype(BF16) for rs in subs]
    accs = [x_ref[rs, :] for rs in subs]
    step = D_MODEL
    for g in range(D_FF // step):
        us = [jnp.maximum(_dot(h, wup_ref[:, g * step:(g + 1) * step]), 0.0) for h in hs]
        accs = [acc + _dot((u * u).astype(BF16), wdn_ref[g * step:(g + 1) * step, :])
                for acc, u in zip(accs, us)]
    for rs, acc in zip(subs, accs):
        y_ref[rs, :] = _rms(acc, fw_ref[...])


def _ffn_cast_kernel(n_prompt_steps, xp_ref, xs_ref, nw_ref, wup_hbm, wdn_hbm, fw_ref,
                     yp_ref, ys_ref, wup_ref, wdn_ref, stage_ref, sem):
    i = pl.program_id(0)

    @pl.when(i == 0)
    def _():
        _cast_weight(wup_hbm, wup_ref, stage_ref, sem)
        _cast_weight(wdn_hbm, wdn_ref, stage_ref, sem)

    @pl.when(i < n_prompt_steps)
    def _():
        _ffn_kernel(xp_ref, nw_ref, wup_ref, wdn_ref, fw_ref, yp_ref)

    @pl.when(i >= n_prompt_steps)
    def _():
        _ffn_kernel(xs_ref, nw_ref, wup_ref, wdn_ref, fw_ref, ys_ref)


def _ffn(xp, xs, nw, wup_f32, wdn_f32, fw):
    n_p, D = xp.shape
    n_s = xs.shape[0]
    rows = FFN_ROWS
    assert n_p % rows == 0 and n_s % rows == 0
    p_steps, s_steps = n_p // rows, n_s // rows
    p_spec = pl.BlockSpec((rows, D), lambda i: (jnp.minimum(i, p_steps - 1), 0))
    s_spec = pl.BlockSpec((rows, D), lambda i: (jnp.maximum(i - p_steps, 0), 0),
                          pipeline_mode=pl.Buffered(1))
    return pl.pallas_call(
        functools.partial(_ffn_cast_kernel, p_steps),
        grid=(p_steps + s_steps,),
        in_specs=[
            p_spec,
            s_spec,
            _const_spec((1, D)),
            pl.BlockSpec(memory_space=pl.ANY),
            pl.BlockSpec(memory_space=pl.ANY),
            _const_spec((1, D)),
        ],
        out_specs=[p_spec, s_spec],
        out_shape=[jax.ShapeDtypeStruct((n_p, D), F32), jax.ShapeDtypeStruct((n_s, D), F32)],
        scratch_shapes=[
            pltpu.VMEM((D, D_FF), BF16),
            pltpu.VMEM((D_FF, D), BF16),
            pltpu.VMEM((CAST_BUFFERS, CAST_ROWS, D), F32),
            pltpu.SemaphoreType.DMA((CAST_BUFFERS, D_FF // D)),
        ],
        compiler_params=pltpu.CompilerParams(
            dimension_semantics=("arbitrary",), vmem_limit_bytes=FFN_VMEM_LIMIT),
        name="ffn",
    )(xp, xs, nw, wup_f32, wdn_f32, fw)


def _rope_tables(pos):
    half = DH // 2
    inv_freq = ROPE_BASE ** (-jnp.arange(half, dtype=F32) / half)
    ang = pos.astype(F32)[:, None] * inv_freq[None, :]
    cos, sin = jnp.cos(ang), jnp.sin(ang)
    return jnp.concatenate([cos, cos], axis=-1), jnp.concatenate([-sin, sin], axis=-1)


def kernel(x_prompt, x_sample, state_ret, state_hgrn, norm_mix_w, w_in, ret_norm_w, hgrn_norm_w,
           lb_logits, w_out, norm_ffn_w, w_up, w_down, final_norm_w):
    B, T, D = x_prompt.shape
    nb, seg, _ = x_sample.shape
    assert w_in.shape[0] == 1, "one layer"

    nw = norm_mix_w[0].reshape(1, D)
    rnw = ret_norm_w[0].reshape(1, DH)
    hnw = hgrn_norm_w[0].reshape(1, DH)
    lbl = lb_logits.astype(F32)
    fnw = norm_ffn_w[0].reshape(1, D)
    fw = final_norm_w.reshape(1, D)

    cos_p, sin_p = _rope_tables(jnp.arange(T, dtype=jnp.int32))
    cos_s, sin_s = _rope_tables(PAST_LEN + jnp.arange(seg, dtype=jnp.int32))
    cos_s = jnp.tile(cos_s, (SAMPLE_ROWS // seg, 1))
    sin_s = jnp.tile(sin_s, (SAMPLE_ROWS // seg, 1))

    xp1, sret_p, shg_p, win, wout = _mix_prompt(x_prompt, nw, w_in[0], cos_p, sin_p, rnw, hnw, lbl,
                                                w_out[0])
    xs1, sret_s, shg_s = _mix_sample(x_sample, nw, win, cos_s, sin_s, rnw, hnw, lbl, wout,
                                     state_ret[0], state_hgrn[0])

    y_p, y_s = _ffn(xp1.reshape(B * T, D), xs1, fnw, w_up[0], w_down[0], fw)
    y_p = y_p.reshape(B, T, D)
    y_s = y_s.reshape(nb, seg, D)
    return (y_p, y_s, sret_p[None], shg_p[None], sret_s[None], shg_s[None])
```

```python
import functools
import math

import jax
import jax.numpy as jnp
from jax import lax
from jax.experimental import pallas as pl
from jax.experimental.pallas import tpu as pltpu

D_MODEL = 1024
HEADS = 4
DH = 128
GROUP_W = HEADS * DH
IN_COLS = 8 * GROUP_W
D_FF = 4 * D_MODEL
ROPE_BASE = 10000.0
NORM_EPS = 1e-6
QK_SCALE = DH ** -0.5
PAST_LEN = 16384

PROMPT_CHUNK = 256
PROMPT_STEP_CHUNKS = 2
SAMPLE_ROWS = 256
SAMPLE_SUB_SEQS = 16
SAMPLE_STATE_BUFFERS = 2
FFN_ROWS = 1024
FFN_SUB_ROWS = 512
PROJ_PIECE = 512
CAST_BUFFERS = 4
CAST_ROWS = 512
VMEM_LIMIT = 56 * 1024 * 1024
SAMPLE_VMEM_LIMIT = 60 * 1024 * 1024

F32 = jnp.float32
BF16 = jnp.bfloat16


def _dot(a, b):
    return jnp.dot(a, b, preferred_element_type=F32)


def _dot_nt(a, b):
    return lax.dot_general(a, b, (((1,), (1,)), ((), ())), preferred_element_type=F32)


def _dot_tn(a, b):
    return lax.dot_general(a, b, (((0,), (0,)), ((), ())), preferred_element_type=F32)


def _rms(x, w):
    ms = jnp.mean(x * x, axis=-1, keepdims=True)
    return x * lax.rsqrt(ms + NORM_EPS) * w


def _sigmoid(x):
    return 0.5 * jnp.tanh(0.5 * x) + 0.5


def _silu(x):
    return x * _sigmoid(x)


def _lower_bound(lbl):
    mx = jnp.max(lbl, axis=0, keepdims=True)
    e = jnp.exp(lbl - mx)
    return e[0:1, :] / jnp.sum(e, axis=0, keepdims=True)


def _rotary(x, cosf, sinf):
    return x * cosf + pltpu.roll(x, DH // 2, 1) * sinf


def _split_halves(x, m):
    blk = 2 * m
    nb = x.shape[0] // blk
    lo = [x[b * blk:b * blk + m] for b in range(nb)]
    up = [x[b * blk + m:(b + 1) * blk] for b in range(nb)]
    cat = lambda ps: ps[0] if len(ps) == 1 else jnp.concatenate(ps, axis=0)
    return cat(lo), cat(up)


def _merge_halves(lo, up, m):
    nb = lo.shape[0] // m
    pieces = []
    for b in range(nb):
        pieces += [lo[b * m:(b + 1) * m], up[b * m:(b + 1) * m]]
    return jnp.concatenate(pieces, axis=0)


def _block_row_bcast(x, m, row):
    nb = x.shape[0] // m
    pieces = [jnp.broadcast_to(x[b * m + row:b * m + row + 1], (m, x.shape[1])) for b in range(nb)]
    return pieces[0] if nb == 1 else jnp.concatenate(pieces, axis=0)


def _hgrn_level_masks(rows, seg):
    masks = []
    for j in range(int(math.log2(seg))):
        if (1 << j) < 8:
            r = lax.broadcasted_iota(jnp.int32, (rows, rows), 0)
            s = lax.broadcasted_iota(jnp.int32, (rows, rows), 1)
            masks.append(jnp.logical_and(((r ^ s) >> j) == 1, r > s))
        else:
            r = lax.broadcasted_iota(jnp.int32, (rows // 2, rows), 0)
            s = lax.broadcasted_iota(jnp.int32, (rows // 2, rows), 1)
            masks.append((s >> j) == 2 * (r >> j))
    return masks


def _hgrn_intra(q, k, f, v, seg, t, masks, mid_hook=None):
    rows = q.shape[0]
    c = jnp.log2(f)
    scores = None
    for j in range(int(math.log2(seg))):
        m = 1 << j
        if m < 8:
            c3 = c.reshape(rows // 8, 8, DH)
            upper = (t & m) != 0
            if m == 1:
                w = jnp.where(upper, f, 1.0)
                tot = pltpu.roll(c3, 1, 1).reshape(rows, DH)
            else:
                if m == 2:
                    low4 = ((t & 7) < 4).reshape(rows // 8, 8, DH)
                    tot3 = jnp.where(low4, jnp.broadcast_to(c3[:, 1:2, :], c3.shape),
                                     jnp.broadcast_to(c3[:, 5:6, :], c3.shape))
                else:
                    tot3 = jnp.broadcast_to(c3[:, 3:4, :], c3.shape)
                tot = tot3.reshape(rows, DH)
                w = jnp.exp2(jnp.where(upper, c, tot - c))
            part = _dot_nt((q * w).astype(BF16), (k * w).astype(BF16))
            scores = jnp.where(masks[j], part, 0.0 if scores is None else scores)
            c = jnp.where(upper, c + tot, c)
        else:
            c_lo, c_up = _split_halves(c, m)
            tot = _block_row_bcast(c_lo, m, m - 1)
            q_up = _split_halves(q, m)[1]
            k_lo, k_up = _split_halves(k, m)
            qt = (q_up * jnp.exp2(c_up)).astype(BF16)
            kt = _merge_halves(k_lo * jnp.exp2(tot - c_lo), k_up, m).astype(BF16)
            s_lo, s_up = _split_halves(scores, m)
            s_up = jnp.where(masks[j], _dot_nt(qt, kt), s_up)
            scores = _merge_halves(s_lo, s_up, m)
            c = _merge_halves(c_lo, c_up + tot, m)
    if mid_hook is not None:
        mid_hook()
    o = _dot(scores.astype(BF16), v.astype(BF16))
    o = o + jnp.sum(q * k, axis=-1, keepdims=True) * v
    return o, c


def _ret_mask(rows, seg, log_gamma):
    r = lax.broadcasted_iota(jnp.int32, (rows, rows), 0)
    s = lax.broadcasted_iota(jnp.int32, (rows, rows), 1)
    valid = jnp.logical_and(((r ^ s) >> int(math.log2(seg))) == 0, r >= s)
    d = jnp.where(valid, r - s, 0).astype(F32)
    return jnp.where(valid, jnp.exp(d * log_gamma) * QK_SCALE, 0.0)


def _log_gamma(hd):
    return math.log(1.0 - 2.0 ** (-5.0 - hd))


def _head_out(o, w, gate):
    return o * lax.rsqrt(jnp.mean(o * o, axis=-1, keepdims=True) + NORM_EPS) * w * _silu(gate)


def _project(x_ref, nw_ref, win_ref, proj_ref):
    h = _rms(x_ref[...], nw_ref[...]).astype(BF16)
    for g in range(8):
        cols = slice(g * GROUP_W, (g + 1) * GROUP_W)
        proj_ref[:, cols] = _dot(h, win_ref[:, cols])


def _cast_weight(w_hbm, w_bf_ref, stage_ref, sem):
    nbuf, brows, bcols = stage_ref.shape
    rows, width = w_hbm.shape
    k = width // bcols
    crows = brows // k
    n = rows // crows

    def copies(i):
        return [pltpu.make_async_copy(
            w_hbm.at[pl.ds(i * crows, crows), pl.ds(c * bcols, bcols)],
            stage_ref.at[i % nbuf, pl.ds(c * crows, crows), :],
            sem.at[i % nbuf, c]) for c in range(k)]

    for i in range(min(nbuf - 1, n)):
        for cp in copies(i):
            cp.start()
    for i in range(n):
        if i + nbuf - 1 < n:
            for cp in copies(i + nbuf - 1):
                cp.start()
        for c, cp in enumerate(copies(i)):
            cp.wait()
            w_bf_ref[pl.ds(i * crows, crows), pl.ds(c * bcols, bcols)] = (
                stage_ref[i % nbuf, c * crows:(c + 1) * crows, :].astype(BF16))


def _cols(group, hd):
    return slice(group * GROUP_W + hd * DH, group * GROUP_W + (hd + 1) * DH)


def _mix_prompt_kernel(nc, *refs):
    CH = PROMPT_STEP_CHUNKS
    L = PROMPT_CHUNK
    x0_ref = refs[0]
    xn_refs = refs[1:1 + CH]
    (xc_ref, nw_ref, win_hbm, cos_ref, sin_ref, rnw_ref, hnw_ref, lbl_ref, wout_hbm,
     wup_f32_ref, wdn_f32_ref,
     x1_ref, sret_ref, shg_ref, win_ref, wout_ref, wup_bf_ref, wdn_bf_ref,
     proj_ref, o_ref, srt_ref, sht_ref, dm_ref, qd_ref, kd_ref, stage_ref, sem) = refs[1 + CH:]
    g = pl.program_id(0)
    ci0 = (g * CH) % nc
    t = lax.broadcasted_iota(jnp.int32, (L, DH), 0)

    @pl.when(g == 0)
    def _():
        _cast_weight(win_hbm, win_ref, stage_ref, sem)
        _cast_weight(wout_hbm, wout_ref, stage_ref, sem)
        tf = t.astype(F32)
        for hd in range(HEADS):
            lg = _log_gamma(hd)
            dm_ref[hd] = _ret_mask(L, L, lg)
            qd_ref[hd] = jnp.exp((tf + 1.0) * lg)
            kd_ref[hd] = jnp.exp((L - 1.0 - tf) * lg) * QK_SCALE
        _project(x0_ref, nw_ref, win_ref, proj_ref.at[0])

    @pl.when(ci0 == 0)
    def _():
        srt_ref[...] = jnp.zeros_like(srt_ref)
        sht_ref[...] = jnp.zeros_like(sht_ref)

    wup_bf_ref[...] = wup_f32_ref[...].astype(BF16)
    wdn_bf_ref[...] = wdn_f32_ref[...].astype(BF16)

    lb = _lower_bound(lbl_ref[...])
    masks = _hgrn_level_masks(L, L)

    def chunk(pr, pw, xn_ref, rows):
        h_next = _rms(xn_ref[...], nw_ref[...]).astype(BF16)

        def project_piece(pi):
            cols = slice(pi * PROJ_PIECE, (pi + 1) * PROJ_PIECE)
            pw[:, cols] = _dot(h_next, win_ref[:, cols])

        cosf = cos_ref[rows, :]
        sinf = sin_ref[rows, :]

        for hd in range(HEADS):
            q = _rotary(pr[:, _cols(0, hd)], cosf, sinf)
            k = _rotary(pr[:, _cols(1, hd)], cosf, sinf)
            v = pr[:, _cols(2, hd)].astype(BF16)
            scores = _dot_nt(q.astype(BF16), k.astype(BF16)) * dm_ref[hd]
            project_piece(hd)
            o = _dot(scores.astype(BF16), v)
            st = srt_ref[hd]
            o = o + _dot_nt((q * qd_ref[hd]).astype(BF16), st.astype(BF16))
            kh = (k * kd_ref[hd]).astype(BF16)
            srt_ref[hd] = st * math.exp(L * _log_gamma(hd)) + _dot_tn(v, kh)
            o_ref[:, _cols(0, hd)] = _head_out(o, rnw_ref[...], pr[:, _cols(3, hd)]).astype(BF16)

        for hd in range(HEADS):
            q = _silu(pr[:, _cols(4, hd)]) * QK_SCALE
            lbh = lb[:, hd * DH:(hd + 1) * DH]
            f = lbh + (1.0 - lbh) * _sigmoid(pr[:, _cols(5, hd)])
            k = 1.0 - f
            v = pr[:, _cols(6, hd)]
            o, b = _hgrn_intra(q, k, f, v, L, t, masks,
                               functools.partial(project_piece, HEADS + hd))
            blast = b[L - 1:L, :]
            st = sht_ref[hd]
            o = o + _dot_nt((q * jnp.exp2(b)).astype(BF16), st.astype(BF16))
            kh = (k * jnp.exp2(blast - b)).astype(BF16)
            sht_ref[hd] = st * jnp.exp2(blast) + _dot_tn(v.astype(BF16), kh)
            o_ref[:, _cols(1, hd)] = _head_out(o, hnw_ref[...], pr[:, _cols(7, hd)]).astype(BF16)

        x1_ref[rows, :] = xc_ref[rows, :] + _dot(o_ref[...], wout_ref[...])

    for sub in range(CH):
        chunk(proj_ref.at[sub % 2], proj_ref.at[(sub + 1) % 2], xn_refs[sub],
              slice(sub * L, (sub + 1) * L))

    @pl.when(ci0 == nc - CH)
    def _():
        for hd in range(HEADS):
            sret_ref[hd] = srt_ref[hd].T
            shg_ref[hd] = sht_ref[hd].T


def _const_spec(shape):
    return pl.BlockSpec(shape, lambda *_: (0,) * len(shape), pipeline_mode=pl.Buffered(1))


def _mix_prompt(x, nw, win_f32, cosf, sinf, rnw, hnw, lbl, wout_f32, wup_f32, wdn_f32):
    B, T, D = x.shape
    L = PROMPT_CHUNK
    CH = PROMPT_STEP_CHUNKS
    nc = T // L
    n = B * nc
    spb = nc // CH
    assert nc % CH == 0 and CH % 2 == 0

    def next_chunk_spec(i):
        idx = lambda g: jnp.minimum(CH * g + i + 1, n - 1)
        return pl.BlockSpec((None, L, D), lambda g: (idx(g) // nc, idx(g) % nc, 0))

    state = jax.ShapeDtypeStruct((B, HEADS, DH, DH), F32)
    state_spec = pl.BlockSpec((None, HEADS, DH, DH), lambda g: (g // spb, 0, 0, 0))
    rows_spec = pl.BlockSpec((None, CH * L, D), lambda g: (g // spb, g % spb, 0))
    table_spec = pl.BlockSpec((CH * L, DH), lambda g: (g % spb, 0))
    steps = n // CH
    wup_spec = pl.BlockSpec((D // steps, D_FF), lambda g: (g, 0))
    wdn_spec = pl.BlockSpec((D_FF // steps, D), lambda g: (g, 0))
    return pl.pallas_call(
        functools.partial(_mix_prompt_kernel, nc),
        grid=(steps,),
        in_specs=[
            pl.BlockSpec((None, L, D), lambda g: (0, 0, 0), pipeline_mode=pl.Buffered(1)),
            *[next_chunk_spec(i) for i in range(CH)],
            rows_spec,
            _const_spec((1, D)),
            pl.BlockSpec(memory_space=pl.ANY),
            table_spec,
            table_spec,
            _const_spec((1, DH)),
            _const_spec((1, DH)),
            _const_spec(lbl.shape),
            pl.BlockSpec(memory_space=pl.ANY),
            wup_spec,
            wdn_spec,
        ],
        out_specs=[rows_spec, state_spec, state_spec,
                   pl.BlockSpec((D, IN_COLS), lambda g: (0, 0), pipeline_mode=pl.Buffered(1)),
                   pl.BlockSpec((D, D), lambda g: (0, 0), pipeline_mode=pl.Buffered(1)),
                   wup_spec, wdn_spec],
        out_shape=[jax.ShapeDtypeStruct((B, T, D), F32), state, state,
                   jax.ShapeDtypeStruct((D, IN_COLS), BF16), jax.ShapeDtypeStruct((D, D), BF16),
                   jax.ShapeDtypeStruct((D, D_FF), BF16), jax.ShapeDtypeStruct((D_FF, D), BF16)],
        scratch_shapes=[
            pltpu.VMEM((2, L, IN_COLS), F32),
            pltpu.VMEM((L, D), BF16),
            pltpu.VMEM((HEADS, DH, DH), F32),
            pltpu.VMEM((HEADS, DH, DH), F32),
            pltpu.VMEM((HEADS, L, L), F32),
            pltpu.VMEM((HEADS, L, DH), F32),
            pltpu.VMEM((HEADS, L, DH), F32),
            pltpu.VMEM((CAST_BUFFERS, CAST_ROWS, D), F32),
            pltpu.SemaphoreType.DMA((CAST_BUFFERS, IN_COLS // D)),
        ],
        compiler_params=pltpu.CompilerParams(
            dimension_semantics=("arbitrary",), vmem_limit_bytes=VMEM_LIMIT),
        name="mix_prompt",
    )(x, *([x] * CH), x, nw, win_f32, cosf, sinf, rnw, hnw, lbl, wout_f32, wup_f32, wdn_f32)


def _mix_sample_kernel(seg, x_ref, nw_ref, win_ref, cos_ref, sin_ref, rnw_ref, hnw_ref, lbl_ref,
                       wout_ref, sret_in_ref, shg_in_ref,
                       x1_ref, sret_ref, shg_ref,
                       proj_ref, o_ref, qh_ref, kh_ref, dec_ref, ob_ref):
    rows = x_ref.shape[0]
    nsub = sret_in_ref.shape[0]
    j = pl.program_id(1)

    @pl.when(j == 0)
    def _():
        _project(x_ref, nw_ref, win_ref, proj_ref)
        t = lax.broadcasted_iota(jnp.int32, (rows, DH), 0)
        tl = (t & (seg - 1)).astype(F32)
        cosf = cos_ref[...]
        sinf = sin_ref[...]
        lb = _lower_bound(lbl_ref[...])
        masks = _hgrn_level_masks(rows, seg)
        for hd in range(HEADS):
            lg = _log_gamma(hd)
            q = _rotary(proj_ref[:, _cols(0, hd)], cosf, sinf)
            k = _rotary(proj_ref[:, _cols(1, hd)], cosf, sinf)
            scores = _dot_nt(q.astype(BF16), k.astype(BF16)) * _ret_mask(rows, seg, lg)
            o_ref[:, _cols(0, hd)] = _dot(scores.astype(BF16),
                                          proj_ref[:, _cols(2, hd)].astype(BF16))
            qh_ref[:, _cols(0, hd)] = q * jnp.exp((tl + 1.0) * lg)
            kh_ref[:, _cols(0, hd)] = k * (jnp.exp((seg - 1.0 - tl) * lg) * QK_SCALE)
        for hd in range(HEADS):
            q = _silu(proj_ref[:, _cols(4, hd)]) * QK_SCALE
            lbh = lb[:, hd * DH:(hd + 1) * DH]
            f = lbh + (1.0 - lbh) * _sigmoid(proj_ref[:, _cols(5, hd)])
            k = 1.0 - f
            o, b = _hgrn_intra(q, k, f, proj_ref[:, _cols(6, hd)], seg, t, masks)
            blast = _block_row_bcast(b, seg, seg - 1)
            o_ref[:, _cols(1, hd)] = o
            qh_ref[:, _cols(1, hd)] = q * jnp.exp2(b)
            kh_ref[:, _cols(1, hd)] = k * jnp.exp2(blast - b)
            dec_ref[:, hd * DH:(hd + 1) * DH] = jnp.exp2(blast)

    def seq_body(s, carry):
        r = pl.ds(pl.multiple_of((j * nsub + s) * seg, seg), seg)
        qh = qh_ref[r, :].astype(BF16)
        kh = kh_ref[r, :].astype(BF16)
        v_ret = proj_ref[r, 2 * GROUP_W:3 * GROUP_W].astype(BF16)
        v_hg = proj_ref[r, 6 * GROUP_W:7 * GROUP_W].astype(BF16)
        dec = dec_ref[r, :][0:1, :]
        o_parts, new_ret, new_hg = [], [], []
        for hd in range(HEADS):
            st = sret_in_ref[s, hd]
            c = slice(hd * DH, (hd + 1) * DH)
            o_parts.append(_dot(qh[:, c], st.astype(BF16)))
            new_ret.append(st * math.exp(seg * _log_gamma(hd)) + _dot_tn(kh[:, c], v_ret[:, c]))
        for hd in range(HEADS):
            st = shg_in_ref[s, hd]
            c = slice(hd * DH, (hd + 1) * DH)
            c1 = slice(GROUP_W + hd * DH, GROUP_W + (hd + 1) * DH)
            o_parts.append(_dot(qh[:, c1], st.astype(BF16)))
            dcol = jnp.broadcast_to(dec[:, c], (DH, DH)).T
            new_hg.append(st * dcol + _dot_tn(kh[:, c1], v_hg[:, c]))
        o_ref[r, :] = o_ref[r, :] + jnp.concatenate(o_parts, axis=1)
        for hd in range(HEADS):
            sret_ref[s, hd] = new_ret[hd]
            shg_ref[s, hd] = new_hg[hd]
        return carry

    lax.fori_loop(0, nsub, seq_body, 0, unroll=2)

    @pl.when(j == pl.num_programs(1) - 1)
    def _():
        for hd in range(HEADS):
            c0 = _cols(0, hd)
            ob_ref[:, c0] = _head_out(o_ref[:, c0], rnw_ref[...],
                                      proj_ref[:, _cols(3, hd)]).astype(BF16)
            c1 = _cols(1, hd)
            ob_ref[:, c1] = _head_out(o_ref[:, c1], hnw_ref[...],
                                      proj_ref[:, _cols(7, hd)]).astype(BF16)
        x1_ref[...] = x_ref[...] + _dot(ob_ref[...], wout_ref[...])


def _mix_sample(x, nw, win, cosf, sinf, rnw, hnw, lbl, wout, sret, shg):
    nb, seg, D = x.shape
    rows = SAMPLE_ROWS
    nsub = SAMPLE_SUB_SEQS
    nj = rows // seg // nsub
    x2 = x.reshape(nb * seg, D)
    state = jax.ShapeDtypeStruct((nb, HEADS, DH, DH), F32)
    state_map = lambda i, j: (i * nj + j, 0, 0, 0)
    state_in_spec = pl.BlockSpec((nsub, HEADS, DH, DH), state_map,
                                 pipeline_mode=pl.Buffered(SAMPLE_STATE_BUFFERS))
    state_out_spec = pl.BlockSpec((nsub, HEADS, DH, DH), state_map)
    x1, sret_new, shg_new = pl.pallas_call(
        functools.partial(_mix_sample_kernel, seg),
        grid=(nb * seg // rows, nj),
        in_specs=[
            pl.BlockSpec((rows, D), lambda i, j: (i, 0)),
            _const_spec((1, D)),
            _const_spec((D, IN_COLS)),
            _const_spec((rows, DH)),
            _const_spec((rows, DH)),
            _const_spec((1, DH)),
            _const_spec((1, DH)),
            _const_spec(lbl.shape),
            _const_spec((D, D)),
            state_in_spec,
            state_in_spec,
        ],
        out_specs=[pl.BlockSpec((rows, D), lambda i, j: (i, 0)), state_out_spec, state_out_spec],
        out_shape=[jax.ShapeDtypeStruct((nb * seg, D), F32), state, state],
        scratch_shapes=[
            pltpu.VMEM((rows, IN_COLS), F32),
            pltpu.VMEM((rows, D), F32),
            pltpu.VMEM((rows, D), F32),
            pltpu.VMEM((rows, D), F32),
            pltpu.VMEM((rows, GROUP_W), F32),
            pltpu.VMEM((rows, D), BF16),
        ],
        compiler_params=pltpu.CompilerParams(
            dimension_semantics=("arbitrary", "arbitrary"), vmem_limit_bytes=SAMPLE_VMEM_LIMIT),
        name="mix_sample",
    )(x2, nw, win, cosf, sinf, rnw, hnw, lbl, wout, sret, shg)
    return x1, sret_new, shg_new


def _ffn_kernel(x_ref, nw_ref, wup_ref, wdn_ref, fw_ref, y_ref):
    subs = [slice(r * FFN_SUB_ROWS, (r + 1) * FFN_SUB_ROWS)
            for r in range(x_ref.shape[0] // FFN_SUB_ROWS)]
    hs = [_rms(x_ref[rs, :], nw_ref[...]).astype(BF16) for rs in subs]
    accs = [x_ref[rs, :] for rs in subs]
    step = D_MODEL
    for g in range(D_FF // step):
        us = [jnp.maximum(_dot(h, wup_ref[:, g * step:(g + 1) * step]), 0.0) for h in hs]
        accs = [acc + _dot((u * u).astype(BF16), wdn_ref[g * step:(g + 1) * step, :])
                for acc, u in zip(accs, us)]
    for rs, acc in zip(subs, accs):
        y_ref[rs, :] = _rms(acc, fw_ref[...])


def _ffn_rows_kernel(n_prompt_steps, xp_ref, xs_ref, nw_ref, wup_ref, wdn_ref, fw_ref,
                     yp_ref, ys_ref):
    i = pl.program_id(0)

    @pl.when(i < n_prompt_steps)
    def _():
        _ffn_kernel(xp_ref, nw_ref, wup_ref, wdn_ref, fw_ref, yp_ref)

    @pl.when(i >= n_prompt_steps)
    def _():
        _ffn_kernel(xs_ref, nw_ref, wup_ref, wdn_ref, fw_ref, ys_ref)


def _ffn(xp, xs, nw, wup, wdn, fw):
    n_p, D = xp.shape
    n_s = xs.shape[0]
    rows = FFN_ROWS
    assert n_p % rows == 0 and n_s % rows == 0
    p_steps, s_steps = n_p // rows, n_s // rows
    p_spec = pl.BlockSpec((rows, D), lambda i: (jnp.minimum(i, p_steps - 1), 0))
    s_spec = pl.BlockSpec((rows, D), lambda i: (jnp.maximum(i - p_steps, 0), 0),
                          pipeline_mode=pl.Buffered(1))
    return pl.pallas_call(
        functools.partial(_ffn_rows_kernel, p_steps),
        grid=(p_steps + s_steps,),
        in_specs=[
            p_spec,
            s_spec,
            _const_spec((1, D)),
            _const_spec((D, D_FF)),
            _const_spec((D_FF, D)),
            _const_spec((1, D)),
        ],
        out_specs=[p_spec, s_spec],
        out_shape=[jax.ShapeDtypeStruct((n_p, D), F32), jax.ShapeDtypeStruct((n_s, D), F32)],
        compiler_params=pltpu.CompilerParams(
            dimension_semantics=("arbitrary",), vmem_limit_bytes=VMEM_LIMIT),
        name="ffn",
    )(xp, xs, nw, wup, wdn, fw)


def _rope_tables(pos):
    half = DH // 2
    inv_freq = ROPE_BASE ** (-jnp.arange(half, dtype=F32) / half)
    ang = pos.astype(F32)[:, None] * inv_freq[None, :]
    cos, sin = jnp.cos(ang), jnp.sin(ang)
    return jnp.concatenate([cos, cos], axis=-1), jnp.concatenate([-sin, sin], axis=-1)


def kernel(x_prompt, x_sample, state_ret, state_hgrn, norm_mix_w, w_in, ret_norm_w, hgrn_norm_w,
           lb_logits, w_out, norm_ffn_w, w_up, w_down, final_norm_w):
    B, T, D = x_prompt.shape
    nb, seg, _ = x_sample.shape
    assert w_in.shape[0] == 1, "one layer"

    nw = norm_mix_w[0].reshape(1, D)
    rnw = ret_norm_w[0].reshape(1, DH)
    hnw = hgrn_norm_w[0].reshape(1, DH)
    lbl = lb_logits.astype(F32)
    fnw = norm_ffn_w[0].reshape(1, D)
    fw = final_norm_w.reshape(1, D)

    cos_p, sin_p = _rope_tables(jnp.arange(T, dtype=jnp.int32))
    cos_s, sin_s = _rope_tables(PAST_LEN + jnp.arange(seg, dtype=jnp.int32))
    cos_s = jnp.tile(cos_s, (SAMPLE_ROWS // seg, 1))
    sin_s = jnp.tile(sin_s, (SAMPLE_ROWS // seg, 1))

    xp1, sret_p, shg_p, win, wout, wup, wdn = _mix_prompt(
        x_prompt, nw, w_in[0], cos_p, sin_p, rnw, hnw, lbl, w_out[0], w_up[0], w_down[0])
    xs1, sret_s, shg_s = _mix_sample(x_sample, nw, win, cos_s, sin_s, rnw, hnw, lbl, wout,
                                     state_ret[0], state_hgrn[0])

    y_p, y_s = _ffn(xp1.reshape(B * T, D), xs1, fnw, wup, wdn, fw)
    y_p = y_p.reshape(B, T, D)
    y_s = y_s.reshape(nb, seg, D)
    return (y_p, y_s, sret_p[None], shg_p[None], sret_s[None], shg_s[None])
```

```python
import functools
import math

import jax
import jax.numpy as jnp
from jax import lax
from jax.experimental import pallas as pl
from jax.experimental.pallas import tpu as pltpu

D_MODEL = 1024
HEADS = 4
DH = 128
GROUP_W = HEADS * DH
IN_COLS = 8 * GROUP_W
D_FF = 4 * D_MODEL
ROPE_BASE = 10000.0
NORM_EPS = 1e-6
QK_SCALE = DH ** -0.5
PAST_LEN = 16384

PROMPT_CHUNK = 256
PROMPT_STEP_CHUNKS = 2
SAMPLE_ROWS = 256
SAMPLE_SUB_SEQS = 16
FFN_ROWS = 1024
FFN_SUB_ROWS = 512
PROJ_PIECE = 512
CAST_BUFFERS = 4
CAST_ROWS = 512
VMEM_LIMIT = 56 * 1024 * 1024
SAMPLE_VMEM_LIMIT = 60 * 1024 * 1024

F32 = jnp.float32
BF16 = jnp.bfloat16


def _dot(a, b):
    return jnp.dot(a, b, preferred_element_type=F32)


def _dot_nt(a, b):
    return lax.dot_general(a, b, (((1,), (1,)), ((), ())), preferred_element_type=F32)


def _dot_tn(a, b):
    return lax.dot_general(a, b, (((0,), (0,)), ((), ())), preferred_element_type=F32)


def _rms(x, w):
    ms = jnp.mean(x * x, axis=-1, keepdims=True)
    return x * lax.rsqrt(ms + NORM_EPS) * w


def _sigmoid(x):
    return 0.5 * jnp.tanh(0.5 * x) + 0.5


def _silu(x):
    return x * _sigmoid(x)


def _lower_bound(lbl):
    mx = jnp.max(lbl, axis=0, keepdims=True)
    e = jnp.exp(lbl - mx)
    return e[0:1, :] / jnp.sum(e, axis=0, keepdims=True)


def _rotary(x, cosf, sinf):
    return x * cosf + pltpu.roll(x, DH // 2, 1) * sinf


def _split_halves(x, m):
    blk = 2 * m
    nb = x.shape[0] // blk
    lo = [x[b * blk:b * blk + m] for b in range(nb)]
    up = [x[b * blk + m:(b + 1) * blk] for b in range(nb)]
    cat = lambda ps: ps[0] if len(ps) == 1 else jnp.concatenate(ps, axis=0)
    return cat(lo), cat(up)


def _merge_halves(lo, up, m):
    nb = lo.shape[0] // m
    pieces = []
    for b in range(nb):
        pieces += [lo[b * m:(b + 1) * m], up[b * m:(b + 1) * m]]
    return jnp.concatenate(pieces, axis=0)


def _block_row_bcast(x, m, row):
    nb = x.shape[0] // m
    pieces = [jnp.broadcast_to(x[b * m + row:b * m + row + 1], (m, x.shape[1])) for b in range(nb)]
    return pieces[0] if nb == 1 else jnp.concatenate(pieces, axis=0)


def _hgrn_level_masks(rows, seg):
    masks = []
    for j in range(int(math.log2(seg))):
        if (1 << j) < 8:
            r = lax.broadcasted_iota(jnp.int32, (rows, rows), 0)
            s = lax.broadcasted_iota(jnp.int32, (rows, rows), 1)
            masks.append(jnp.logical_and(((r ^ s) >> j) == 1, r > s))
        else:
            r = lax.broadcasted_iota(jnp.int32, (rows // 2, rows), 0)
            s = lax.broadcasted_iota(jnp.int32, (rows // 2, rows), 1)
            masks.append((s >> j) == 2 * (r >> j))
    return masks


def _hgrn_intra(q, k, f, v, seg, t, masks, mid_hook=None):
    rows = q.shape[0]
    c = jnp.log2(f)
    scores = None
    for j in range(int(math.log2(seg))):
        m = 1 << j
        if m < 8:
            c3 = c.reshape(rows // 8, 8, DH)
            upper = (t & m) != 0
            if m == 1:
                w = jnp.where(upper, f, 1.0)
                tot = pltpu.roll(c3, 1, 1).reshape(rows, DH)
            else:
                if m == 2:
                    low4 = ((t & 7) < 4).reshape(rows // 8, 8, DH)
                    tot3 = jnp.where(low4, jnp.broadcast_to(c3[:, 1:2, :], c3.shape),
                                     jnp.broadcast_to(c3[:, 5:6, :], c3.shape))
                else:
                    tot3 = jnp.broadcast_to(c3[:, 3:4, :], c3.shape)
                tot = tot3.reshape(rows, DH)
                w = jnp.exp2(jnp.where(upper, c, tot - c))
            part = _dot_nt((q * w).astype(BF16), (k * w).astype(BF16))
            scores = jnp.where(masks[j], part, 0.0 if scores is None else scores)
            c = jnp.where(upper, c + tot, c)
        else:
            c_lo, c_up = _split_halves(c, m)
            tot = _block_row_bcast(c_lo, m, m - 1)
            q_up = _split_halves(q, m)[1]
            k_lo, k_up = _split_halves(k, m)
            qt = (q_up * jnp.exp2(c_up)).astype(BF16)
            kt = _merge_halves(k_lo * jnp.exp2(tot - c_lo), k_up, m).astype(BF16)
            s_lo, s_up = _split_halves(scores, m)
            s_up = jnp.where(masks[j], _dot_nt(qt, kt), s_up)
            scores = _merge_halves(s_lo, s_up, m)
            c = _merge_halves(c_lo, c_up + tot, m)
    if mid_hook is not None:
        mid_hook()
    o = _dot(scores.astype(BF16), v.astype(BF16))
    o = o + jnp.sum(q * k, axis=-1, keepdims=True) * v
    return o, c


def _ret_mask(rows, seg, log_gamma):
    r = lax.broadcasted_iota(jnp.int32, (rows, rows), 0)
    s = lax.broadcasted_iota(jnp.int32, (rows, rows), 1)
    valid = jnp.logical_and(((r ^ s) >> int(math.log2(seg))) == 0, r >= s)
    d = jnp.where(valid, r - s, 0).astype(F32)
    return jnp.where(valid, jnp.exp(d * log_gamma) * QK_SCALE, 0.0)


def _log_gamma(hd):
    return math.log(1.0 - 2.0 ** (-5.0 - hd))


def _head_out(o, w, gate):
    return o * lax.rsqrt(jnp.mean(o * o, axis=-1, keepdims=True) + NORM_EPS) * w * _silu(gate)


def _project(x_ref, nw_ref, win_ref, proj_ref):
    h = _rms(x_ref[...], nw_ref[...]).astype(BF16)
    for g in range(8):
        cols = slice(g * GROUP_W, (g + 1) * GROUP_W)
        proj_ref[:, cols] = _dot(h, win_ref[:, cols])


def _cast_weight(w_hbm, w_bf_ref, stage_ref, sem):
    nbuf, brows, bcols = stage_ref.shape
    rows, width = w_hbm.shape
    k = width // bcols
    crows = brows // k
    n = rows // crows

    def copies(i):
        return [pltpu.make_async_copy(
            w_hbm.at[pl.ds(i * crows, crows), pl.ds(c * bcols, bcols)],
            stage_ref.at[i % nbuf, pl.ds(c * crows, crows), :],
            sem.at[i % nbuf, c]) for c in range(k)]

    for i in range(min(nbuf - 1, n)):
        for cp in copies(i):
            cp.start()
    for i in range(n):
        if i + nbuf - 1 < n:
            for cp in copies(i + nbuf - 1):
                cp.start()
        for c, cp in enumerate(copies(i)):
            cp.wait()
            w_bf_ref[pl.ds(i * crows, crows), pl.ds(c * bcols, bcols)] = (
                stage_ref[i % nbuf, c * crows:(c + 1) * crows, :].astype(BF16))


def _cols(group, hd):
    return slice(group * GROUP_W + hd * DH, group * GROUP_W + (hd + 1) * DH)


def _mix_prompt_kernel(nc, *refs):
    CH = PROMPT_STEP_CHUNKS
    L = PROMPT_CHUNK
    x0_ref = refs[0]
    xn_refs = refs[1:1 + CH]
    (xc_ref, nw_ref, win_hbm, cos_ref, sin_ref, rnw_ref, hnw_ref, lbl_ref, wout_hbm,
     wup_f32_ref, wdn_f32_ref,
     x1_ref, sret_ref, shg_ref, win_ref, wout_ref, wup_bf_ref, wdn_bf_ref,
     proj_ref, o_ref, srt_ref, sht_ref, dm_ref, qd_ref, kd_ref, stage_ref, sem) = refs[1 + CH:]
    g = pl.program_id(0)
    ci0 = (g * CH) % nc
    t = lax.broadcasted_iota(jnp.int32, (L, DH), 0)

    @pl.when(g == 0)
    def _():
        _cast_weight(win_hbm, win_ref, stage_ref, sem)
        _cast_weight(wout_hbm, wout_ref, stage_ref, sem)
        tf = t.astype(F32)
        for hd in range(HEADS):
            lg = _log_gamma(hd)
            dm_ref[hd] = _ret_mask(L, L, lg)
            qd_ref[hd] = jnp.exp((tf + 1.0) * lg)
            kd_ref[hd] = jnp.exp((L - 1.0 - tf) * lg) * QK_SCALE
        _project(x0_ref, nw_ref, win_ref, proj_ref.at[0])

    @pl.when(ci0 == 0)
    def _():
        srt_ref[...] = jnp.zeros_like(srt_ref)
        sht_ref[...] = jnp.zeros_like(sht_ref)

    wup_bf_ref[...] = wup_f32_ref[...].astype(BF16)
    wdn_bf_ref[...] = wdn_f32_ref[...].astype(BF16)

    lb = _lower_bound(lbl_ref[...])
    masks = _hgrn_level_masks(L, L)

    def chunk(pr, pw, xn_ref, rows):
        h_next = _rms(xn_ref[...], nw_ref[...]).astype(BF16)

        def project_piece(pi):
            cols = slice(pi * PROJ_PIECE, (pi + 1) * PROJ_PIECE)
            pw[:, cols] = _dot(h_next, win_ref[:, cols])

        cosf = cos_ref[rows, :]
        sinf = sin_ref[rows, :]

        for hd in range(HEADS):
            q = _rotary(pr[:, _cols(0, hd)], cosf, sinf)
            k = _rotary(pr[:, _cols(1, hd)], cosf, sinf)
            v = pr[:, _cols(2, hd)].astype(BF16)
            scores = _dot_nt(q.astype(BF16), k.astype(BF16)) * dm_ref[hd]
            project_piece(hd)
            o = _dot(scores.astype(BF16), v)
            st = srt_ref[hd]
            o = o + _dot_nt((q * qd_ref[hd]).astype(BF16), st.astype(BF16))
            kh = (k * kd_ref[hd]).astype(BF16)
            srt_ref[hd] = st * math.exp(L * _log_gamma(hd)) + _dot_tn(v, kh)
            o_ref[:, _cols(0, hd)] = _head_out(o, rnw_ref[...], pr[:, _cols(3, hd)]).astype(BF16)

        for hd in range(HEADS):
            q = _silu(pr[:, _cols(4, hd)]) * QK_SCALE
            lbh = lb[:, hd * DH:(hd + 1) * DH]
            f = lbh + (1.0 - lbh) * _sigmoid(pr[:, _cols(5, hd)])
            k = 1.0 - f
            v = pr[:, _cols(6, hd)]
            o, b = _hgrn_intra(q, k, f, v, L, t, masks,
                               functools.partial(project_piece, HEADS + hd))
            blast = b[L - 1:L, :]
            st = sht_ref[hd]
            o = o + _dot_nt((q * jnp.exp2(b)).astype(BF16), st.astype(BF16))
            kh = (k * jnp.exp2(blast - b)).astype(BF16)
            sht_ref[hd] = st * jnp.exp2(blast) + _dot_tn(v.astype(BF16), kh)
            o_ref[:, _cols(1, hd)] = _head_out(o, hnw_ref[...], pr[:, _cols(7, hd)]).astype(BF16)

        x1_ref[rows, :] = xc_ref[rows, :] + _dot(o_ref[...], wout_ref[...])

    for sub in range(CH):
        chunk(proj_ref.at[sub % 2], proj_ref.at[(sub + 1) % 2], xn_refs[sub],
              slice(sub * L, (sub + 1) * L))

    @pl.when(ci0 == nc - CH)
    def _():
        for hd in range(HEADS):
            sret_ref[hd] = srt_ref[hd].T
            shg_ref[hd] = sht_ref[hd].T


def _const_spec(shape):
    return pl.BlockSpec(shape, lambda *_: (0,) * len(shape), pipeline_mode=pl.Buffered(1))


def _mix_prompt(x, nw, win_f32, cosf, sinf, rnw, hnw, lbl, wout_f32, wup_f32, wdn_f32):
    B, T, D = x.shape
    L = PROMPT_CHUNK
    CH = PROMPT_STEP_CHUNKS
    nc = T // L
    n = B * nc
    spb = nc // CH
    assert nc % CH == 0 and CH % 2 == 0

    def next_chunk_spec(i):
        idx = lambda g: jnp.minimum(CH * g + i + 1, n - 1)
        return pl.BlockSpec((None, L, D), lambda g: (idx(g) // nc, idx(g) % nc, 0))

    state = jax.ShapeDtypeStruct((B, HEADS, DH, DH), F32)
    state_spec = pl.BlockSpec((None, HEADS, DH, DH), lambda g: (g // spb, 0, 0, 0))
    rows_spec = pl.BlockSpec((None, CH * L, D), lambda g: (g // spb, g % spb, 0))
    table_spec = pl.BlockSpec((CH * L, DH), lambda g: (g % spb, 0))
    steps = n // CH
    wup_spec = pl.BlockSpec((D // steps, D_FF), lambda g: (g, 0))
    wdn_spec = pl.BlockSpec((D_FF // steps, D), lambda g: (g, 0))
    return pl.pallas_call(
        functools.partial(_mix_prompt_kernel, nc),
        grid=(steps,),
        in_specs=[
            pl.BlockSpec((None, L, D), lambda g: (0, 0, 0), pipeline_mode=pl.Buffered(1)),
            *[next_chunk_spec(i) for i in range(CH)],
            rows_spec,
            _const_spec((1, D)),
            pl.BlockSpec(memory_space=pl.ANY),
            table_spec,
            table_spec,
            _const_spec((1, DH)),
            _const_spec((1, DH)),
            _const_spec(lbl.shape),
            pl.BlockSpec(memory_space=pl.ANY),
            wup_spec,
            wdn_spec,
        ],
        out_specs=[rows_spec, state_spec, state_spec,
                   pl.BlockSpec((D, IN_COLS), lambda g: (0, 0), pipeline_mode=pl.Buffered(1)),
                   pl.BlockSpec((D, D), lambda g: (0, 0), pipeline_mode=pl.Buffered(1)),
                   wup_spec, wdn_spec],
        out_shape=[jax.ShapeDtypeStruct((B, T, D), F32), state, state,
                   jax.ShapeDtypeStruct((D, IN_COLS), BF16), jax.ShapeDtypeStruct((D, D), BF16),
                   jax.ShapeDtypeStruct((D, D_FF), BF16), jax.ShapeDtypeStruct((D_FF, D), BF16)],
        scratch_shapes=[
            pltpu.VMEM((2, L, IN_COLS), F32),
            pltpu.VMEM((L, D), BF16),
            pltpu.VMEM((HEADS, DH, DH), F32),
            pltpu.VMEM((HEADS, DH, DH), F32),
            pltpu.VMEM((HEADS, L, L), F32),
            pltpu.VMEM((HEADS, L, DH), F32),
            pltpu.VMEM((HEADS, L, DH), F32),
            pltpu.VMEM((CAST_BUFFERS, CAST_ROWS, D), F32),
            pltpu.SemaphoreType.DMA((CAST_BUFFERS, IN_COLS // D)),
        ],
        compiler_params=pltpu.CompilerParams(
            dimension_semantics=("arbitrary",), vmem_limit_bytes=VMEM_LIMIT),
        name="mix_prompt",
    )(x, *([x] * CH), x, nw, win_f32, cosf, sinf, rnw, hnw, lbl, wout_f32, wup_f32, wdn_f32)


def _mix_sample_kernel(seg, x_ref, nw_ref, win_ref, cos_ref, sin_ref, rnw_ref, hnw_ref, lbl_ref,
                       wout_ref, sret_in_ref, shg_in_ref,
                       x1_ref, sret_ref, shg_ref,
                       proj_ref, o_ref, qh_ref, kh_ref, dec_ref, ob_ref):
    rows = x_ref.shape[0]
    nsub = sret_in_ref.shape[0]
    j = pl.program_id(1)

    @pl.when(j == 0)
    def _():
        _project(x_ref, nw_ref, win_ref, proj_ref)
        t = lax.broadcasted_iota(jnp.int32, (rows, DH), 0)
        tl = (t & (seg - 1)).astype(F32)
        cosf = cos_ref[...]
        sinf = sin_ref[...]
        lb = _lower_bound(lbl_ref[...])
        masks = _hgrn_level_masks(rows, seg)
        for hd in range(HEADS):
            lg = _log_gamma(hd)
            q = _rotary(proj_ref[:, _cols(0, hd)], cosf, sinf)
            k = _rotary(proj_ref[:, _cols(1, hd)], cosf, sinf)
            scores = _dot_nt(q.astype(BF16), k.astype(BF16)) * _ret_mask(rows, seg, lg)
            o_ref[:, _cols(0, hd)] = _dot(scores.astype(BF16),
                                          proj_ref[:, _cols(2, hd)].astype(BF16))
            qh_ref[:, _cols(0, hd)] = q * jnp.exp((tl + 1.0) * lg)
            kh_ref[:, _cols(0, hd)] = k * (jnp.exp((seg - 1.0 - tl) * lg) * QK_SCALE)
        for hd in range(HEADS):
            q = _silu(proj_ref[:, _cols(4, hd)]) * QK_SCALE
            lbh = lb[:, hd * DH:(hd + 1) * DH]
            f = lbh + (1.0 - lbh) * _sigmoid(proj_ref[:, _cols(5, hd)])
            k = 1.0 - f
            o, b = _hgrn_intra(q, k, f, proj_ref[:, _cols(6, hd)], seg, t, masks)
            blast = _block_row_bcast(b, seg, seg - 1)
            o_ref[:, _cols(1, hd)] = o
            qh_ref[:, _cols(1, hd)] = q * jnp.exp2(b)
            kh_ref[:, _cols(1, hd)] = k * jnp.exp2(blast - b)
            dec_ref[:, hd * DH:(hd + 1) * DH] = jnp.exp2(blast)

    def seq_body(s, carry):
        r = pl.ds(pl.multiple_of((j * nsub + s) * seg, seg), seg)
        qh = qh_ref[r, :].astype(BF16)
        kh = kh_ref[r, :].astype(BF16)
        v_ret = proj_ref[r, 2 * GROUP_W:3 * GROUP_W].astype(BF16)
        v_hg = proj_ref[r, 6 * GROUP_W:7 * GROUP_W].astype(BF16)
        dec = dec_ref[r, :][0:1, :]
        o_parts, new_ret, new_hg = [], [], []
        for hd in range(HEADS):
            st = sret_in_ref[s, hd]
            c = slice(hd * DH, (hd + 1) * DH)
            o_parts.append(_dot(qh[:, c], st.astype(BF16)))
            new_ret.append(st * math.exp(seg * _log_gamma(hd)) + _dot_tn(kh[:, c], v_ret[:, c]))
        for hd in range(HEADS):
            st = shg_in_ref[s, hd]
            c = slice(hd * DH, (hd + 1) * DH)
            c1 = slice(GROUP_W + hd * DH, GROUP_W + (hd + 1) * DH)
            o_parts.append(_dot(qh[:, c1], st.astype(BF16)))
            dcol = jnp.broadcast_to(dec[:, c], (DH, DH)).T
            new_hg.append(st * dcol + _dot_tn(kh[:, c1], v_hg[:, c]))
        o_ref[r, :] = o_ref[r, :] + jnp.concatenate(o_parts, axis=1)
        for hd in range(HEADS):
            sret_ref[s, hd] = new_ret[hd]
            shg_ref[s, hd] = new_hg[hd]
        return carry

    lax.fori_loop(0, nsub, seq_body, 0, unroll=2)

    @pl.when(j == pl.num_programs(1) - 1)
    def _():
        for hd in range(HEADS):
            c0 = _cols(0, hd)
            ob_ref[:, c0] = _head_out(o_ref[:, c0], rnw_ref[...],
                                      proj_ref[:, _cols(3, hd)]).astype(BF16)
            c1 = _cols(1, hd)
            ob_ref[:, c1] = _head_out(o_ref[:, c1], hnw_ref[...],
                                      proj_ref[:, _cols(7, hd)]).astype(BF16)
        x1_ref[...] = x_ref[...] + _dot(ob_ref[...], wout_ref[...])


def _mix_sample(x, nw, win, cosf, sinf, rnw, hnw, lbl, wout, sret, shg):
    nb, seg, D = x.shape
    rows = SAMPLE_ROWS
    nsub = SAMPLE_SUB_SEQS
    nj = rows // seg // nsub
    x2 = x.reshape(nb * seg, D)
    state = jax.ShapeDtypeStruct((nb, HEADS, DH, DH), F32)
    state_map = lambda i, j: (i * nj + j, 0, 0, 0)
    state_spec = pl.BlockSpec((nsub, HEADS, DH, DH), state_map)
    x1, sret_new, shg_new = pl.pallas_call(
        functools.partial(_mix_sample_kernel, seg),
        grid=(nb * seg // rows, nj),
        in_specs=[
            pl.BlockSpec((rows, D), lambda i, j: (i, 0)),
            _const_spec((1, D)),
            _const_spec((D, IN_COLS)),
            _const_spec((rows, DH)),
            _const_spec((rows, DH)),
            _const_spec((1, DH)),
            _const_spec((1, DH)),
            _const_spec(lbl.shape),
            _const_spec((D, D)),
            state_spec,
            state_spec,
        ],
        out_specs=[pl.BlockSpec((rows, D), lambda i, j: (i, 0)), state_spec, state_spec],
        out_shape=[jax.ShapeDtypeStruct((nb * seg, D), F32), state, state],
        scratch_shapes=[
            pltpu.VMEM((rows, IN_COLS), F32),
            pltpu.VMEM((rows, D), F32),
            pltpu.VMEM((rows, D), F32),
            pltpu.VMEM((rows, D), F32),
            pltpu.VMEM((rows, GROUP_W), F32),
            pltpu.VMEM((rows, D), BF16),
        ],
        compiler_params=pltpu.CompilerParams(
            dimension_semantics=("arbitrary", "arbitrary"), vmem_limit_bytes=SAMPLE_VMEM_LIMIT),
        name="mix_sample",
    )(x2, nw, win, cosf, sinf, rnw, hnw, lbl, wout, sret, shg)
    return x1, sret_new, shg_new


def _ffn_kernel(x_ref, nw_ref, wup_ref, wdn_ref, fw_ref, y_ref):
    subs = [slice(r * FFN_SUB_ROWS, (r + 1) * FFN_SUB_ROWS)
            for r in range(x_ref.shape[0] // FFN_SUB_ROWS)]
    hs = [_rms(x_ref[rs, :], nw_ref[...]).astype(BF16) for rs in subs]
    accs = [x_ref[rs, :] for rs in subs]
    step = D_MODEL
    for g in range(D_FF // step):
        us = [jnp.maximum(_dot(h, wup_ref[:, g * step:(g + 1) * step]), 0.0) for h in hs]
        accs = [acc + _dot((u * u).astype(BF16), wdn_ref[g * step:(g + 1) * step, :])
                for acc, u in zip(accs, us)]
    for rs, acc in zip(subs, accs):
        y_ref[rs, :] = _rms(acc, fw_ref[...])


def _ffn_rows_kernel(n_prompt_steps, xp_ref, xs_ref, nw_ref, wup_ref, wdn_ref, fw_ref,
                     yp_ref, ys_ref):
    i = pl.program_id(0)

    @pl.when(i < n_prompt_steps)
    def _():
        _ffn_kernel(xp_ref, nw_ref, wup_ref, wdn_ref, fw_ref, yp_ref)

    @pl.when(i >= n_prompt_steps)
    def _():
        _ffn_kernel(xs_ref, nw_ref, wup_ref, wdn_ref, fw_ref, ys_ref)


def _ffn(xp, xs, nw, wup, wdn, fw):
    n_p, D = xp.shape
    n_s = xs.shape[0]
    rows = FFN_ROWS
    assert n_p % rows == 0 and n_s % rows == 0
    p_steps, s_steps = n_p // rows, n_s // rows
    p_spec = pl.BlockSpec((rows, D), lambda i: (jnp.minimum(i, p_steps - 1), 0))
    s_spec = pl.BlockSpec((rows, D), lambda i: (jnp.maximum(i - p_steps, 0), 0),
                          pipeline_mode=pl.Buffered(1))
    return pl.pallas_call(
        functools.partial(_ffn_rows_kernel, p_steps),
        grid=(p_steps + s_steps,),
        in_specs=[
            p_spec,
            s_spec,
            _const_spec((1, D)),
            _const_spec((D, D_FF)),
            _const_spec((D_FF, D)),
            _const_spec((1, D)),
        ],
        out_specs=[p_spec, s_spec],
        out_shape=[jax.ShapeDtypeStruct((n_p, D), F32), jax.ShapeDtypeStruct((n_s, D), F32)],
        compiler_params=pltpu.CompilerParams(
            dimension_semantics=("arbitrary",), vmem_limit_bytes=VMEM_LIMIT),
        name="ffn",
    )(xp, xs, nw, wup, wdn, fw)


def _rope_tables(pos):
    half = DH // 2
    inv_freq = ROPE_BASE ** (-jnp.arange(half, dtype=F32) / half)
    ang = pos.astype(F32)[:, None] * inv_freq[None, :]
    cos, sin = jnp.cos(ang), jnp.sin(ang)
    return jnp.concatenate([cos, cos], axis=-1), jnp.concatenate([-sin, sin], axis=-1)


def kernel(x_prompt, x_sample, state_ret, state_hgrn, norm_mix_w, w_in, ret_norm_w, hgrn_norm_w,
           lb_logits, w_out, norm_ffn_w, w_up, w_down, final_norm_w):
    B, T, D = x_prompt.shape
    nb, seg, _ = x_sample.shape
    assert w_in.shape[0] == 1, "one layer"

    nw = norm_mix_w[0].reshape(1, D)
    rnw = ret_norm_w[0].reshape(1, DH)
    hnw = hgrn_norm_w[0].reshape(1, DH)
    lbl = lb_logits.astype(F32)
    fnw = norm_ffn_w[0].reshape(1, D)
    fw = final_norm_w.reshape(1, D)

    cos_p, sin_p = _rope_tables(jnp.arange(T, dtype=jnp.int32))
    cos_s, sin_s = _rope_tables(PAST_LEN + jnp.arange(seg, dtype=jnp.int32))
    cos_s = jnp.tile(cos_s, (SAMPLE_ROWS // seg, 1))
    sin_s = jnp.tile(sin_s, (SAMPLE_ROWS // seg, 1))

    xp1, sret_p, shg_p, win, wout, wup, wdn = _mix_prompt(
        x_prompt, nw, w_in[0], cos_p, sin_p, rnw, hnw, lbl, w_out[0], w_up[0], w_down[0])
    xs1, sret_s, shg_s = _mix_sample(x_sample, nw, win, cos_s, sin_s, rnw, hnw, lbl, wout,
                                     state_ret[0], state_hgrn[0])

    y_p, y_s = _ffn(xp1.reshape(B * T, D), xs1, fnw, wup, wdn, fw)
    y_p = y_p.reshape(B, T, D)
    y_s = y_s.reshape(nb, seg, D)
    return (y_p, y_s, sret_p[None], shg_p[None], sret_s[None], shg_s[None])
```

```python
import functools
import math

import jax
import jax.numpy as jnp
from jax import lax
from jax.experimental import pallas as pl
from jax.experimental.pallas import tpu as pltpu

D_MODEL = 1024
HEADS = 4
DH = 128
GROUP_W = HEADS * DH
IN_COLS = 8 * GROUP_W
D_FF = 4 * D_MODEL
ROPE_BASE = 10000.0
NORM_EPS = 1e-6
QK_SCALE = DH ** -0.5
PAST_LEN = 16384

PROMPT_CHUNK = 256
PROMPT_STEP_CHUNKS = 2
SAMPLE_ROWS = 256
SAMPLE_SUB_SEQS = 16
FFN_ROWS = 1024
FFN_SUB_ROWS = 512
PROJ_PIECE = 512
CAST_BUFFERS = 4
CAST_ROWS = 512
VMEM_LIMIT = 56 * 1024 * 1024
SAMPLE_VMEM_LIMIT = 60 * 1024 * 1024

F32 = jnp.float32
BF16 = jnp.bfloat16


def _dot(a, b):
    return jnp.dot(a, b, preferred_element_type=F32)


def _dot_nt(a, b):
    return lax.dot_general(a, b, (((1,), (1,)), ((), ())), preferred_element_type=F32)


def _dot_tn(a, b):
    return lax.dot_general(a, b, (((0,), (0,)), ((), ())), preferred_element_type=F32)


def _rms(x, w):
    ms = jnp.mean(x * x, axis=-1, keepdims=True)
    return x * lax.rsqrt(ms + NORM_EPS) * w


def _sigmoid(x):
    return 0.5 * jnp.tanh(0.5 * x) + 0.5


def _silu(x):
    return x * _sigmoid(x)


def _lower_bound(lbl):
    mx = jnp.max(lbl, axis=0, keepdims=True)
    e = jnp.exp(lbl - mx)
    return e[0:1, :] / jnp.sum(e, axis=0, keepdims=True)


def _rotary(x, cosf, sinf):
    return x * cosf + pltpu.roll(x, DH // 2, 1) * sinf


def _split_halves(x, m):
    blk = 2 * m
    nb = x.shape[0] // blk
    lo = [x[b * blk:b * blk + m] for b in range(nb)]
    up = [x[b * blk + m:(b + 1) * blk] for b in range(nb)]
    cat = lambda ps: ps[0] if len(ps) == 1 else jnp.concatenate(ps, axis=0)
    return cat(lo), cat(up)


def _merge_halves(lo, up, m):
    nb = lo.shape[0] // m
    pieces = []
    for b in range(nb):
        pieces += [lo[b * m:(b + 1) * m], up[b * m:(b + 1) * m]]
    return jnp.concatenate(pieces, axis=0)


def _block_row_bcast(x, m, row):
    nb = x.shape[0] // m
    pieces = [jnp.broadcast_to(x[b * m + row:b * m + row + 1], (m, x.shape[1])) for b in range(nb)]
    return pieces[0] if nb == 1 else jnp.concatenate(pieces, axis=0)


def _hgrn_level_masks(rows, seg):
    masks = []
    for j in range(int(math.log2(seg))):
        if (1 << j) < 8:
            r = lax.broadcasted_iota(jnp.int32, (rows, rows), 0)
            s = lax.broadcasted_iota(jnp.int32, (rows, rows), 1)
            masks.append(jnp.logical_and(((r ^ s) >> j) == 1, r > s))
        else:
            r = lax.broadcasted_iota(jnp.int32, (rows // 2, rows), 0)
            s = lax.broadcasted_iota(jnp.int32, (rows // 2, rows), 1)
            masks.append((s >> j) == 2 * (r >> j))
    return masks


def _hgrn_intra(q, k, f, v, seg, t, masks, mid_hook=None):
    rows = q.shape[0]
    c = jnp.log2(f)
    scores = None
    for j in range(int(math.log2(seg))):
        m = 1 << j
        if m < 8:
            c3 = c.reshape(rows // 8, 8, DH)
            upper = (t & m) != 0
            if m == 1:
                w = jnp.where(upper, f, 1.0)
                tot = pltpu.roll(c3, 1, 1).reshape(rows, DH)
            else:
                if m == 2:
                    low4 = ((t & 7) < 4).reshape(rows // 8, 8, DH)
                    tot3 = jnp.where(low4, jnp.broadcast_to(c3[:, 1:2, :], c3.shape),
                                     jnp.broadcast_to(c3[:, 5:6, :], c3.shape))
                else:
                    tot3 = jnp.broadcast_to(c3[:, 3:4, :], c3.shape)
                tot = tot3.reshape(rows, DH)
                w = jnp.exp2(jnp.where(upper, c, tot - c))
            part = _dot_nt((q * w).astype(BF16), (k * w).astype(BF16))
            scores = jnp.where(masks[j], part, 0.0 if scores is None else scores)
            c = jnp.where(upper, c + tot, c)
        else:
            c_lo, c_up = _split_halves(c, m)
            tot = _block_row_bcast(c_lo, m, m - 1)
            q_up = _split_halves(q, m)[1]
            k_lo, k_up = _split_halves(k, m)
            qt = (q_up * jnp.exp2(c_up)).astype(BF16)
            kt = _merge_halves(k_lo * jnp.exp2(tot - c_lo), k_up, m).astype(BF16)
            s_lo, s_up = _split_halves(scores, m)
            s_up = jnp.where(masks[j], _dot_nt(qt, kt), s_up)
            scores = _merge_halves(s_lo, s_up, m)
            c = _merge_halves(c_lo, c_up + tot, m)
    if mid_hook is not None:
        mid_hook()
    o = _dot(scores.astype(BF16), v.astype(BF16))
    o = o + jnp.sum(q * k, axis=-1, keepdims=True) * v
    return o, c


def _ret_mask(rows, seg, log_gamma):
    r = lax.broadcasted_iota(jnp.int32, (rows, rows), 0)
    s = lax.broadcasted_iota(jnp.int32, (rows, rows), 1)
    valid = jnp.logical_and(((r ^ s) >> int(math.log2(seg))) == 0, r >= s)
    d = jnp.where(valid, r - s, 0).astype(F32)
    return jnp.where(valid, jnp.exp(d * log_gamma) * QK_SCALE, 0.0)


def _log_gamma(hd):
    return math.log(1.0 - 2.0 ** (-5.0 - hd))


def _head_out(o, w, gate):
    return o * lax.rsqrt(jnp.mean(o * o, axis=-1, keepdims=True) + NORM_EPS) * w * _silu(gate)


def _project(x_ref, nw_ref, win_ref, proj_ref):
    h = _rms(x_ref[...], nw_ref[...]).astype(BF16)
    for g in range(8):
        cols = slice(g * GROUP_W, (g + 1) * GROUP_W)
        proj_ref[:, cols] = _dot(h, win_ref[:, cols])


def _cast_weight(w_hbm, w_bf_ref, stage_ref, sem):
    nbuf, brows, bcols = stage_ref.shape
    rows, width = w_hbm.shape
    k = width // bcols
    crows = brows // k
    n = rows // crows

    def copies(i):
        return [pltpu.make_async_copy(
            w_hbm.at[pl.ds(i * crows, crows), pl.ds(c * bcols, bcols)],
            stage_ref.at[i % nbuf, pl.ds(c * crows, crows), :],
            sem.at[i % nbuf, c]) for c in range(k)]

    for i in range(min(nbuf - 1, n)):
        for cp in copies(i):
            cp.start()
    for i in range(n):
        if i + nbuf - 1 < n:
            for cp in copies(i + nbuf - 1):
                cp.start()
        for c, cp in enumerate(copies(i)):
            cp.wait()
            w_bf_ref[pl.ds(i * crows, crows), pl.ds(c * bcols, bcols)] = (
                stage_ref[i % nbuf, c * crows:(c + 1) * crows, :].astype(BF16))


def _cols(group, hd):
    return slice(group * GROUP_W + hd * DH, group * GROUP_W + (hd + 1) * DH)


def _mix_prompt_kernel(nc, *refs):
    CH = PROMPT_STEP_CHUNKS
    L = PROMPT_CHUNK
    x0_ref = refs[0]
    xn_refs = refs[1:1 + CH]
    (xc_ref, nw_ref, win_hbm, cos_ref, sin_ref, rnw_ref, hnw_ref, lbl_ref, wout_hbm,
     wup_f32_ref, wdn_f32_ref,
     x1_ref, sret_ref, shg_ref, win_ref, wout_ref, wup_bf_ref, wdn_bf_ref,
     proj_ref, o_ref, srt_ref, sht_ref, dm_ref, qd_ref, kd_ref, stage_ref, sem) = refs[1 + CH:]
    g = pl.program_id(0)
    ci0 = (g * CH) % nc
    t = lax.broadcasted_iota(jnp.int32, (L, DH), 0)

    @pl.when(g == 0)
    def _():
        _cast_weight(win_hbm, win_ref, stage_ref, sem)
        _cast_weight(wout_hbm, wout_ref, stage_ref, sem)
        tf = t.astype(F32)
        for hd in range(HEADS):
            lg = _log_gamma(hd)
            dm_ref[hd] = _ret_mask(L, L, lg)
            qd_ref[hd] = jnp.exp((tf + 1.0) * lg)
            kd_ref[hd] = jnp.exp((L - 1.0 - tf) * lg) * QK_SCALE
        _project(x0_ref, nw_ref, win_ref, proj_ref.at[0])

    @pl.when(ci0 == 0)
    def _():
        srt_ref[...] = jnp.zeros_like(srt_ref)
        sht_ref[...] = jnp.zeros_like(sht_ref)

    wup_bf_ref[...] = wup_f32_ref[...].astype(BF16)
    wdn_bf_ref[...] = wdn_f32_ref[...].astype(BF16)

    lb = _lower_bound(lbl_ref[...])
    masks = _hgrn_level_masks(L, L)

    def chunk(pr, pw, xn_ref, rows):
        h_next = _rms(xn_ref[...], nw_ref[...]).astype(BF16)

        def project_piece(pi):
            cols = slice(pi * PROJ_PIECE, (pi + 1) * PROJ_PIECE)
            pw[:, cols] = _dot(h_next, win_ref[:, cols])

        cosf = cos_ref[rows, :]
        sinf = sin_ref[rows, :]

        for hd in range(HEADS):
            q = _rotary(pr[:, _cols(0, hd)], cosf, sinf)
            k = _rotary(pr[:, _cols(1, hd)], cosf, sinf)
            v = pr[:, _cols(2, hd)].astype(BF16)
            scores = _dot_nt(q.astype(BF16), k.astype(BF16)) * dm_ref[hd]
            project_piece(hd)
            o = _dot(scores.astype(BF16), v)
            st = srt_ref[hd]
            o = o + _dot_nt((q * qd_ref[hd]).astype(BF16), st.astype(BF16))
            kh = (k * kd_ref[hd]).astype(BF16)
            srt_ref[hd] = st * math.exp(L * _log_gamma(hd)) + _dot_tn(v, kh)
            o_ref[:, _cols(0, hd)] = _head_out(o, rnw_ref[...], pr[:, _cols(3, hd)]).astype(BF16)

        for hd in range(HEADS):
            q = _silu(pr[:, _cols(4, hd)]) * QK_SCALE
            lbh = lb[:, hd * DH:(hd + 1) * DH]
            f = lbh + (1.0 - lbh) * _sigmoid(pr[:, _cols(5, hd)])
            k = 1.0 - f
            v = pr[:, _cols(6, hd)]
            o, b = _hgrn_intra(q, k, f, v, L, t, masks,
                               functools.partial(project_piece, HEADS + hd))
            blast = b[L - 1:L, :]
            st = sht_ref[hd]
            o = o + _dot_nt((q * jnp.exp2(b)).astype(BF16), st.astype(BF16))
            kh = (k * jnp.exp2(blast - b)).astype(BF16)
            sht_ref[hd] = st * jnp.exp2(blast) + _dot_tn(v.astype(BF16), kh)
            o_ref[:, _cols(1, hd)] = _head_out(o, hnw_ref[...], pr[:, _cols(7, hd)]).astype(BF16)

        x1_ref[rows, :] = xc_ref[rows, :] + _dot(o_ref[...], wout_ref[...])

    for sub in range(CH):
        chunk(proj_ref.at[sub % 2], proj_ref.at[(sub + 1) % 2], xn_refs[sub],
              slice(sub * L, (sub + 1) * L))

    @pl.when(ci0 == nc - CH)
    def _():
        for hd in range(HEADS):
            sret_ref[hd] = srt_ref[hd].T
            shg_ref[hd] = sht_ref[hd].T


def _const_spec(shape):
    return pl.BlockSpec(shape, lambda *_: (0,) * len(shape), pipeline_mode=pl.Buffered(1))


def _mix_prompt(x, nw, win_f32, cosf, sinf, rnw, hnw, lbl, wout_f32, wup_f32, wdn_f32):
    B, T, D = x.shape
    L = PROMPT_CHUNK
    CH = PROMPT_STEP_CHUNKS
    nc = T // L
    n = B * nc
    spb = nc // CH
    assert nc % CH == 0 and CH % 2 == 0

    def next_chunk_spec(i):
        idx = lambda g: jnp.minimum(CH * g + i + 1, n - 1)
        return pl.BlockSpec((None, L, D), lambda g: (idx(g) // nc, idx(g) % nc, 0))

    state = jax.ShapeDtypeStruct((B, HEADS, DH, DH), F32)
    state_spec = pl.BlockSpec((None, HEADS, DH, DH), lambda g: (g // spb, 0, 0, 0))
    rows_spec = pl.BlockSpec((None, CH * L, D), lambda g: (g // spb, g % spb, 0))
    table_spec = pl.BlockSpec((CH * L, DH), lambda g: (g % spb, 0))
    steps = n // CH
    wup_spec = pl.BlockSpec((D // steps, D_FF), lambda g: (g, 0))
    wdn_spec = pl.BlockSpec((D_FF // steps, D), lambda g: (g, 0))
    return pl.pallas_call(
        functools.partial(_mix_prompt_kernel, nc),
        grid=(steps,),
        in_specs=[
            pl.BlockSpec((None, L, D), lambda g: (0, 0, 0), pipeline_mode=pl.Buffered(1)),
            *[next_chunk_spec(i) for i in range(CH)],
            rows_spec,
            _const_spec((1, D)),
            pl.BlockSpec(memory_space=pl.ANY),
            table_spec,
            table_spec,
            _const_spec((1, DH)),
            _const_spec((1, DH)),
            _const_spec(lbl.shape),
            pl.BlockSpec(memory_space=pl.ANY),
            wup_spec,
            wdn_spec,
        ],
        out_specs=[rows_spec, state_spec, state_spec,
                   pl.BlockSpec((D, IN_COLS), lambda g: (0, 0), pipeline_mode=pl.Buffered(1)),
                   pl.BlockSpec((D, D), lambda g: (0, 0), pipeline_mode=pl.Buffered(1)),
                   wup_spec, wdn_spec],
        out_shape=[jax.ShapeDtypeStruct((B, T, D), F32), state, state,
                   jax.ShapeDtypeStruct((D, IN_COLS), BF16), jax.ShapeDtypeStruct((D, D), BF16),
                   jax.ShapeDtypeStruct((D, D_FF), BF16), jax.ShapeDtypeStruct((D_FF, D), BF16)],
        scratch_shapes=[
            pltpu.VMEM((2, L, IN_COLS), F32),
            pltpu.VMEM((L, D), BF16),
            pltpu.VMEM((HEADS, DH, DH), F32),
            pltpu.VMEM((HEADS, DH, DH), F32),
            pltpu.VMEM((HEADS, L, L), F32),
            pltpu.VMEM((HEADS, L, DH), F32),
            pltpu.VMEM((HEADS, L, DH), F32),
            pltpu.VMEM((CAST_BUFFERS, CAST_ROWS, D), F32),
            pltpu.SemaphoreType.DMA((CAST_BUFFERS, IN_COLS // D)),
        ],
        compiler_params=pltpu.CompilerParams(
            dimension_semantics=("arbitrary",), vmem_limit_bytes=VMEM_LIMIT),
        name="mix_prompt",
    )(x, *([x] * CH), x, nw, win_f32, cosf, sinf, rnw, hnw, lbl, wout_f32, wup_f32, wdn_f32)


def _mix_sample_kernel(seg, x_ref, nw_ref, win_ref, cos_ref, sin_ref, rnw_ref, hnw_ref, lbl_ref,
                       wout_ref, sret_lo_ref, sret_hi_ref, shg_lo_ref, shg_hi_ref,
                       x1_ref, sret_ref, shg_ref,
                       proj_ref, o_ref, qh_ref, kh_ref, dec_ref, ob_ref):
    rows = x_ref.shape[0]
    nsub = sret_lo_ref.shape[0]
    sret_in = (sret_lo_ref, sret_hi_ref)
    shg_in = (shg_lo_ref, shg_hi_ref)
    hh = HEADS // 2
    j = pl.program_id(1)

    @pl.when(j == 0)
    def _():
        _project(x_ref, nw_ref, win_ref, proj_ref)
        t = lax.broadcasted_iota(jnp.int32, (rows, DH), 0)
        tl = (t & (seg - 1)).astype(F32)
        cosf = cos_ref[...]
        sinf = sin_ref[...]
        lb = _lower_bound(lbl_ref[...])
        masks = _hgrn_level_masks(rows, seg)
        for hd in range(HEADS):
            lg = _log_gamma(hd)
            q = _rotary(proj_ref[:, _cols(0, hd)], cosf, sinf)
            k = _rotary(proj_ref[:, _cols(1, hd)], cosf, sinf)
            scores = _dot_nt(q.astype(BF16), k.astype(BF16)) * _ret_mask(rows, seg, lg)
            o_ref[:, _cols(0, hd)] = _dot(scores.astype(BF16),
                                          proj_ref[:, _cols(2, hd)].astype(BF16))
            qh_ref[:, _cols(0, hd)] = q * jnp.exp((tl + 1.0) * lg)
            kh_ref[:, _cols(0, hd)] = k * (jnp.exp((seg - 1.0 - tl) * lg) * QK_SCALE)
        for hd in range(HEADS):
            q = _silu(proj_ref[:, _cols(4, hd)]) * QK_SCALE
            lbh = lb[:, hd * DH:(hd + 1) * DH]
            f = lbh + (1.0 - lbh) * _sigmoid(proj_ref[:, _cols(5, hd)])
            k = 1.0 - f
            o, b = _hgrn_intra(q, k, f, proj_ref[:, _cols(6, hd)], seg, t, masks)
            blast = _block_row_bcast(b, seg, seg - 1)
            o_ref[:, _cols(1, hd)] = o
            qh_ref[:, _cols(1, hd)] = q * jnp.exp2(b)
            kh_ref[:, _cols(1, hd)] = k * jnp.exp2(blast - b)
            dec_ref[:, hd * DH:(hd + 1) * DH] = jnp.exp2(blast)

    def seq_body(s, carry):
        r = pl.ds(pl.multiple_of((j * nsub + s) * seg, seg), seg)
        qh = qh_ref[r, :].astype(BF16)
        kh = kh_ref[r, :].astype(BF16)
        v_ret = proj_ref[r, 2 * GROUP_W:3 * GROUP_W].astype(BF16)
        v_hg = proj_ref[r, 6 * GROUP_W:7 * GROUP_W].astype(BF16)
        dec = dec_ref[r, :][0:1, :]
        o_parts, new_ret, new_hg = [], [], []
        for hd in range(HEADS):
            st = sret_in[hd // hh][s, hd % hh]
            c = slice(hd * DH, (hd + 1) * DH)
            o_parts.append(_dot(qh[:, c], st.astype(BF16)))
            new_ret.append(st * math.exp(seg * _log_gamma(hd)) + _dot_tn(kh[:, c], v_ret[:, c]))
        for hd in range(HEADS):
            st = shg_in[hd // hh][s, hd % hh]
            c = slice(hd * DH, (hd + 1) * DH)
            c1 = slice(GROUP_W + hd * DH, GROUP_W + (hd + 1) * DH)
            o_parts.append(_dot(qh[:, c1], st.astype(BF16)))
            dcol = jnp.broadcast_to(dec[:, c], (DH, DH)).T
            new_hg.append(st * dcol + _dot_tn(kh[:, c1], v_hg[:, c]))
        o_ref[r, :] = o_ref[r, :] + jnp.concatenate(o_parts, axis=1)
        for hd in range(HEADS):
            sret_ref[s, hd] = new_ret[hd]
            shg_ref[s, hd] = new_hg[hd]
        return carry

    lax.fori_loop(0, nsub, seq_body, 0, unroll=2)

    @pl.when(j == pl.num_programs(1) - 1)
    def _():
        for hd in range(HEADS):
            c0 = _cols(0, hd)
            ob_ref[:, c0] = _head_out(o_ref[:, c0], rnw_ref[...],
                                      proj_ref[:, _cols(3, hd)]).astype(BF16)
            c1 = _cols(1, hd)
            ob_ref[:, c1] = _head_out(o_ref[:, c1], hnw_ref[...],
                                      proj_ref[:, _cols(7, hd)]).astype(BF16)
        x1_ref[...] = x_ref[...] + _dot(ob_ref[...], wout_ref[...])


def _mix_sample(x, nw, win, cosf, sinf, rnw, hnw, lbl, wout, sret, shg):
    nb, seg, D = x.shape
    rows = SAMPLE_ROWS
    nsub = SAMPLE_SUB_SEQS
    nj = rows // seg // nsub
    x2 = x.reshape(nb * seg, D)
    state = jax.ShapeDtypeStruct((nb, HEADS, DH, DH), F32)
    state_map = lambda i, j: (i * nj + j, 0, 0, 0)
    state_spec = pl.BlockSpec((nsub, HEADS, DH, DH), state_map)
    half_specs = [pl.BlockSpec((nsub, HEADS // 2, DH, DH), lambda i, j, h=h: (i * nj + j, h, 0, 0))
                  for h in range(2)]
    x1, sret_new, shg_new = pl.pallas_call(
        functools.partial(_mix_sample_kernel, seg),
        grid=(nb * seg // rows, nj),
        in_specs=[
            pl.BlockSpec((rows, D), lambda i, j: (i, 0)),
            _const_spec((1, D)),
            _const_spec((D, IN_COLS)),
            _const_spec((rows, DH)),
            _const_spec((rows, DH)),
            _const_spec((1, DH)),
            _const_spec((1, DH)),
            _const_spec(lbl.shape),
            _const_spec((D, D)),
            *half_specs,
            *half_specs,
        ],
        out_specs=[pl.BlockSpec((rows, D), lambda i, j: (i, 0)), state_spec, state_spec],
        out_shape=[jax.ShapeDtypeStruct((nb * seg, D), F32), state, state],
        scratch_shapes=[
            pltpu.VMEM((rows, IN_COLS), F32),
            pltpu.VMEM((rows, D), F32),
            pltpu.VMEM((rows, D), F32),
            pltpu.VMEM((rows, D), F32),
            pltpu.VMEM((rows, GROUP_W), F32),
            pltpu.VMEM((rows, D), BF16),
        ],
        compiler_params=pltpu.CompilerParams(
            dimension_semantics=("arbitrary", "arbitrary"), vmem_limit_bytes=SAMPLE_VMEM_LIMIT),
        name="mix_sample",
    )(x2, nw, win, cosf, sinf, rnw, hnw, lbl, wout, sret, sret, shg, shg)
    return x1, sret_new, shg_new


def _ffn_kernel(x_ref, nw_ref, wup_ref, wdn_ref, fw_ref, y_ref):
    subs = [slice(r * FFN_SUB_ROWS, (r + 1) * FFN_SUB_ROWS)
            for r in range(x_ref.shape[0] // FFN_SUB_ROWS)]
    hs = [_rms(x_ref[rs, :], nw_ref[...]).astype(BF16) for rs in subs]
    accs = [x_ref[rs, :] for rs in subs]
    step = D_MODEL
    for g in range(D_FF // step):
        us = [jnp.maximum(_dot(h, wup_ref[:, g * step:(g + 1) * step]), 0.0) for h in hs]
        accs = [acc + _dot((u * u).astype(BF16), wdn_ref[g * step:(g + 1) * step, :])
                for acc, u in zip(accs, us)]
    for rs, acc in zip(subs, accs):
        y_ref[rs, :] = _rms(acc, fw_ref[...])


def _ffn_rows_kernel(n_prompt_steps, xp_ref, xs_ref, nw_ref, wup_ref, wdn_ref, fw_ref,
                     yp_ref, ys_ref):
    i = pl.program_id(0)

    @pl.when(i < n_prompt_steps)
    def _():
        _ffn_kernel(xp_ref, nw_ref, wup_ref, wdn_ref, fw_ref, yp_ref)

    @pl.when(i >= n_prompt_steps)
    def _():
        _ffn_kernel(xs_ref, nw_ref, wup_ref, wdn_ref, fw_ref, ys_ref)


def _ffn(xp, xs, nw, wup, wdn, fw):
    n_p, D = xp.shape
    n_s = xs.shape[0]
    rows = FFN_ROWS
    assert n_p % rows == 0 and n_s % rows == 0
    p_steps, s_steps = n_p // rows, n_s // rows
    p_spec = pl.BlockSpec((rows, D), lambda i: (jnp.minimum(i, p_steps - 1), 0))
    s_spec = pl.BlockSpec((rows, D), lambda i: (jnp.maximum(i - p_steps, 0), 0),
                          pipeline_mode=pl.Buffered(1))
    return pl.pallas_call(
        functools.partial(_ffn_rows_kernel, p_steps),
        grid=(p_steps + s_steps,),
        in_specs=[
            p_spec,
            s_spec,
            _const_spec((1, D)),
            _const_spec((D, D_FF)),
            _const_spec((D_FF, D)),
            _const_spec((1, D)),
        ],
        out_specs=[p_spec, s_spec],
        out_shape=[jax.ShapeDtypeStruct((n_p, D), F32), jax.ShapeDtypeStruct((n_s, D), F32)],
        compiler_params=pltpu.CompilerParams(
            dimension_semantics=("arbitrary",), vmem_limit_bytes=VMEM_LIMIT),
        name="ffn",
    )(xp, xs, nw, wup, wdn, fw)


def _rope_tables(pos):
    half = DH // 2
    inv_freq = ROPE_BASE ** (-jnp.arange(half, dtype=F32) / half)
    ang = pos.astype(F32)[:, None] * inv_freq[None, :]
    cos, sin = jnp.cos(ang), jnp.sin(ang)
    return jnp.concatenate([cos, cos], axis=-1), jnp.concatenate([-sin, sin], axis=-1)


def kernel(x_prompt, x_sample, state_ret, state_hgrn, norm_mix_w, w_in, ret_norm_w, hgrn_norm_w,
           lb_logits, w_out, norm_ffn_w, w_up, w_down, final_norm_w):
    B, T, D = x_prompt.shape
    nb, seg, _ = x_sample.shape
    assert w_in.shape[0] == 1, "one layer"

    nw = norm_mix_w[0].reshape(1, D)
    rnw = ret_norm_w[0].reshape(1, DH)
    hnw = hgrn_norm_w[0].reshape(1, DH)
    lbl = lb_logits.astype(F32)
    fnw = norm_ffn_w[0].reshape(1, D)
    fw = final_norm_w.reshape(1, D)

    cos_p, sin_p = _rope_tables(jnp.arange(T, dtype=jnp.int32))
    cos_s, sin_s = _rope_tables(PAST_LEN + jnp.arange(seg, dtype=jnp.int32))
    cos_s = jnp.tile(cos_s, (SAMPLE_ROWS // seg, 1))
    sin_s = jnp.tile(sin_s, (SAMPLE_ROWS // seg, 1))

    xp1, sret_p, shg_p, win, wout, wup, wdn = _mix_prompt(
        x_prompt, nw, w_in[0], cos_p, sin_p, rnw, hnw, lbl, w_out[0], w_up[0], w_down[0])
    xs1, sret_s, shg_s = _mix_sample(x_sample, nw, win, cos_s, sin_s, rnw, hnw, lbl, wout,
                                     state_ret[0], state_hgrn[0])

    y_p, y_s = _ffn(xp1.reshape(B * T, D), xs1, fnw, wup, wdn, fw)
    y_p = y_p.reshape(B, T, D)
    y_s = y_s.reshape(nb, seg, D)
    return (y_p, y_s, sret_p[None], shg_p[None], sret_s[None], shg_s[None])
```

```python
import functools
import math

import jax
import jax.numpy as jnp
from jax import lax
from jax.experimental import pallas as pl
from jax.experimental.pallas import tpu as pltpu

D_MODEL = 1024
HEADS = 4
DH = 128
GROUP_W = HEADS * DH
IN_COLS = 8 * GROUP_W
D_FF = 4 * D_MODEL
ROPE_BASE = 10000.0
NORM_EPS = 1e-6
QK_SCALE = DH ** -0.5
PAST_LEN = 16384

PROMPT_CHUNK = 256
PROMPT_STEP_CHUNKS = 2
SAMPLE_ROWS = 256
SAMPLE_SUB_SEQS = 16
FFN_ROWS = 1024
FFN_SUB_ROWS = 512
PROJ_PIECE = 512
CAST_BUFFERS = 4
CAST_ROWS = 512
VMEM_LIMIT = 56 * 1024 * 1024
SAMPLE_VMEM_LIMIT = 60 * 1024 * 1024

F32 = jnp.float32
BF16 = jnp.bfloat16


def _dot(a, b):
    return jnp.dot(a, b, preferred_element_type=F32)


def _dot_nt(a, b):
    return lax.dot_general(a, b, (((1,), (1,)), ((), ())), preferred_element_type=F32)


def _dot_tn(a, b):
    return lax.dot_general(a, b, (((0,), (0,)), ((), ())), preferred_element_type=F32)


def _rms(x, w):
    ms = jnp.mean(x * x, axis=-1, keepdims=True)
    return x * lax.rsqrt(ms + NORM_EPS) * w


def _sigmoid(x):
    return 0.5 * jnp.tanh(0.5 * x) + 0.5


def _silu(x):
    return x * _sigmoid(x)


def _lower_bound(lbl):
    mx = jnp.max(lbl, axis=0, keepdims=True)
    e = jnp.exp(lbl - mx)
    return e[0:1, :] / jnp.sum(e, axis=0, keepdims=True)


def _rotary(x, cosf, sinf):
    return x * cosf + pltpu.roll(x, DH // 2, 1) * sinf


def _split_halves(x, m):
    blk = 2 * m
    nb = x.shape[0] // blk
    lo = [x[b * blk:b * blk + m] for b in range(nb)]
    up = [x[b * blk + m:(b + 1) * blk] for b in range(nb)]
    cat = lambda ps: ps[0] if len(ps) == 1 else jnp.concatenate(ps, axis=0)
    return cat(lo), cat(up)


def _merge_halves(lo, up, m):
    nb = lo.shape[0] // m
    pieces = []
    for b in range(nb):
        pieces += [lo[b * m:(b + 1) * m], up[b * m:(b + 1) * m]]
    return jnp.concatenate(pieces, axis=0)


def _block_row_bcast(x, m, row):
    nb = x.shape[0] // m
    pieces = [jnp.broadcast_to(x[b * m + row:b * m + row + 1], (m, x.shape[1])) for b in range(nb)]
    return pieces[0] if nb == 1 else jnp.concatenate(pieces, axis=0)


def _hgrn_level_masks(rows, seg):
    masks = []
    for j in range(int(math.log2(seg))):
        if (1 << j) < 8:
            r = lax.broadcasted_iota(jnp.int32, (rows, rows), 0)
            s = lax.broadcasted_iota(jnp.int32, (rows, rows), 1)
            masks.append(jnp.logical_and(((r ^ s) >> j) == 1, r > s))
        else:
            r = lax.broadcasted_iota(jnp.int32, (rows // 2, rows), 0)
            s = lax.broadcasted_iota(jnp.int32, (rows // 2, rows), 1)
            masks.append((s >> j) == 2 * (r >> j))
    return masks


def _hgrn_intra(q, k, f, v, seg, t, masks, mid_hook=None, transposed=False):
    rows = q.shape[0]
    c = jnp.log2(f)
    scores = None
    for j in range(int(math.log2(seg))):
        m = 1 << j
        if m < 8:
            c3 = c.reshape(rows // 8, 8, DH)
            upper = (t & m) != 0
            if m == 1:
                w = jnp.where(upper, f, 1.0)
                tot = pltpu.roll(c3, 1, 1).reshape(rows, DH)
            else:
                if m == 2:
                    low4 = ((t & 7) < 4).reshape(rows // 8, 8, DH)
                    tot3 = jnp.where(low4, jnp.broadcast_to(c3[:, 1:2, :], c3.shape),
                                     jnp.broadcast_to(c3[:, 5:6, :], c3.shape))
                else:
                    tot3 = jnp.broadcast_to(c3[:, 3:4, :], c3.shape)
                tot = tot3.reshape(rows, DH)
                w = jnp.exp2(jnp.where(upper, c, tot - c))
            part = _dot_nt((q * w).astype(BF16), (k * w).astype(BF16))
            scores = jnp.where(masks[j], part, 0.0 if scores is None else scores)
            c = jnp.where(upper, c + tot, c)
        else:
            c_lo, c_up = _split_halves(c, m)
            tot = _block_row_bcast(c_lo, m, m - 1)
            q_up = _split_halves(q, m)[1]
            k_lo, k_up = _split_halves(k, m)
            qt = (q_up * jnp.exp2(c_up)).astype(BF16)
            kt = _merge_halves(k_lo * jnp.exp2(tot - c_lo), k_up, m).astype(BF16)
            s_lo, s_up = _split_halves(scores, m)
            s_up = jnp.where(masks[j], _dot_nt(qt, kt), s_up)
            scores = _merge_halves(s_lo, s_up, m)
            c = _merge_halves(c_lo, c_up + tot, m)
    if mid_hook is not None:
        mid_hook()
    diag = jnp.sum(q * k, axis=-1, keepdims=True) * v
    if transposed:
        vt = v.astype(BF16).T
        return _dot_nt(vt, scores.astype(BF16)), vt, diag, c
    return _dot(scores.astype(BF16), v.astype(BF16)) + diag, c


def _ret_mask(rows, seg, log_gamma):
    r = lax.broadcasted_iota(jnp.int32, (rows, rows), 0)
    s = lax.broadcasted_iota(jnp.int32, (rows, rows), 1)
    valid = jnp.logical_and(((r ^ s) >> int(math.log2(seg))) == 0, r >= s)
    d = jnp.where(valid, r - s, 0).astype(F32)
    return jnp.where(valid, jnp.exp(d * log_gamma) * QK_SCALE, 0.0)


def _log_gamma(hd):
    return math.log(1.0 - 2.0 ** (-5.0 - hd))


def _head_out(o, w, gate):
    return o * lax.rsqrt(jnp.mean(o * o, axis=-1, keepdims=True) + NORM_EPS) * w * _silu(gate)


def _project(x_ref, nw_ref, win_ref, proj_ref):
    h = _rms(x_ref[...], nw_ref[...]).astype(BF16)
    for g in range(8):
        cols = slice(g * GROUP_W, (g + 1) * GROUP_W)
        proj_ref[:, cols] = _dot(h, win_ref[:, cols])


def _cast_weight(w_hbm, w_bf_ref, stage_ref, sem):
    nbuf, brows, bcols = stage_ref.shape
    rows, width = w_hbm.shape
    k = width // bcols
    crows = brows // k
    n = rows // crows

    def copies(i):
        return [pltpu.make_async_copy(
            w_hbm.at[pl.ds(i * crows, crows), pl.ds(c * bcols, bcols)],
            stage_ref.at[i % nbuf, pl.ds(c * crows, crows), :],
            sem.at[i % nbuf, c]) for c in range(k)]

    for i in range(min(nbuf - 1, n)):
        for cp in copies(i):
            cp.start()
    for i in range(n):
        if i + nbuf - 1 < n:
            for cp in copies(i + nbuf - 1):
                cp.start()
        for c, cp in enumerate(copies(i)):
            cp.wait()
            w_bf_ref[pl.ds(i * crows, crows), pl.ds(c * bcols, bcols)] = (
                stage_ref[i % nbuf, c * crows:(c + 1) * crows, :].astype(BF16))


def _cols(group, hd):
    return slice(group * GROUP_W + hd * DH, group * GROUP_W + (hd + 1) * DH)


def _mix_prompt_kernel(nc, *refs):
    CH = PROMPT_STEP_CHUNKS
    L = PROMPT_CHUNK
    x0_ref = refs[0]
    xn_refs = refs[1:1 + CH]
    (xc_ref, nw_ref, win_hbm, cos_ref, sin_ref, rnw_ref, hnw_ref, lbl_ref, wout_hbm,
     wup_f32_ref, wdn_f32_ref,
     x1_ref, sret_ref, shg_ref, win_ref, wout_ref, wup_bf_ref, wdn_bf_ref,
     proj_ref, o_ref, srt_ref, sht_ref, dm_ref, qd_ref, kd_ref, stage_ref, sem) = refs[1 + CH:]
    g = pl.program_id(0)
    ci0 = (g * CH) % nc
    t = lax.broadcasted_iota(jnp.int32, (L, DH), 0)

    @pl.when(g == 0)
    def _():
        _cast_weight(win_hbm, win_ref, stage_ref, sem)
        _cast_weight(wout_hbm, wout_ref, stage_ref, sem)
        tf = t.astype(F32)
        for hd in range(HEADS):
            lg = _log_gamma(hd)
            dm_ref[hd] = _ret_mask(L, L, lg)
            qd_ref[hd] = jnp.exp((tf + 1.0) * lg)
            kd_ref[hd] = jnp.exp((L - 1.0 - tf) * lg) * QK_SCALE
        _project(x0_ref, nw_ref, win_ref, proj_ref.at[0])

    @pl.when(ci0 == 0)
    def _():
        srt_ref[...] = jnp.zeros_like(srt_ref)
        sht_ref[...] = jnp.zeros_like(sht_ref)

    wup_bf_ref[...] = wup_f32_ref[...].astype(BF16)
    wdn_bf_ref[...] = wdn_f32_ref[...].astype(BF16)

    lb = _lower_bound(lbl_ref[...])
    masks = _hgrn_level_masks(L, L)

    def chunk(pr, pw, xn_ref, rows):
        h_next = _rms(xn_ref[...], nw_ref[...]).astype(BF16)

        def project_piece(pi):
            cols = slice(pi * PROJ_PIECE, (pi + 1) * PROJ_PIECE)
            pw[:, cols] = _dot(h_next, win_ref[:, cols])

        cosf = cos_ref[rows, :]
        sinf = sin_ref[rows, :]

        for hd in range(HEADS):
            q = _rotary(pr[:, _cols(0, hd)], cosf, sinf)
            k = _rotary(pr[:, _cols(1, hd)], cosf, sinf)
            v = pr[:, _cols(2, hd)].astype(BF16)
            scores = _dot_nt(q.astype(BF16), k.astype(BF16)) * dm_ref[hd]
            project_piece(hd)
            vt = v.T
            st = srt_ref[hd]
            ot = (_dot_nt(vt, scores.astype(BF16))
                  + _dot_nt(st.astype(BF16), (q * qd_ref[hd]).astype(BF16)))
            kh = (k * kd_ref[hd]).astype(BF16)
            srt_ref[hd] = st * math.exp(L * _log_gamma(hd)) + _dot(vt, kh)
            o_ref[:, _cols(0, hd)] = _head_out(ot.T, rnw_ref[...],
                                               pr[:, _cols(3, hd)]).astype(BF16)

        for hd in range(HEADS):
            q = _silu(pr[:, _cols(4, hd)]) * QK_SCALE
            lbh = lb[:, hd * DH:(hd + 1) * DH]
            f = lbh + (1.0 - lbh) * _sigmoid(pr[:, _cols(5, hd)])
            k = 1.0 - f
            v = pr[:, _cols(6, hd)]
            ot, vt, diag, b = _hgrn_intra(q, k, f, v, L, t, masks,
                                          functools.partial(project_piece, HEADS + hd), True)
            blast = b[L - 1:L, :]
            st = sht_ref[hd]
            ot = ot + _dot_nt(st.astype(BF16), (q * jnp.exp2(b)).astype(BF16))
            kh = (k * jnp.exp2(blast - b)).astype(BF16)
            sht_ref[hd] = st * jnp.exp2(blast) + _dot(vt, kh)
            o_ref[:, _cols(1, hd)] = _head_out(ot.T + diag, hnw_ref[...],
                                               pr[:, _cols(7, hd)]).astype(BF16)

        x1_ref[rows, :] = xc_ref[rows, :] + _dot(o_ref[...], wout_ref[...])

    for sub in range(CH):
        chunk(proj_ref.at[sub % 2], proj_ref.at[(sub + 1) % 2], xn_refs[sub],
              slice(sub * L, (sub + 1) * L))

    @pl.when(ci0 == nc - CH)
    def _():
        for hd in range(HEADS):
            sret_ref[hd] = srt_ref[hd].T
            shg_ref[hd] = sht_ref[hd].T


def _const_spec(shape):
    return pl.BlockSpec(shape, lambda *_: (0,) * len(shape), pipeline_mode=pl.Buffered(1))


def _mix_prompt(x, nw, win_f32, cosf, sinf, rnw, hnw, lbl, wout_f32, wup_f32, wdn_f32):
    B, T, D = x.shape
    L = PROMPT_CHUNK
    CH = PROMPT_STEP_CHUNKS
    nc = T // L
    n = B * nc
    spb = nc // CH
    assert nc % CH == 0 and CH % 2 == 0

    def next_chunk_spec(i):
        idx = lambda g: jnp.minimum(CH * g + i + 1, n - 1)
        return pl.BlockSpec((None, L, D), lambda g: (idx(g) // nc, idx(g) % nc, 0))

    state = jax.ShapeDtypeStruct((B, HEADS, DH, DH), F32)
    state_spec = pl.BlockSpec((None, HEADS, DH, DH), lambda g: (g // spb, 0, 0, 0))
    rows_spec = pl.BlockSpec((None, CH * L, D), lambda g: (g // spb, g % spb, 0))
    table_spec = pl.BlockSpec((CH * L, DH), lambda g: (g % spb, 0))
    steps = n // CH
    wup_spec = pl.BlockSpec((D // steps, D_FF), lambda g: (g, 0))
    wdn_spec = pl.BlockSpec((D_FF // steps, D), lambda g: (g, 0))
    return pl.pallas_call(
        functools.partial(_mix_prompt_kernel, nc),
        grid=(steps,),
        in_specs=[
            pl.BlockSpec((None, L, D), lambda g: (0, 0, 0), pipeline_mode=pl.Buffered(1)),
            *[next_chunk_spec(i) for i in range(CH)],
            rows_spec,
            _const_spec((1, D)),
            pl.BlockSpec(memory_space=pl.ANY),
            table_spec,
            table_spec,
            _const_spec((1, DH)),
            _const_spec((1, DH)),
            _const_spec(lbl.shape),
            pl.BlockSpec(memory_space=pl.ANY),
            wup_spec,
            wdn_spec,
        ],
        out_specs=[rows_spec, state_spec, state_spec,
                   pl.BlockSpec((D, IN_COLS), lambda g: (0, 0), pipeline_mode=pl.Buffered(1)),
                   pl.BlockSpec((D, D), lambda g: (0, 0), pipeline_mode=pl.Buffered(1)),
                   wup_spec, wdn_spec],
        out_shape=[jax.ShapeDtypeStruct((B, T, D), F32), state, state,
                   jax.ShapeDtypeStruct((D, IN_COLS), BF16), jax.ShapeDtypeStruct((D, D), BF16),
                   jax.ShapeDtypeStruct((D, D_FF), BF16), jax.ShapeDtypeStruct((D_FF, D), BF16)],
        scratch_shapes=[
            pltpu.VMEM((2, L, IN_COLS), F32),
            pltpu.VMEM((L, D), BF16),
            pltpu.VMEM((HEADS, DH, DH), F32),
            pltpu.VMEM((HEADS, DH, DH), F32),
            pltpu.VMEM((HEADS, L, L), F32),
            pltpu.VMEM((HEADS, L, DH), F32),
            pltpu.VMEM((HEADS, L, DH), F32),
            pltpu.VMEM((CAST_BUFFERS, CAST_ROWS, D), F32),
            pltpu.SemaphoreType.DMA((CAST_BUFFERS, IN_COLS // D)),
        ],
        compiler_params=pltpu.CompilerParams(
            dimension_semantics=("arbitrary",), vmem_limit_bytes=VMEM_LIMIT),
        name="mix_prompt",
    )(x, *([x] * CH), x, nw, win_f32, cosf, sinf, rnw, hnw, lbl, wout_f32, wup_f32, wdn_f32)


def _mix_sample_kernel(seg, x_ref, nw_ref, win_ref, cos_ref, sin_ref, rnw_ref, hnw_ref, lbl_ref,
                       wout_ref, sret_in_ref, shg_in_ref,
                       x1_ref, sret_ref, shg_ref,
                       proj_ref, o_ref, qh_ref, kh_ref, dec_ref, ob_ref):
    rows = x_ref.shape[0]
    nsub = sret_in_ref.shape[0]
    j = pl.program_id(1)

    @pl.when(j == 0)
    def _():
        _project(x_ref, nw_ref, win_ref, proj_ref)
        t = lax.broadcasted_iota(jnp.int32, (rows, DH), 0)
        tl = (t & (seg - 1)).astype(F32)
        cosf = cos_ref[...]
        sinf = sin_ref[...]
        lb = _lower_bound(lbl_ref[...])
        masks = _hgrn_level_masks(rows, seg)
        for hd in range(HEADS):
            lg = _log_gamma(hd)
            q = _rotary(proj_ref[:, _cols(0, hd)], cosf, sinf)
            k = _rotary(proj_ref[:, _cols(1, hd)], cosf, sinf)
            scores = _dot_nt(q.astype(BF16), k.astype(BF16)) * _ret_mask(rows, seg, lg)
            o_ref[:, _cols(0, hd)] = _dot(scores.astype(BF16),
                                          proj_ref[:, _cols(2, hd)].astype(BF16))
            qh_ref[:, _cols(0, hd)] = q * jnp.exp((tl + 1.0) * lg)
            kh_ref[:, _cols(0, hd)] = k * (jnp.exp((seg - 1.0 - tl) * lg) * QK_SCALE)
        for hd in range(HEADS):
            q = _silu(proj_ref[:, _cols(4, hd)]) * QK_SCALE
            lbh = lb[:, hd * DH:(hd + 1) * DH]
            f = lbh + (1.0 - lbh) * _sigmoid(proj_ref[:, _cols(5, hd)])
            k = 1.0 - f
            o, b = _hgrn_intra(q, k, f, proj_ref[:, _cols(6, hd)], seg, t, masks)
            blast = _block_row_bcast(b, seg, seg - 1)
            o_ref[:, _cols(1, hd)] = o
            qh_ref[:, _cols(1, hd)] = q * jnp.exp2(b)
            kh_ref[:, _cols(1, hd)] = k * jnp.exp2(blast - b)
            dec_ref[:, hd * DH:(hd + 1) * DH] = jnp.exp2(blast)

    def seq_body(s, carry):
        r = pl.ds(pl.multiple_of((j * nsub + s) * seg, seg), seg)
        qh = qh_ref[r, :].astype(BF16)
        kh = kh_ref[r, :].astype(BF16)
        v_ret = proj_ref[r, 2 * GROUP_W:3 * GROUP_W].astype(BF16)
        v_hg = proj_ref[r, 6 * GROUP_W:7 * GROUP_W].astype(BF16)
        dec = dec_ref[r, :][0:1, :]
        o_parts, new_ret, new_hg = [], [], []
        for hd in range(HEADS):
            st = sret_in_ref[s, hd]
            c = slice(hd * DH, (hd + 1) * DH)
            o_parts.append(_dot(qh[:, c], st.astype(BF16)))
            new_ret.append(st * math.exp(seg * _log_gamma(hd)) + _dot_tn(kh[:, c], v_ret[:, c]))
        for hd in range(HEADS):
            st = shg_in_ref[s, hd]
            c = slice(hd * DH, (hd + 1) * DH)
            c1 = slice(GROUP_W + hd * DH, GROUP_W + (hd + 1) * DH)
            o_parts.append(_dot(qh[:, c1], st.astype(BF16)))
            dcol = jnp.broadcast_to(dec[:, c], (DH, DH)).T
            new_hg.append(st * dcol + _dot_tn(kh[:, c1], v_hg[:, c]))
        o_ref[r, :] = o_ref[r, :] + jnp.concatenate(o_parts, axis=1)
        for hd in range(HEADS):
            sret_ref[s, hd] = new_ret[hd]
            shg_ref[s, hd] = new_hg[hd]
        return carry

    lax.fori_loop(0, nsub, seq_body, 0, unroll=2)

    @pl.when(j == pl.num_programs(1) - 1)
    def _():
        for hd in range(HEADS):
            c0 = _cols(0, hd)
            ob_ref[:, c0] = _head_out(o_ref[:, c0], rnw_ref[...],
                                      proj_ref[:, _cols(3, hd)]).astype(BF16)
            c1 = _cols(1, hd)
            ob_ref[:, c1] = _head_out(o_ref[:, c1], hnw_ref[...],
                                      proj_ref[:, _cols(7, hd)]).astype(BF16)
        x1_ref[...] = x_ref[...] + _dot(ob_ref[...], wout_ref[...])


def _mix_sample(x, nw, win, cosf, sinf, rnw, hnw, lbl, wout, sret, shg):
    nb, seg, D = x.shape
    rows = SAMPLE_ROWS
    nsub = SAMPLE_SUB_SEQS
    nj = rows // seg // nsub
    x2 = x.reshape(nb * seg, D)
    state = jax.ShapeDtypeStruct((nb, HEADS, DH, DH), F32)
    state_map = lambda i, j: (i * nj + j, 0, 0, 0)
    state_spec = pl.BlockSpec((nsub, HEADS, DH, DH), state_map)
    x1, sret_new, shg_new = pl.pallas_call(
        functools.partial(_mix_sample_kernel, seg),
        grid=(nb * seg // rows, nj),
        in_specs=[
            pl.BlockSpec((rows, D), lambda i, j: (i, 0)),
            _const_spec((1, D)),
            _const_spec((D, IN_COLS)),
            _const_spec((rows, DH)),
            _const_spec((rows, DH)),
            _const_spec((1, DH)),
            _const_spec((1, DH)),
            _const_spec(lbl.shape),
            _const_spec((D, D)),
            state_spec,
            state_spec,
        ],
        out_specs=[pl.BlockSpec((rows, D), lambda i, j: (i, 0)), state_spec, state_spec],
        out_shape=[jax.ShapeDtypeStruct((nb * seg, D), F32), state, state],
        scratch_shapes=[
            pltpu.VMEM((rows, IN_COLS), F32),
            pltpu.VMEM((rows, D), F32),
            pltpu.VMEM((rows, D), F32),
            pltpu.VMEM((rows, D), F32),
            pltpu.VMEM((rows, GROUP_W), F32),
            pltpu.VMEM((rows, D), BF16),
        ],
        compiler_params=pltpu.CompilerParams(
            dimension_semantics=("arbitrary", "arbitrary"), vmem_limit_bytes=SAMPLE_VMEM_LIMIT),
        name="mix_sample",
    )(x2, nw, win, cosf, sinf, rnw, hnw, lbl, wout, sret, shg)
    return x1, sret_new, shg_new


def _ffn_kernel(x_ref, nw_ref, wup_ref, wdn_ref, fw_ref, y_ref):
    subs = [slice(r * FFN_SUB_ROWS, (r + 1) * FFN_SUB_ROWS)
            for r in range(x_ref.shape[0] // FFN_SUB_ROWS)]
    hs = [_rms(x_ref[rs, :], nw_ref[...]).astype(BF16) for rs in subs]
    accs = [x_ref[rs, :] for rs in subs]
    step = D_MODEL
    for g in range(D_FF // step):
        us = [jnp.maximum(_dot(h, wup_ref[:, g * step:(g + 1) * step]), 0.0) for h in hs]
        accs = [acc + _dot((u * u).astype(BF16), wdn_ref[g * step:(g + 1) * step, :])
                for acc, u in zip(accs, us)]
    for rs, acc in zip(subs, accs):
        y_ref[rs, :] = _rms(acc, fw_ref[...])


def _ffn_rows_kernel(n_prompt_steps, xp_ref, xs_ref, nw_ref, wup_ref, wdn_ref, fw_ref,
                     yp_ref, ys_ref):
    i = pl.program_id(0)

    @pl.when(i < n_prompt_steps)
    def _():
        _ffn_kernel(xp_ref, nw_ref, wup_ref, wdn_ref, fw_ref, yp_ref)

    @pl.when(i >= n_prompt_steps)
    def _():
        _ffn_kernel(xs_ref, nw_ref, wup_ref, wdn_ref, fw_ref, ys_ref)


def _ffn(xp, xs, nw, wup, wdn, fw):
    n_p, D = xp.shape
    n_s = xs.shape[0]
    rows = FFN_ROWS
    assert n_p % rows == 0 and n_s % rows == 0
    p_steps, s_steps = n_p // rows, n_s // rows
    p_spec = pl.BlockSpec((rows, D), lambda i: (jnp.minimum(i, p_steps - 1), 0))
    s_spec = pl.BlockSpec((rows, D), lambda i: (jnp.maximum(i - p_steps, 0), 0),
                          pipeline_mode=pl.Buffered(1))
    return pl.pallas_call(
        functools.partial(_ffn_rows_kernel, p_steps),
        grid=(p_steps + s_steps,),
        in_specs=[
            p_spec,
            s_spec,
            _const_spec((1, D)),
            _const_spec((D, D_FF)),
            _const_spec((D_FF, D)),
            _const_spec((1, D)),
        ],
        out_specs=[p_spec, s_spec],
        out_shape=[jax.ShapeDtypeStruct((n_p, D), F32), jax.ShapeDtypeStruct((n_s, D), F32)],
        compiler_params=pltpu.CompilerParams(
            dimension_semantics=("arbitrary",), vmem_limit_bytes=VMEM_LIMIT),
        name="ffn",
    )(xp, xs, nw, wup, wdn, fw)


def _rope_tables(pos):
    half = DH // 2
    inv_freq = ROPE_BASE ** (-jnp.arange(half, dtype=F32) / half)
    ang = pos.astype(F32)[:, None] * inv_freq[None, :]
    cos, sin = jnp.cos(ang), jnp.sin(ang)
    return jnp.concatenate([cos, cos], axis=-1), jnp.concatenate([-sin, sin], axis=-1)


def kernel(x_prompt, x_sample, state_ret, state_hgrn, norm_mix_w, w_in, ret_norm_w, hgrn_norm_w,
           lb_logits, w_out, norm_ffn_w, w_up, w_down, final_norm_w):
    B, T, D = x_prompt.shape
    nb, seg, _ = x_sample.shape
    assert w_in.shape[0] == 1, "one layer"

    nw = norm_mix_w[0].reshape(1, D)
    rnw = ret_norm_w[0].reshape(1, DH)
    hnw = hgrn_norm_w[0].reshape(1, DH)
    lbl = lb_logits.astype(F32)
    fnw = norm_ffn_w[0].reshape(1, D)
    fw = final_norm_w.reshape(1, D)

    cos_p, sin_p = _rope_tables(jnp.arange(T, dtype=jnp.int32))
    cos_s, sin_s = _rope_tables(PAST_LEN + jnp.arange(seg, dtype=jnp.int32))
    cos_s = jnp.tile(cos_s, (SAMPLE_ROWS // seg, 1))
    sin_s = jnp.tile(sin_s, (SAMPLE_ROWS // seg, 1))

    xp1, sret_p, shg_p, win, wout, wup, wdn = _mix_prompt(
        x_prompt, nw, w_in[0], cos_p, sin_p, rnw, hnw, lbl, w_out[0], w_up[0], w_down[0])
    xs1, sret_s, shg_s = _mix_sample(x_sample, nw, win, cos_s, sin_s, rnw, hnw, lbl, wout,
                                     state_ret[0], state_hgrn[0])

    y_p, y_s = _ffn(xp1.reshape(B * T, D), xs1, fnw, wup, wdn, fw)
    y_p = y_p.reshape(B, T, D)
    y_s = y_s.reshape(nb, seg, D)
    return (y_p, y_s, sret_p[None], shg_p[None], sret_s[None], shg_s[None])
```

```python
import functools
import math

import jax
import jax.numpy as jnp
from jax import lax
from jax.experimental import pallas as pl
from jax.experimental.pallas import tpu as pltpu

D_MODEL = 1024
HEADS = 4
DH = 128
GROUP_W = HEADS * DH
IN_COLS = 8 * GROUP_W
D_FF = 4 * D_MODEL
ROPE_BASE = 10000.0
NORM_EPS = 1e-6
QK_SCALE = DH ** -0.5
PAST_LEN = 16384

PROMPT_CHUNK = 256
PROMPT_STEP_CHUNKS = 2
SAMPLE_ROWS = 256
SAMPLE_SUB_SEQS = 16
FFN_ROWS = 1024
FFN_SUB_ROWS = 512
PROJ_PIECE = 512
CAST_BUFFERS = 4
CAST_ROWS = 512
VMEM_LIMIT = 56 * 1024 * 1024
SAMPLE_VMEM_LIMIT = 60 * 1024 * 1024

F32 = jnp.float32
BF16 = jnp.bfloat16


def _dot(a, b):
    return jnp.dot(a, b, preferred_element_type=F32)


def _dot_nt(a, b):
    return lax.dot_general(a, b, (((1,), (1,)), ((), ())), preferred_element_type=F32)


def _dot_tn(a, b):
    return lax.dot_general(a, b, (((0,), (0,)), ((), ())), preferred_element_type=F32)


def _rms(x, w):
    ms = jnp.mean(x * x, axis=-1, keepdims=True)
    return x * lax.rsqrt(ms + NORM_EPS) * w


def _sigmoid(x):
    return 0.5 * jnp.tanh(0.5 * x) + 0.5


def _silu(x):
    return x * _sigmoid(x)


def _lower_bound(lbl):
    mx = jnp.max(lbl, axis=0, keepdims=True)
    e = jnp.exp(lbl - mx)
    return e[0:1, :] / jnp.sum(e, axis=0, keepdims=True)


def _rotary(x, cosf, sinf):
    return x * cosf + pltpu.roll(x, DH // 2, 1) * sinf


def _split_halves(x, m):
    blk = 2 * m
    nb = x.shape[0] // blk
    lo = [x[b * blk:b * blk + m] for b in range(nb)]
    up = [x[b * blk + m:(b + 1) * blk] for b in range(nb)]
    cat = lambda ps: ps[0] if len(ps) == 1 else jnp.concatenate(ps, axis=0)
    return cat(lo), cat(up)


def _merge_halves(lo, up, m):
    nb = lo.shape[0] // m
    pieces = []
    for b in range(nb):
        pieces += [lo[b * m:(b + 1) * m], up[b * m:(b + 1) * m]]
    return jnp.concatenate(pieces, axis=0)


def _block_row_bcast(x, m, row):
    nb = x.shape[0] // m
    pieces = [jnp.broadcast_to(x[b * m + row:b * m + row + 1], (m, x.shape[1])) for b in range(nb)]
    return pieces[0] if nb == 1 else jnp.concatenate(pieces, axis=0)


def _hgrn_level_masks(rows, seg):
    masks = []
    for j in range(int(math.log2(seg))):
        if (1 << j) < 8:
            r = lax.broadcasted_iota(jnp.int32, (rows, rows), 0)
            s = lax.broadcasted_iota(jnp.int32, (rows, rows), 1)
            masks.append(jnp.logical_and(((r ^ s) >> j) == 1, r > s))
        else:
            r = lax.broadcasted_iota(jnp.int32, (rows // 2, rows), 0)
            s = lax.broadcasted_iota(jnp.int32, (rows // 2, rows), 1)
            masks.append((s >> j) == 2 * (r >> j))
    return masks


def _hgrn_intra(q, k, f, v, seg, t, masks, mid_hook=None, dot_nt=_dot_nt):
    rows = q.shape[0]
    c = jnp.log2(f)
    scores = None
    for j in range(int(math.log2(seg))):
        m = 1 << j
        if m < 8:
            c3 = c.reshape(rows // 8, 8, DH)
            upper = (t & m) != 0
            if m == 1:
                w = jnp.where(upper, f, 1.0)
                tot = pltpu.roll(c3, 1, 1).reshape(rows, DH)
            else:
                if m == 2:
                    low4 = ((t & 7) < 4).reshape(rows // 8, 8, DH)
                    tot3 = jnp.where(low4, jnp.broadcast_to(c3[:, 1:2, :], c3.shape),
                                     jnp.broadcast_to(c3[:, 5:6, :], c3.shape))
                else:
                    tot3 = jnp.broadcast_to(c3[:, 3:4, :], c3.shape)
                tot = tot3.reshape(rows, DH)
                w = jnp.exp2(jnp.where(upper, c, tot - c))
            part = dot_nt((q * w).astype(BF16), (k * w).astype(BF16))
            scores = jnp.where(masks[j], part, 0.0 if scores is None else scores)
            c = jnp.where(upper, c + tot, c)
        else:
            c_lo, c_up = _split_halves(c, m)
            tot = _block_row_bcast(c_lo, m, m - 1)
            q_up = _split_halves(q, m)[1]
            k_lo, k_up = _split_halves(k, m)
            qt = (q_up * jnp.exp2(c_up)).astype(BF16)
            kt = _merge_halves(k_lo * jnp.exp2(tot - c_lo), k_up, m).astype(BF16)
            s_lo, s_up = _split_halves(scores, m)
            s_up = jnp.where(masks[j], dot_nt(qt, kt), s_up)
            scores = _merge_halves(s_lo, s_up, m)
            c = _merge_halves(c_lo, c_up + tot, m)
    if mid_hook is not None:
        mid_hook()
    o = _dot(scores.astype(BF16), v.astype(BF16))
    o = o + jnp.sum(q * k, axis=-1, keepdims=True) * v
    return o, c


def _ret_mask(rows, seg, log_gamma):
    r = lax.broadcasted_iota(jnp.int32, (rows, rows), 0)
    s = lax.broadcasted_iota(jnp.int32, (rows, rows), 1)
    valid = jnp.logical_and(((r ^ s) >> int(math.log2(seg))) == 0, r >= s)
    d = jnp.where(valid, r - s, 0).astype(F32)
    return jnp.where(valid, jnp.exp(d * log_gamma) * QK_SCALE, 0.0)


def _log_gamma(hd):
    return math.log(1.0 - 2.0 ** (-5.0 - hd))


def _head_out(o, w, gate):
    return o * lax.rsqrt(jnp.mean(o * o, axis=-1, keepdims=True) + NORM_EPS) * w * _silu(gate)


def _project(x_ref, nw_ref, win_ref, proj_ref):
    h = _rms(x_ref[...], nw_ref[...]).astype(BF16)
    for g in range(8):
        cols = slice(g * GROUP_W, (g + 1) * GROUP_W)
        proj_ref[:, cols] = _dot(h, win_ref[:, cols])


def _cast_weight(w_hbm, w_bf_ref, stage_ref, sem):
    nbuf, brows, bcols = stage_ref.shape
    rows, width = w_hbm.shape
    k = width // bcols
    crows = brows // k
    n = rows // crows

    def copies(i):
        return [pltpu.make_async_copy(
            w_hbm.at[pl.ds(i * crows, crows), pl.ds(c * bcols, bcols)],
            stage_ref.at[i % nbuf, pl.ds(c * crows, crows), :],
            sem.at[i % nbuf, c]) for c in range(k)]

    for i in range(min(nbuf - 1, n)):
        for cp in copies(i):
            cp.start()
    for i in range(n):
        if i + nbuf - 1 < n:
            for cp in copies(i + nbuf - 1):
                cp.start()
        for c, cp in enumerate(copies(i)):
            cp.wait()
            w_bf_ref[pl.ds(i * crows, crows), pl.ds(c * bcols, bcols)] = (
                stage_ref[i % nbuf, c * crows:(c + 1) * crows, :].astype(BF16))


def _cols(group, hd):
    return slice(group * GROUP_W + hd * DH, group * GROUP_W + (hd + 1) * DH)


def _mix_prompt_kernel(nc, *refs):
    CH = PROMPT_STEP_CHUNKS
    L = PROMPT_CHUNK
    x0_ref = refs[0]
    xn_refs = refs[1:1 + CH]
    (xc_ref, nw_ref, win_hbm, cos_ref, sin_ref, rnw_ref, hnw_ref, lbl_ref, wout_hbm,
     wup_f32_ref, wdn_f32_ref,
     x1_ref, sret_ref, shg_ref, win_ref, wout_ref, wup_bf_ref, wdn_bf_ref,
     proj_ref, o_ref, srt_ref, sht_ref, dm_ref, qd_ref, kd_ref, stage_ref, sem,
     tr_ref) = refs[1 + CH:]
    g = pl.program_id(0)
    ci0 = (g * CH) % nc
    t = lax.broadcasted_iota(jnp.int32, (L, DH), 0)

    @pl.when(g == 0)
    def _():
        _cast_weight(win_hbm, win_ref, stage_ref, sem)
        _cast_weight(wout_hbm, wout_ref, stage_ref, sem)
        tf = t.astype(F32)
        for hd in range(HEADS):
            lg = _log_gamma(hd)
            dm_ref[hd] = _ret_mask(L, L, lg)
            qd_ref[hd] = jnp.exp((tf + 1.0) * lg)
            kd_ref[hd] = jnp.exp((L - 1.0 - tf) * lg) * QK_SCALE
        _project(x0_ref, nw_ref, win_ref, proj_ref.at[0])

    @pl.when(ci0 == 0)
    def _():
        srt_ref[...] = jnp.zeros_like(srt_ref)
        sht_ref[...] = jnp.zeros_like(sht_ref)

    wup_bf_ref[...] = wup_f32_ref[...].astype(BF16)
    wdn_bf_ref[...] = wdn_f32_ref[...].astype(BF16)

    lb = _lower_bound(lbl_ref[...])
    masks = _hgrn_level_masks(L, L)
    tr_count = [0]

    def dot_nt_staged(a, b):
        slot = tr_count[0] % tr_ref.shape[0]
        tr_count[0] += 1
        n = b.shape[0]
        tr_ref[slot, :, 0:n] = b.T
        return _dot(a, tr_ref[slot, :, 0:n])

    def chunk(pr, pw, xn_ref, rows):
        h_next = _rms(xn_ref[...], nw_ref[...]).astype(BF16)

        def project_piece(pi):
            cols = slice(pi * PROJ_PIECE, (pi + 1) * PROJ_PIECE)
            pw[:, cols] = _dot(h_next, win_ref[:, cols])

        cosf = cos_ref[rows, :]
        sinf = sin_ref[rows, :]

        for hd in range(HEADS):
            q = _rotary(pr[:, _cols(0, hd)], cosf, sinf)
            k = _rotary(pr[:, _cols(1, hd)], cosf, sinf)
            v = pr[:, _cols(2, hd)].astype(BF16)
            scores = dot_nt_staged(q.astype(BF16), k.astype(BF16)) * dm_ref[hd]
            project_piece(hd)
            o = _dot(scores.astype(BF16), v)
            st = srt_ref[hd]
            o = o + _dot_nt((q * qd_ref[hd]).astype(BF16), st.astype(BF16))
            kh = (k * kd_ref[hd]).astype(BF16)
            srt_ref[hd] = st * math.exp(L * _log_gamma(hd)) + _dot_tn(v, kh)
            o_ref[:, _cols(0, hd)] = _head_out(o, rnw_ref[...], pr[:, _cols(3, hd)]).astype(BF16)

        for hd in range(HEADS):
            q = _silu(pr[:, _cols(4, hd)]) * QK_SCALE
            lbh = lb[:, hd * DH:(hd + 1) * DH]
            f = lbh + (1.0 - lbh) * _sigmoid(pr[:, _cols(5, hd)])
            k = 1.0 - f
            v = pr[:, _cols(6, hd)]
            o, b = _hgrn_intra(q, k, f, v, L, t, masks,
                               functools.partial(project_piece, HEADS + hd), dot_nt_staged)
            blast = b[L - 1:L, :]
            st = sht_ref[hd]
            o = o + _dot_nt((q * jnp.exp2(b)).astype(BF16), st.astype(BF16))
            kh = (k * jnp.exp2(blast - b)).astype(BF16)
            sht_ref[hd] = st * jnp.exp2(blast) + _dot_tn(v.astype(BF16), kh)
            o_ref[:, _cols(1, hd)] = _head_out(o, hnw_ref[...], pr[:, _cols(7, hd)]).astype(BF16)

        x1_ref[rows, :] = xc_ref[rows, :] + _dot(o_ref[...], wout_ref[...])

    for sub in range(CH):
        chunk(proj_ref.at[sub % 2], proj_ref.at[(sub + 1) % 2], xn_refs[sub],
              slice(sub * L, (sub + 1) * L))

    @pl.when(ci0 == nc - CH)
    def _():
        for hd in range(HEADS):
            sret_ref[hd] = srt_ref[hd].T
            shg_ref[hd] = sht_ref[hd].T


def _const_spec(shape):
    return pl.BlockSpec(shape, lambda *_: (0,) * len(shape), pipeline_mode=pl.Buffered(1))


def _mix_prompt(x, nw, win_f32, cosf, sinf, rnw, hnw, lbl, wout_f32, wup_f32, wdn_f32):
    B, T, D = x.shape
    L = PROMPT_CHUNK
    CH = PROMPT_STEP_CHUNKS
    nc = T // L
    n = B * nc
    spb = nc // CH
    assert nc % CH == 0 and CH % 2 == 0

    def next_chunk_spec(i):
        idx = lambda g: jnp.minimum(CH * g + i + 1, n - 1)
        return pl.BlockSpec((None, L, D), lambda g: (idx(g) // nc, idx(g) % nc, 0))

    state = jax.ShapeDtypeStruct((B, HEADS, DH, DH), F32)
    state_spec = pl.BlockSpec((None, HEADS, DH, DH), lambda g: (g // spb, 0, 0, 0))
    rows_spec = pl.BlockSpec((None, CH * L, D), lambda g: (g // spb, g % spb, 0))
    table_spec = pl.BlockSpec((CH * L, DH), lambda g: (g % spb, 0))
    steps = n // CH
    wup_spec = pl.BlockSpec((D // steps, D_FF), lambda g: (g, 0))
    wdn_spec = pl.BlockSpec((D_FF // steps, D), lambda g: (g, 0))
    return pl.pallas_call(
        functools.partial(_mix_prompt_kernel, nc),
        grid=(steps,),
        in_specs=[
            pl.BlockSpec((None, L, D), lambda g: (0, 0, 0), pipeline_mode=pl.Buffered(1)),
            *[next_chunk_spec(i) for i in range(CH)],
            rows_spec,
            _const_spec((1, D)),
            pl.BlockSpec(memory_space=pl.ANY),
            table_spec,
            table_spec,
            _const_spec((1, DH)),
            _const_spec((1, DH)),
            _const_spec(lbl.shape),
            pl.BlockSpec(memory_space=pl.ANY),
            wup_spec,
            wdn_spec,
        ],
        out_specs=[rows_spec, state_spec, state_spec,
                   pl.BlockSpec((D, IN_COLS), lambda g: (0, 0), pipeline_mode=pl.Buffered(1)),
                   pl.BlockSpec((D, D), lambda g: (0, 0), pipeline_mode=pl.Buffered(1)),
                   wup_spec, wdn_spec],
        out_shape=[jax.ShapeDtypeStruct((B, T, D), F32), state, state,
                   jax.ShapeDtypeStruct((D, IN_COLS), BF16), jax.ShapeDtypeStruct((D, D), BF16),
                   jax.ShapeDtypeStruct((D, D_FF), BF16), jax.ShapeDtypeStruct((D_FF, D), BF16)],
        scratch_shapes=[
            pltpu.VMEM((2, L, IN_COLS), F32),
            pltpu.VMEM((L, D), BF16),
            pltpu.VMEM((HEADS, DH, DH), F32),
            pltpu.VMEM((HEADS, DH, DH), F32),
            pltpu.VMEM((HEADS, L, L), F32),
            pltpu.VMEM((HEADS, L, DH), F32),
            pltpu.VMEM((HEADS, L, DH), F32),
            pltpu.VMEM((CAST_BUFFERS, CAST_ROWS, D), F32),
            pltpu.SemaphoreType.DMA((CAST_BUFFERS, IN_COLS // D)),
            pltpu.VMEM((4, DH, L), BF16),
        ],
        compiler_params=pltpu.CompilerParams(
            dimension_semantics=("arbitrary",), vmem_limit_bytes=SAMPLE_VMEM_LIMIT),
        name="mix_prompt",
    )(x, *([x] * CH), x, nw, win_f32, cosf, sinf, rnw, hnw, lbl, wout_f32, wup_f32, wdn_f32)


def _mix_sample_kernel(seg, x_ref, nw_ref, win_ref, cos_ref, sin_ref, rnw_ref, hnw_ref, lbl_ref,
                       wout_ref, sret_in_ref, shg_in_ref,
                       x1_ref, sret_ref, shg_ref,
                       proj_ref, o_ref, qh_ref, kh_ref, dec_ref, ob_ref):
    rows = x_ref.shape[0]
    nsub = sret_in_ref.shape[0]
    j = pl.program_id(1)

    @pl.when(j == 0)
    def _():
        _project(x_ref, nw_ref, win_ref, proj_ref)
        t = lax.broadcasted_iota(jnp.int32, (rows, DH), 0)
        tl = (t & (seg - 1)).astype(F32)
        cosf = cos_ref[...]
        sinf = sin_ref[...]
        lb = _lower_bound(lbl_ref[...])
        masks = _hgrn_level_masks(rows, seg)
        for hd in range(HEADS):
            lg = _log_gamma(hd)
            q = _rotary(proj_ref[:, _cols(0, hd)], cosf, sinf)
            k = _rotary(proj_ref[:, _cols(1, hd)], cosf, sinf)
            scores = _dot_nt(q.astype(BF16), k.astype(BF16)) * _ret_mask(rows, seg, lg)
            o_ref[:, _cols(0, hd)] = _dot(scores.astype(BF16),
                                          proj_ref[:, _cols(2, hd)].astype(BF16))
            qh_ref[:, _cols(0, hd)] = q * jnp.exp((tl + 1.0) * lg)
            kh_ref[:, _cols(0, hd)] = k * (jnp.exp((seg - 1.0 - tl) * lg) * QK_SCALE)
        for hd in range(HEADS):
            q = _silu(proj_ref[:, _cols(4, hd)]) * QK_SCALE
            lbh = lb[:, hd * DH:(hd + 1) * DH]
            f = lbh + (1.0 - lbh) * _sigmoid(proj_ref[:, _cols(5, hd)])
            k = 1.0 - f
            o, b = _hgrn_intra(q, k, f, proj_ref[:, _cols(6, hd)], seg, t, masks)
            blast = _block_row_bcast(b, seg, seg - 1)
            o_ref[:, _cols(1, hd)] = o
            qh_ref[:, _cols(1, hd)] = q * jnp.exp2(b)
            kh_ref[:, _cols(1, hd)] = k * jnp.exp2(blast - b)
            dec_ref[:, hd * DH:(hd + 1) * DH] = jnp.exp2(blast)

    def seq_body(s, carry):
        r = pl.ds(pl.multiple_of((j * nsub + s) * seg, seg), seg)
        qh = qh_ref[r, :].astype(BF16)
        kh = kh_ref[r, :].astype(BF16)
        v_ret = proj_ref[r, 2 * GROUP_W:3 * GROUP_W].astype(BF16)
        v_hg = proj_ref[r, 6 * GROUP_W:7 * GROUP_W].astype(BF16)
        dec = dec_ref[r, :][0:1, :]
        o_parts, new_ret, new_hg = [], [], []
        for hd in range(HEADS):
            st = sret_in_ref[s, hd]
            c = slice(hd * DH, (hd + 1) * DH)
            o_parts.append(_dot(qh[:, c], st.astype(BF16)))
            new_ret.append(st * math.exp(seg * _log_gamma(hd)) + _dot_tn(kh[:, c], v_ret[:, c]))
        for hd in range(HEADS):
            st = shg_in_ref[s, hd]
            c = slice(hd * DH, (hd + 1) * DH)
            c1 = slice(GROUP_W + hd * DH, GROUP_W + (hd + 1) * DH)
            o_parts.append(_dot(qh[:, c1], st.astype(BF16)))
            dcol = jnp.broadcast_to(dec[:, c], (DH, DH)).T
            new_hg.append(st * dcol + _dot_tn(kh[:, c1], v_hg[:, c]))
        o_ref[r, :] = o_ref[r, :] + jnp.concatenate(o_parts, axis=1)
        for hd in range(HEADS):
            sret_ref[s, hd] = new_ret[hd]
            shg_ref[s, hd] = new_hg[hd]
        return carry

    lax.fori_loop(0, nsub, seq_body, 0, unroll=2)

    @pl.when(j == pl.num_programs(1) - 1)
    def _():
        for hd in range(HEADS):
            c0 = _cols(0, hd)
            ob_ref[:, c0] = _head_out(o_ref[:, c0], rnw_ref[...],
                                      proj_ref[:, _cols(3, hd)]).astype(BF16)
            c1 = _cols(1, hd)
            ob_ref[:, c1] = _head_out(o_ref[:, c1], hnw_ref[...],
                                      proj_ref[:, _cols(7, hd)]).astype(BF16)
        x1_ref[...] = x_ref[...] + _dot(ob_ref[...], wout_ref[...])


def _mix_sample(x, nw, win, cosf, sinf, rnw, hnw, lbl, wout, sret, shg):
    nb, seg, D = x.shape
    rows = SAMPLE_ROWS
    nsub = SAMPLE_SUB_SEQS
    nj = rows // seg // nsub
    x2 = x.reshape(nb * seg, D)
    state = jax.ShapeDtypeStruct((nb, HEADS, DH, DH), F32)
    state_map = lambda i, j: (i * nj + j, 0, 0, 0)
    state_spec = pl.BlockSpec((nsub, HEADS, DH, DH), state_map)
    x1, sret_new, shg_new = pl.pallas_call(
        functools.partial(_mix_sample_kernel, seg),
        grid=(nb * seg // rows, nj),
        in_specs=[
            pl.BlockSpec((rows, D), lambda i, j: (i, 0)),
            _const_spec((1, D)),
            _const_spec((D, IN_COLS)),
            _const_spec((rows, DH)),
            _const_spec((rows, DH)),
            _const_spec((1, DH)),
            _const_spec((1, DH)),
            _const_spec(lbl.shape),
            _const_spec((D, D)),
            state_spec,
            state_spec,
        ],
        out_specs=[pl.BlockSpec((rows, D), lambda i, j: (i, 0)), state_spec, state_spec],
        out_shape=[jax.ShapeDtypeStruct((nb * seg, D), F32), state, state],
        scratch_shapes=[
            pltpu.VMEM((rows, IN_COLS), F32),
            pltpu.VMEM((rows, D), F32),
            pltpu.VMEM((rows, D), F32),
            pltpu.VMEM((rows, D), F32),
            pltpu.VMEM((rows, GROUP_W), F32),
            pltpu.VMEM((rows, D), BF16),
        ],
        compiler_params=pltpu.CompilerParams(
            dimension_semantics=("arbitrary", "arbitrary"), vmem_limit_bytes=SAMPLE_VMEM_LIMIT),
        name="mix_sample",
    )(x2, nw, win, cosf, sinf, rnw, hnw, lbl, wout, sret, shg)
    return x1, sret_new, shg_new


def _ffn_kernel(x_ref, nw_ref, wup_ref, wdn_ref, fw_ref, y_ref):
    subs = [slice(r * FFN_SUB_ROWS, (r + 1) * FFN_SUB_ROWS)
            for r in range(x_ref.shape[0] // FFN_SUB_ROWS)]
    hs = [_rms(x_ref[rs, :], nw_ref[...]).astype(BF16) for rs in subs]
    accs = [x_ref[rs, :] for rs in subs]
    step = D_MODEL
    for g in range(D_FF // step):
        us = [jnp.maximum(_dot(h, wup_ref[:, g * step:(g + 1) * step]), 0.0) for h in hs]
        accs = [acc + _dot((u * u).astype(BF16), wdn_ref[g * step:(g + 1) * step, :])
                for acc, u in zip(accs, us)]
    for rs, acc in zip(subs, accs):
        y_ref[rs, :] = _rms(acc, fw_ref[...])


def _ffn_rows_kernel(n_prompt_steps, xp_ref, xs_ref, nw_ref, wup_ref, wdn_ref, fw_ref,
                     yp_ref, ys_ref):
    i = pl.program_id(0)

    @pl.when(i < n_prompt_steps)
    def _():
        _ffn_kernel(xp_ref, nw_ref, wup_ref, wdn_ref, fw_ref, yp_ref)

    @pl.when(i >= n_prompt_steps)
    def _():
        _ffn_kernel(xs_ref, nw_ref, wup_ref, wdn_ref, fw_ref, ys_ref)


def _ffn(xp, xs, nw, wup, wdn, fw):
    n_p, D = xp.shape
    n_s = xs.shape[0]
    rows = FFN_ROWS
    assert n_p % rows == 0 and n_s % rows == 0
    p_steps, s_steps = n_p // rows, n_s // rows
    p_spec = pl.BlockSpec((rows, D), lambda i: (jnp.minimum(i, p_steps - 1), 0))
    s_spec = pl.BlockSpec((rows, D), lambda i: (jnp.maximum(i - p_steps, 0), 0),
                          pipeline_mode=pl.Buffered(1))
    return pl.pallas_call(
        functools.partial(_ffn_rows_kernel, p_steps),
        grid=(p_steps + s_steps,),
        in_specs=[
            p_spec,
            s_spec,
            _const_spec((1, D)),
            _const_spec((D, D_FF)),
            _const_spec((D_FF, D)),
            _const_spec((1, D)),
        ],
        out_specs=[p_spec, s_spec],
        out_shape=[jax.ShapeDtypeStruct((n_p, D), F32), jax.ShapeDtypeStruct((n_s, D), F32)],
        compiler_params=pltpu.CompilerParams(
            dimension_semantics=("arbitrary",), vmem_limit_bytes=VMEM_LIMIT),
        name="ffn",
    )(xp, xs, nw, wup, wdn, fw)


def _rope_tables(pos):
    half = DH // 2
    inv_freq = ROPE_BASE ** (-jnp.arange(half, dtype=F32) / half)
    ang = pos.astype(F32)[:, None] * inv_freq[None, :]
    cos, sin = jnp.cos(ang), jnp.sin(ang)
    return jnp.concatenate([cos, cos], axis=-1), jnp.concatenate([-sin, sin], axis=-1)


def kernel(x_prompt, x_sample, state_ret, state_hgrn, norm_mix_w, w_in, ret_norm_w, hgrn_norm_w,
           lb_logits, w_out, norm_ffn_w, w_up, w_down, final_norm_w):
    B, T, D = x_prompt.shape
    nb, seg, _ = x_sample.shape
    assert w_in.shape[0] == 1, "one layer"

    nw = norm_mix_w[0].reshape(1, D)
    rnw = ret_norm_w[0].reshape(1, DH)
    hnw = hgrn_norm_w[0].reshape(1, DH)
    lbl = lb_logits.astype(F32)
    fnw = norm_ffn_w[0].reshape(1, D)
    fw = final_norm_w.reshape(1, D)

    cos_p, sin_p = _rope_tables(jnp.arange(T, dtype=jnp.int32))
    cos_s, sin_s = _rope_tables(PAST_LEN + jnp.arange(seg, dtype=jnp.int32))
    cos_s = jnp.tile(cos_s, (SAMPLE_ROWS // seg, 1))
    sin_s = jnp.tile(sin_s, (SAMPLE_ROWS // seg, 1))

    xp1, sret_p, shg_p, win, wout, wup, wdn = _mix_prompt(
        x_prompt, nw, w_in[0], cos_p, sin_p, rnw, hnw, lbl, w_out[0], w_up[0], w_down[0])
    xs1, sret_s, shg_s = _mix_sample(x_sample, nw, win, cos_s, sin_s, rnw, hnw, lbl, wout,
                                     state_ret[0], state_hgrn[0])

    y_p, y_s = _ffn(xp1.reshape(B * T, D), xs1, fnw, wup, wdn, fw)
    y_p = y_p.reshape(B, T, D)
    y_s = y_s.reshape(nb, seg, D)
    return (y_p, y_s, sret_p[None], shg_p[None], sret_s[None], shg_s[None])
```

```python
import functools
import math

import jax
import jax.numpy as jnp
from jax import lax
from jax.experimental import pallas as pl
from jax.experimental.pallas import tpu as pltpu

D_MODEL = 1024
HEADS = 4
DH = 128
GROUP_W = HEADS * DH
IN_COLS = 8 * GROUP_W
D_FF = 4 * D_MODEL
ROPE_BASE = 10000.0
NORM_EPS = 1e-6
QK_SCALE = DH ** -0.5
PAST_LEN = 16384

PROMPT_CHUNK = 256
PROMPT_STEP_CHUNKS = 2
SAMPLE_ROWS = 256
SAMPLE_SUB_SEQS = 16
FFN_ROWS = 1024
FFN_SUB_ROWS = 512
PROJ_PIECE = 512
CAST_BUFFERS = 4
CAST_ROWS = 512
VMEM_LIMIT = 56 * 1024 * 1024
SAMPLE_VMEM_LIMIT = 60 * 1024 * 1024

F32 = jnp.float32
BF16 = jnp.bfloat16


def _dot(a, b):
    return jnp.dot(a, b, preferred_element_type=F32)


def _dot_nt(a, b):
    return lax.dot_general(a, b, (((1,), (1,)), ((), ())), preferred_element_type=F32)


def _dot_tn(a, b):
    return lax.dot_general(a, b, (((0,), (0,)), ((), ())), preferred_element_type=F32)


def _rms(x, w):
    ms = jnp.mean(x * x, axis=-1, keepdims=True)
    return x * lax.rsqrt(ms + NORM_EPS) * w


def _sigmoid(x):
    return 0.5 * jnp.tanh(0.5 * x) + 0.5


def _silu(x):
    return x * _sigmoid(x)


def _lower_bound(lbl):
    mx = jnp.max(lbl, axis=0, keepdims=True)
    e = jnp.exp(lbl - mx)
    return e[0:1, :] / jnp.sum(e, axis=0, keepdims=True)


def _rotary(x, cosf, sinf):
    return x * cosf + pltpu.roll(x, DH // 2, 1) * sinf


def _split_halves(x, m):
    blk = 2 * m
    nb = x.shape[0] // blk
    lo = [x[b * blk:b * blk + m] for b in range(nb)]
    up = [x[b * blk + m:(b + 1) * blk] for b in range(nb)]
    cat = lambda ps: ps[0] if len(ps) == 1 else jnp.concatenate(ps, axis=0)
    return cat(lo), cat(up)


def _merge_halves(lo, up, m):
    nb = lo.shape[0] // m
    pieces = []
    for b in range(nb):
        pieces += [lo[b * m:(b + 1) * m], up[b * m:(b + 1) * m]]
    return jnp.concatenate(pieces, axis=0)


def _block_row_bcast(x, m, row):
    nb = x.shape[0] // m
    pieces = [jnp.broadcast_to(x[b * m + row:b * m + row + 1], (m, x.shape[1])) for b in range(nb)]
    return pieces[0] if nb == 1 else jnp.concatenate(pieces, axis=0)


def _hgrn_level_masks(rows, seg):
    masks = []
    for j in range(int(math.log2(seg))):
        if (1 << j) < 8:
            r = lax.broadcasted_iota(jnp.int32, (rows, rows), 0)
            s = lax.broadcasted_iota(jnp.int32, (rows, rows), 1)
            masks.append(jnp.logical_and(((r ^ s) >> j) == 1, r > s))
        else:
            r = lax.broadcasted_iota(jnp.int32, (rows // 2, rows), 0)
            s = lax.broadcasted_iota(jnp.int32, (rows // 2, rows), 1)
            masks.append((s >> j) == 2 * (r >> j))
    return masks


def _hgrn_intra(q, k, f, v, seg, t, masks, mid_hook=None):
    rows = q.shape[0]
    c = jnp.log2(f)
    scores = None
    for j in range(int(math.log2(seg))):
        m = 1 << j
        if m < 8:
            c3 = c.reshape(rows // 8, 8, DH)
            upper = (t & m) != 0
            if m == 1:
                w = jnp.where(upper, f, 1.0)
                tot = pltpu.roll(c3, 1, 1).reshape(rows, DH)
            else:
                if m == 2:
                    low4 = ((t & 7) < 4).reshape(rows // 8, 8, DH)
                    tot3 = jnp.where(low4, jnp.broadcast_to(c3[:, 1:2, :], c3.shape),
                                     jnp.broadcast_to(c3[:, 5:6, :], c3.shape))
                else:
                    tot3 = jnp.broadcast_to(c3[:, 3:4, :], c3.shape)
                tot = tot3.reshape(rows, DH)
                w = jnp.exp2(jnp.where(upper, c, tot - c))
            part = _dot_nt((q * w).astype(BF16), (k * w).astype(BF16))
            scores = jnp.where(masks[j], part, 0.0 if scores is None else scores)
            c = jnp.where(upper, c + tot, c)
        else:
            c_lo, c_up = _split_halves(c, m)
            tot = _block_row_bcast(c_lo, m, m - 1)
            q_up = _split_halves(q, m)[1]
            k_lo, k_up = _split_halves(k, m)
            qt = (q_up * jnp.exp2(c_up)).astype(BF16)
            kt = _merge_halves(k_lo * jnp.exp2(tot - c_lo), k_up, m).astype(BF16)
            s_lo, s_up = _split_halves(scores, m)
            s_up = jnp.where(masks[j], _dot_nt(qt, kt), s_up)
            scores = _merge_halves(s_lo, s_up, m)
            c = _merge_halves(c_lo, c_up + tot, m)
    if mid_hook is not None:
        mid_hook()
    o = _dot(scores.astype(BF16), v.astype(BF16))
    o = o + jnp.sum(q * k, axis=-1, keepdims=True) * v
    return o, c


def _ret_mask(rows, seg, log_gamma):
    r = lax.broadcasted_iota(jnp.int32, (rows, rows), 0)
    s = lax.broadcasted_iota(jnp.int32, (rows, rows), 1)
    valid = jnp.logical_and(((r ^ s) >> int(math.log2(seg))) == 0, r >= s)
    d = jnp.where(valid, r - s, 0).astype(F32)
    return jnp.where(valid, jnp.exp(d * log_gamma) * QK_SCALE, 0.0)


def _log_gamma(hd):
    return math.log(1.0 - 2.0 ** (-5.0 - hd))


def _head_out(o, w, gate):
    return o * lax.rsqrt(jnp.mean(o * o, axis=-1, keepdims=True) + NORM_EPS) * w * _silu(gate)


def _project(x_ref, nw_ref, win_ref, proj_ref):
    h = _rms(x_ref[...], nw_ref[...]).astype(BF16)
    for g in range(8):
        cols = slice(g * GROUP_W, (g + 1) * GROUP_W)
        proj_ref[:, cols] = _dot(h, win_ref[:, cols])


def _cast_weight(w_hbm, w_bf_ref, stage_ref, sem):
    nbuf, brows, bcols = stage_ref.shape
    rows, width = w_hbm.shape
    k = width // bcols
    crows = brows // k
    n = rows // crows

    def copies(i):
        return [pltpu.make_async_copy(
            w_hbm.at[pl.ds(i * crows, crows), pl.ds(c * bcols, bcols)],
            stage_ref.at[i % nbuf, pl.ds(c * crows, crows), :],
            sem.at[i % nbuf, c]) for c in range(k)]

    for i in range(min(nbuf - 1, n)):
        for cp in copies(i):
            cp.start()
    for i in range(n):
        if i + nbuf - 1 < n:
            for cp in copies(i + nbuf - 1):
                cp.start()
        for c, cp in enumerate(copies(i)):
            cp.wait()
            w_bf_ref[pl.ds(i * crows, crows), pl.ds(c * bcols, bcols)] = (
                stage_ref[i % nbuf, c * crows:(c + 1) * crows, :].astype(BF16))


def _cols(group, hd):
    return slice(group * GROUP_W + hd * DH, group * GROUP_W + (hd + 1) * DH)


def _mix_prompt_kernel(nc, *refs):
    CH = PROMPT_STEP_CHUNKS
    L = PROMPT_CHUNK
    x0_ref = refs[0]
    xn_refs = refs[1:1 + CH]
    (xc_ref, nw_ref, win_hbm, cos_ref, sin_ref, rnw_ref, hnw_ref, lbl_ref, wout_hbm,
     wup_f32_ref, wdn_f32_ref,
     x1_ref, sret_ref, shg_ref, win_ref, wout_ref, wup_bf_ref, wdn_bf_ref,
     proj_ref, o_ref, srt_ref, sht_ref, dm_ref, qd_ref, kd_ref, stage_ref, sem) = refs[1 + CH:]
    g = pl.program_id(0)
    ci0 = (g * CH) % nc
    t = lax.broadcasted_iota(jnp.int32, (L, DH), 0)

    @pl.when(g == 0)
    def _():
        _cast_weight(win_hbm, win_ref, stage_ref, sem)
        _cast_weight(wout_hbm, wout_ref, stage_ref, sem)
        tf = t.astype(F32)
        for hd in range(HEADS):
            lg = _log_gamma(hd)
            dm_ref[hd] = _ret_mask(L, L, lg)
            qd_ref[hd] = jnp.exp((tf + 1.0) * lg)
            kd_ref[hd] = jnp.exp((L - 1.0 - tf) * lg) * QK_SCALE
        _project(x0_ref, nw_ref, win_ref, proj_ref.at[0])

    @pl.when(ci0 == 0)
    def _():
        srt_ref[...] = jnp.zeros_like(srt_ref)
        sht_ref[...] = jnp.zeros_like(sht_ref)

    wup_bf_ref[...] = wup_f32_ref[...].astype(BF16)
    wdn_bf_ref[...] = wdn_f32_ref[...].astype(BF16)

    lb = _lower_bound(lbl_ref[...])
    masks = _hgrn_level_masks(L, L)

    def chunk(pr, pw, xn_ref, rows):
        h_next = _rms(xn_ref[...], nw_ref[...]).astype(BF16)

        def project_piece(pi):
            cols = slice(pi * PROJ_PIECE, (pi + 1) * PROJ_PIECE)
            pw[:, cols] = _dot(h_next, win_ref[:, cols])

        cosf = cos_ref[rows, :]
        sinf = sin_ref[rows, :]

        def ret_head(hd):
            q = _rotary(pr[:, _cols(0, hd)], cosf, sinf)
            k = _rotary(pr[:, _cols(1, hd)], cosf, sinf)
            v = pr[:, _cols(2, hd)].astype(BF16)
            scores = _dot_nt(q.astype(BF16), k.astype(BF16)) * dm_ref[hd]
            project_piece(hd)
            o = _dot(scores.astype(BF16), v)
            st = srt_ref[hd]
            o = o + _dot_nt((q * qd_ref[hd]).astype(BF16), st.astype(BF16))
            kh = (k * kd_ref[hd]).astype(BF16)
            srt_ref[hd] = st * math.exp(L * _log_gamma(hd)) + _dot_tn(v, kh)
            o_ref[:, _cols(0, hd)] = _head_out(o, rnw_ref[...], pr[:, _cols(3, hd)]).astype(BF16)

        def hgrn_head(hd):
            q = _silu(pr[:, _cols(4, hd)]) * QK_SCALE
            lbh = lb[:, hd * DH:(hd + 1) * DH]
            f = lbh + (1.0 - lbh) * _sigmoid(pr[:, _cols(5, hd)])
            k = 1.0 - f
            v = pr[:, _cols(6, hd)]
            o, b = _hgrn_intra(q, k, f, v, L, t, masks,
                               functools.partial(project_piece, HEADS + hd))
            blast = b[L - 1:L, :]
            st = sht_ref[hd]
            o = o + _dot_nt((q * jnp.exp2(b)).astype(BF16), st.astype(BF16))
            kh = (k * jnp.exp2(blast - b)).astype(BF16)
            sht_ref[hd] = st * jnp.exp2(blast) + _dot_tn(v.astype(BF16), kh)
            o_ref[:, _cols(1, hd)] = _head_out(o, hnw_ref[...], pr[:, _cols(7, hd)]).astype(BF16)

        for hd in range(HEADS):
            ret_head(hd)
            hgrn_head(hd)

        x1_ref[rows, :] = xc_ref[rows, :] + _dot(o_ref[...], wout_ref[...])

    for sub in range(CH):
        chunk(proj_ref.at[sub % 2], proj_ref.at[(sub + 1) % 2], xn_refs[sub],
              slice(sub * L, (sub + 1) * L))

    @pl.when(ci0 == nc - CH)
    def _():
        for hd in range(HEADS):
            sret_ref[hd] = srt_ref[hd].T
            shg_ref[hd] = sht_ref[hd].T


def _const_spec(shape):
    return pl.BlockSpec(shape, lambda *_: (0,) * len(shape), pipeline_mode=pl.Buffered(1))


def _mix_prompt(x, nw, win_f32, cosf, sinf, rnw, hnw, lbl, wout_f32, wup_f32, wdn_f32):
    B, T, D = x.shape
    L = PROMPT_CHUNK
    CH = PROMPT_STEP_CHUNKS
    nc = T // L
    n = B * nc
    spb = nc // CH
    assert nc % CH == 0 and CH % 2 == 0

    def next_chunk_spec(i):
        idx = lambda g: jnp.minimum(CH * g + i + 1, n - 1)
        return pl.BlockSpec((None, L, D), lambda g: (idx(g) // nc, idx(g) % nc, 0))

    state = jax.ShapeDtypeStruct((B, HEADS, DH, DH), F32)
    state_spec = pl.BlockSpec((None, HEADS, DH, DH), lambda g: (g // spb, 0, 0, 0))
    rows_spec = pl.BlockSpec((None, CH * L, D), lambda g: (g // spb, g % spb, 0))
    table_spec = pl.BlockSpec((CH * L, DH), lambda g: (g % spb, 0))
    steps = n // CH
    wup_spec = pl.BlockSpec((D // steps, D_FF), lambda g: (g, 0))
    wdn_spec = pl.BlockSpec((D_FF // steps, D), lambda g: (g, 0))
    return pl.pallas_call(
        functools.partial(_mix_prompt_kernel, nc),
        grid=(steps,),
        in_specs=[
            pl.BlockSpec((None, L, D), lambda g: (0, 0, 0), pipeline_mode=pl.Buffered(1)),
            *[next_chunk_spec(i) for i in range(CH)],
            rows_spec,
            _const_spec((1, D)),
            pl.BlockSpec(memory_space=pl.ANY),
            table_spec,
            table_spec,
            _const_spec((1, DH)),
            _const_spec((1, DH)),
            _const_spec(lbl.shape),
            pl.BlockSpec(memory_space=pl.ANY),
            wup_spec,
            wdn_spec,
        ],
        out_specs=[rows_spec, state_spec, state_spec,
                   pl.BlockSpec((D, IN_COLS), lambda g: (0, 0), pipeline_mode=pl.Buffered(1)),
                   pl.BlockSpec((D, D), lambda g: (0, 0), pipeline_mode=pl.Buffered(1)),
                   wup_spec, wdn_spec],
        out_shape=[jax.ShapeDtypeStruct((B, T, D), F32), state, state,
                   jax.ShapeDtypeStruct((D, IN_COLS), BF16), jax.ShapeDtypeStruct((D, D), BF16),
                   jax.ShapeDtypeStruct((D, D_FF), BF16), jax.ShapeDtypeStruct((D_FF, D), BF16)],
        scratch_shapes=[
            pltpu.VMEM((2, L, IN_COLS), F32),
            pltpu.VMEM((L, D), BF16),
            pltpu.VMEM((HEADS, DH, DH), F32),
            pltpu.VMEM((HEADS, DH, DH), F32),
            pltpu.VMEM((HEADS, L, L), F32),
            pltpu.VMEM((HEADS, L, DH), F32),
            pltpu.VMEM((HEADS, L, DH), F32),
            pltpu.VMEM((CAST_BUFFERS, CAST_ROWS, D), F32),
            pltpu.SemaphoreType.DMA((CAST_BUFFERS, IN_COLS // D)),
        ],
        compiler_params=pltpu.CompilerParams(
            dimension_semantics=("arbitrary",), vmem_limit_bytes=VMEM_LIMIT),
        name="mix_prompt",
    )(x, *([x] * CH), x, nw, win_f32, cosf, sinf, rnw, hnw, lbl, wout_f32, wup_f32, wdn_f32)


def _mix_sample_kernel(seg, x_ref, nw_ref, win_ref, cos_ref, sin_ref, rnw_ref, hnw_ref, lbl_ref,
                       wout_ref, sret_in_ref, shg_in_ref,
                       x1_ref, sret_ref, shg_ref,
                       proj_ref, o_ref, qh_ref, kh_ref, dec_ref, ob_ref):
    rows = x_ref.shape[0]
    nsub = sret_in_ref.shape[0]
    j = pl.program_id(1)

    @pl.when(j == 0)
    def _():
        _project(x_ref, nw_ref, win_ref, proj_ref)
        t = lax.broadcasted_iota(jnp.int32, (rows, DH), 0)
        tl = (t & (seg - 1)).astype(F32)
        cosf = cos_ref[...]
        sinf = sin_ref[...]
        lb = _lower_bound(lbl_ref[...])
        masks = _hgrn_level_masks(rows, seg)
        for hd in range(HEADS):
            lg = _log_gamma(hd)
            q = _rotary(proj_ref[:, _cols(0, hd)], cosf, sinf)
            k = _rotary(proj_ref[:, _cols(1, hd)], cosf, sinf)
            scores = _dot_nt(q.astype(BF16), k.astype(BF16)) * _ret_mask(rows, seg, lg)
            o_ref[:, _cols(0, hd)] = _dot(scores.astype(BF16),
                                          proj_ref[:, _cols(2, hd)].astype(BF16))
            qh_ref[:, _cols(0, hd)] = q * jnp.exp((tl + 1.0) * lg)
            kh_ref[:, _cols(0, hd)] = k * (jnp.exp((seg - 1.0 - tl) * lg) * QK_SCALE)
        for hd in range(HEADS):
            q = _silu(proj_ref[:, _cols(4, hd)]) * QK_SCALE
            lbh = lb[:, hd * DH:(hd + 1) * DH]
            f = lbh + (1.0 - lbh) * _sigmoid(proj_ref[:, _cols(5, hd)])
            k = 1.0 - f
            o, b = _hgrn_intra(q, k, f, proj_ref[:, _cols(6, hd)], seg, t, masks)
            blast = _block_row_bcast(b, seg, seg - 1)
            o_ref[:, _cols(1, hd)] = o
            qh_ref[:, _cols(1, hd)] = q * jnp.exp2(b)
            kh_ref[:, _cols(1, hd)] = k * jnp.exp2(blast - b)
            dec_ref[:, hd * DH:(hd + 1) * DH] = jnp.exp2(blast)

    def seq_body(s, carry):
        r = pl.ds(pl.multiple_of((j * nsub + s) * seg, seg), seg)
        qh = qh_ref[r, :].astype(BF16)
        kh = kh_ref[r, :].astype(BF16)
        v_ret = proj_ref[r, 2 * GROUP_W:3 * GROUP_W].astype(BF16)
        v_hg = proj_ref[r, 6 * GROUP_W:7 * GROUP_W].astype(BF16)
        dec = dec_ref[r, :][0:1, :]
        o_parts, new_ret, new_hg = [], [], []
        for hd in range(HEADS):
            st = sret_in_ref[s, hd]
            c = slice(hd * DH, (hd + 1) * DH)
            o_parts.append(_dot(qh[:, c], st.astype(BF16)))
            new_ret.append(st * math.exp(seg * _log_gamma(hd)) + _dot_tn(kh[:, c], v_ret[:, c]))
        for hd in range(HEADS):
            st = shg_in_ref[s, hd]
            c = slice(hd * DH, (hd + 1) * DH)
            c1 = slice(GROUP_W + hd * DH, GROUP_W + (hd + 1) * DH)
            o_parts.append(_dot(qh[:, c1], st.astype(BF16)))
            dcol = jnp.broadcast_to(dec[:, c], (DH, DH)).T
            new_hg.append(st * dcol + _dot_tn(kh[:, c1], v_hg[:, c]))
        o_ref[r, :] = o_ref[r, :] + jnp.concatenate(o_parts, axis=1)
        for hd in range(HEADS):
            sret_ref[s, hd] = new_ret[hd]
            shg_ref[s, hd] = new_hg[hd]
        return carry

    lax.fori_loop(0, nsub, seq_body, 0, unroll=2)

    @pl.when(j == pl.num_programs(1) - 1)
    def _():
        for hd in range(HEADS):
            c0 = _cols(0, hd)
            ob_ref[:, c0] = _head_out(o_ref[:, c0], rnw_ref[...],
                                      proj_ref[:, _cols(3, hd)]).astype(BF16)
            c1 = _cols(1, hd)
            ob_ref[:, c1] = _head_out(o_ref[:, c1], hnw_ref[...],
                                      proj_ref[:, _cols(7, hd)]).astype(BF16)
        x1_ref[...] = x_ref[...] + _dot(ob_ref[...], wout_ref[...])


def _mix_sample(x, nw, win, cosf, sinf, rnw, hnw, lbl, wout, sret, shg):
    nb, seg, D = x.shape
    rows = SAMPLE_ROWS
    nsub = SAMPLE_SUB_SEQS
    nj = rows // seg // nsub
    x2 = x.reshape(nb * seg, D)
    state = jax.ShapeDtypeStruct((nb, HEADS, DH, DH), F32)
    state_map = lambda i, j: (i * nj + j, 0, 0, 0)
    state_spec = pl.BlockSpec((nsub, HEADS, DH, DH), state_map)
    x1, sret_new, shg_new = pl.pallas_call(
        functools.partial(_mix_sample_kernel, seg),
        grid=(nb * seg // rows, nj),
        in_specs=[
            pl.BlockSpec((rows, D), lambda i, j: (i, 0)),
            _const_spec((1, D)),
            _const_spec((D, IN_COLS)),
            _const_spec((rows, DH)),
            _const_spec((rows, DH)),
            _const_spec((1, DH)),
            _const_spec((1, DH)),
            _const_spec(lbl.shape),
            _const_spec((D, D)),
            state_spec,
            state_spec,
        ],
        out_specs=[pl.BlockSpec((rows, D), lambda i, j: (i, 0)), state_spec, state_spec],
        out_shape=[jax.ShapeDtypeStruct((nb * seg, D), F32), state, state],
        scratch_shapes=[
            pltpu.VMEM((rows, IN_COLS), F32),
            pltpu.VMEM((rows, D), F32),
            pltpu.VMEM((rows, D), F32),
            pltpu.VMEM((rows, D), F32),
            pltpu.VMEM((rows, GROUP_W), F32),
            pltpu.VMEM((rows, D), BF16),
        ],
        compiler_params=pltpu.CompilerParams(
            dimension_semantics=("arbitrary", "arbitrary"), vmem_limit_bytes=SAMPLE_VMEM_LIMIT),
        name="mix_sample",
    )(x2, nw, win, cosf, sinf, rnw, hnw, lbl, wout, sret, shg)
    return x1, sret_new, shg_new


def _ffn_kernel(x_ref, nw_ref, wup_ref, wdn_ref, fw_ref, y_ref):
    subs = [slice(r * FFN_SUB_ROWS, (r + 1) * FFN_SUB_ROWS)
            for r in range(x_ref.shape[0] // FFN_SUB_ROWS)]
    hs = [_rms(x_ref[rs, :], nw_ref[...]).astype(BF16) for rs in subs]
    accs = [x_ref[rs, :] for rs in subs]
    step = D_MODEL
    for g in range(D_FF // step):
        us = [jnp.maximum(_dot(h, wup_ref[:, g * step:(g + 1) * step]), 0.0) for h in hs]
        accs = [acc + _dot((u * u).astype(BF16), wdn_ref[g * step:(g + 1) * step, :])
                for acc, u in zip(accs, us)]
    for rs, acc in zip(subs, accs):
        y_ref[rs, :] = _rms(acc, fw_ref[...])


def _ffn_rows_kernel(n_prompt_steps, xp_ref, xs_ref, nw_ref, wup_ref, wdn_ref, fw_ref,
                     yp_ref, ys_ref):
    i = pl.program_id(0)

    @pl.when(i < n_prompt_steps)
    def _():
        _ffn_kernel(xp_ref, nw_ref, wup_ref, wdn_ref, fw_ref, yp_ref)

    @pl.when(i >= n_prompt_steps)
    def _():
        _ffn_kernel(xs_ref, nw_ref, wup_ref, wdn_ref, fw_ref, ys_ref)


def _ffn(xp, xs, nw, wup, wdn, fw):
    n_p, D = xp.shape
    n_s = xs.shape[0]
    rows = FFN_ROWS
    assert n_p % rows == 0 and n_s % rows == 0
    p_steps, s_steps = n_p // rows, n_s // rows
    p_spec = pl.BlockSpec((rows, D), lambda i: (jnp.minimum(i, p_steps - 1), 0))
    s_spec = pl.BlockSpec((rows, D), lambda i: (jnp.maximum(i - p_steps, 0), 0),
                          pipeline_mode=pl.Buffered(1))
    return pl.pallas_call(
        functools.partial(_ffn_rows_kernel, p_steps),
        grid=(p_steps + s_steps,),
        in_specs=[
            p_spec,
            s_spec,
            _const_spec((1, D)),
            _const_spec((D, D_FF)),
            _const_spec((D_FF, D)),
            _const_spec((1, D)),
        ],
        out_specs=[p_spec, s_spec],
        out_shape=[jax.ShapeDtypeStruct((n_p, D), F32), jax.ShapeDtypeStruct((n_s, D), F32)],
        compiler_params=pltpu.CompilerParams(
            dimension_semantics=("arbitrary",), vmem_limit_bytes=VMEM_LIMIT),
        name="ffn",
    )(xp, xs, nw, wup, wdn, fw)


def _rope_tables(pos):
    half = DH // 2
    inv_freq = ROPE_BASE ** (-jnp.arange(half, dtype=F32) / half)
    ang = pos.astype(F32)[:, None] * inv_freq[None, :]
    cos, sin = jnp.cos(ang), jnp.sin(ang)
    return jnp.concatenate([cos, cos], axis=-1), jnp.concatenate([-sin, sin], axis=-1)


def kernel(x_prompt, x_sample, state_ret, state_hgrn, norm_mix_w, w_in, ret_norm_w, hgrn_norm_w,
           lb_logits, w_out, norm_ffn_w, w_up, w_down, final_norm_w):
    B, T, D = x_prompt.shape
    nb, seg, _ = x_sample.shape
    assert w_in.shape[0] == 1, "one layer"

    nw = norm_mix_w[0].reshape(1, D)
    rnw = ret_norm_w[0].reshape(1, DH)
    hnw = hgrn_norm_w[0].reshape(1, DH)
    lbl = lb_logits.astype(F32)
    fnw = norm_ffn_w[0].reshape(1, D)
    fw = final_norm_w.reshape(1, D)

    cos_p, sin_p = _rope_tables(jnp.arange(T, dtype=jnp.int32))
    cos_s, sin_s = _rope_tables(PAST_LEN + jnp.arange(seg, dtype=jnp.int32))
    cos_s = jnp.tile(cos_s, (SAMPLE_ROWS // seg, 1))
    sin_s = jnp.tile(sin_s, (SAMPLE_ROWS // seg, 1))

    xp1, sret_p, shg_p, win, wout, wup, wdn = _mix_prompt(
        x_prompt, nw, w_in[0], cos_p, sin_p, rnw, hnw, lbl, w_out[0], w_up[0], w_down[0])
    xs1, sret_s, shg_s = _mix_sample(x_sample, nw, win, cos_s, sin_s, rnw, hnw, lbl, wout,
                                     state_ret[0], state_hgrn[0])

    y_p, y_s = _ffn(xp1.reshape(B * T, D), xs1, fnw, wup, wdn, fw)
    y_p = y_p.reshape(B, T, D)
    y_s = y_s.reshape(nb, seg, D)
    return (y_p, y_s, sret_p[None], shg_p[None], sret_s[None], shg_s[None])
```

```python
import functools
import math

import jax
import jax.numpy as jnp
from jax import lax
from jax.experimental import pallas as pl
from jax.experimental.pallas import tpu as pltpu

D_MODEL = 1024
HEADS = 4
DH = 128
GROUP_W = HEADS * DH
IN_COLS = 8 * GROUP_W
D_FF = 4 * D_MODEL
ROPE_BASE = 10000.0
NORM_EPS = 1e-6
QK_SCALE = DH ** -0.5
PAST_LEN = 16384

PROMPT_CHUNK = 256
PROMPT_STEP_CHUNKS = 2
SAMPLE_ROWS = 256
SAMPLE_SUB_SEQS = 16
FFN_ROWS = 1024
FFN_SUB_ROWS = 512
PROJ_PIECE = 512
CAST_BUFFERS = 4
CAST_ROWS = 512
VMEM_LIMIT = 56 * 1024 * 1024
SAMPLE_VMEM_LIMIT = 60 * 1024 * 1024

F32 = jnp.float32
BF16 = jnp.bfloat16


def _dot(a, b):
    return jnp.dot(a, b, preferred_element_type=F32)


def _dot_nt(a, b):
    return lax.dot_general(a, b, (((1,), (1,)), ((), ())), preferred_element_type=F32)


def _dot_tn(a, b):
    return lax.dot_general(a, b, (((0,), (0,)), ((), ())), preferred_element_type=F32)


def _rms(x, w):
    ms = jnp.mean(x * x, axis=-1, keepdims=True)
    return x * lax.rsqrt(ms + NORM_EPS) * w


def _sigmoid(x):
    return 0.5 * jnp.tanh(0.5 * x) + 0.5


def _silu(x):
    return x * _sigmoid(x)


def _lower_bound(lbl):
    mx = jnp.max(lbl, axis=0, keepdims=True)
    e = jnp.exp(lbl - mx)
    return e[0:1, :] / jnp.sum(e, axis=0, keepdims=True)


def _rotary(x, cosf, sinf):
    return x * cosf + pltpu.roll(x, DH // 2, 1) * sinf


def _split_halves(x, m):
    blk = 2 * m
    nb = x.shape[0] // blk
    lo = [x[b * blk:b * blk + m] for b in range(nb)]
    up = [x[b * blk + m:(b + 1) * blk] for b in range(nb)]
    cat = lambda ps: ps[0] if len(ps) == 1 else jnp.concatenate(ps, axis=0)
    return cat(lo), cat(up)


def _merge_halves(lo, up, m):
    nb = lo.shape[0] // m
    pieces = []
    for b in range(nb):
        pieces += [lo[b * m:(b + 1) * m], up[b * m:(b + 1) * m]]
    return jnp.concatenate(pieces, axis=0)


def _block_row_bcast(x, m, row):
    nb = x.shape[0] // m
    pieces = [jnp.broadcast_to(x[b * m + row:b * m + row + 1], (m, x.shape[1])) for b in range(nb)]
    return pieces[0] if nb == 1 else jnp.concatenate(pieces, axis=0)


def _hgrn_level_masks(rows, seg):
    masks = []
    for j in range(int(math.log2(seg))):
        if (1 << j) < 8:
            r = lax.broadcasted_iota(jnp.int32, (rows, rows), 0)
            s = lax.broadcasted_iota(jnp.int32, (rows, rows), 1)
            masks.append(jnp.logical_and(((r ^ s) >> j) == 1, r > s))
        else:
            r = lax.broadcasted_iota(jnp.int32, (rows // 2, rows), 0)
            s = lax.broadcasted_iota(jnp.int32, (rows // 2, rows), 1)
            masks.append((s >> j) == 2 * (r >> j))
    return masks


def _hgrn_intra(q, k, f, v, seg, t, masks, mid_hook=None):
    rows = q.shape[0]
    c = jnp.log2(f)
    scores = None
    for j in range(int(math.log2(seg))):
        m = 1 << j
        if m < 8:
            c3 = c.reshape(rows // 8, 8, DH)
            upper = (t & m) != 0
            if m == 1:
                w = jnp.where(upper, f, 1.0)
                tot = pltpu.roll(c3, 1, 1).reshape(rows, DH)
            else:
                if m == 2:
                    low4 = ((t & 7) < 4).reshape(rows // 8, 8, DH)
                    tot3 = jnp.where(low4, jnp.broadcast_to(c3[:, 1:2, :], c3.shape),
                                     jnp.broadcast_to(c3[:, 5:6, :], c3.shape))
                else:
                    tot3 = jnp.broadcast_to(c3[:, 3:4, :], c3.shape)
                tot = tot3.reshape(rows, DH)
                w = jnp.exp2(jnp.where(upper, c, tot - c))
            part = _dot_nt((q * w).astype(BF16), (k * w).astype(BF16))
            scores = jnp.where(masks[j], part, 0.0 if scores is None else scores)
            c = jnp.where(upper, c + tot, c)
        else:
            c_lo, c_up = _split_halves(c, m)
            tot = _block_row_bcast(c_lo, m, m - 1)
            q_up = _split_halves(q, m)[1]
            k_lo, k_up = _split_halves(k, m)
            qt = (q_up * jnp.exp2(c_up)).astype(BF16)
            kt = _merge_halves(k_lo * jnp.exp2(tot - c_lo), k_up, m).astype(BF16)
            s_lo, s_up = _split_halves(scores, m)
            s_up = jnp.where(masks[j], _dot_nt(qt, kt), s_up)
            scores = _merge_halves(s_lo, s_up, m)
            c = _merge_halves(c_lo, c_up + tot, m)
    if mid_hook is not None:
        mid_hook()
    o = _dot(scores.astype(BF16), v.astype(BF16))
    o = o + jnp.sum(q * k, axis=-1, keepdims=True) * v
    return o, c


def _ret_mask(rows, seg, log_gamma):
    r = lax.broadcasted_iota(jnp.int32, (rows, rows), 0)
    s = lax.broadcasted_iota(jnp.int32, (rows, rows), 1)
    valid = jnp.logical_and(((r ^ s) >> int(math.log2(seg))) == 0, r >= s)
    d = jnp.where(valid, r - s, 0).astype(F32)
    return jnp.where(valid, jnp.exp(d * log_gamma) * QK_SCALE, 0.0)


def _log_gamma(hd):
    return math.log(1.0 - 2.0 ** (-5.0 - hd))


def _head_out(o, w, gate):
    return o * lax.rsqrt(jnp.mean(o * o, axis=-1, keepdims=True) + NORM_EPS) * w * _silu(gate)


def _project(x_ref, nw_ref, win_ref, proj_ref):
    h = _rms(x_ref[...], nw_ref[...]).astype(BF16)
    for g in range(8):
        cols = slice(g * GROUP_W, (g + 1) * GROUP_W)
        proj_ref[:, cols] = _dot(h, win_ref[:, cols])


def _cast_weight(w_hbm, w_bf_ref, stage_ref, sem):
    nbuf, brows, bcols = stage_ref.shape
    rows, width = w_hbm.shape
    k = width // bcols
    crows = brows // k
    n = rows // crows

    def copies(i):
        return [pltpu.make_async_copy(
            w_hbm.at[pl.ds(i * crows, crows), pl.ds(c * bcols, bcols)],
            stage_ref.at[i % nbuf, pl.ds(c * crows, crows), :],
            sem.at[i % nbuf, c]) for c in range(k)]

    for i in range(min(nbuf - 1, n)):
        for cp in copies(i):
            cp.start()
    for i in range(n):
        if i + nbuf - 1 < n:
            for cp in copies(i + nbuf - 1):
                cp.start()
        for c, cp in enumerate(copies(i)):
            cp.wait()
            w_bf_ref[pl.ds(i * crows, crows), pl.ds(c * bcols, bcols)] = (
                stage_ref[i % nbuf, c * crows:(c + 1) * crows, :].astype(BF16))


def _cols(group, hd):
    return slice(group * GROUP_W + hd * DH, group * GROUP_W + (hd + 1) * DH)


def _mix_prompt_kernel(nc, *refs):
    CH = PROMPT_STEP_CHUNKS
    L = PROMPT_CHUNK
    x0_ref = refs[0]
    xn_refs = refs[1:1 + CH]
    (xc_ref, nw_ref, win_hbm, cos_ref, sin_ref, rnw_ref, hnw_ref, lbl_ref, wout_hbm,
     wup_f32_ref, wdn_f32_ref,
     x1_ref, sret_ref, shg_ref, win_ref, wout_ref, wup_bf_ref, wdn_bf_ref,
     proj_ref, o_ref, srt_ref, sht_ref, dm_ref, qd_ref, kd_ref, stage_ref, sem) = refs[1 + CH:]
    g = pl.program_id(0)
    ci0 = (g * CH) % nc
    t = lax.broadcasted_iota(jnp.int32, (L, DH), 0)

    @pl.when(g == 0)
    def _():
        _cast_weight(win_hbm, win_ref, stage_ref, sem)
        _cast_weight(wout_hbm, wout_ref, stage_ref, sem)
        tf = t.astype(F32)
        for hd in range(HEADS):
            lg = _log_gamma(hd)
            dm_ref[hd] = _ret_mask(L, L, lg)
            qd_ref[hd] = jnp.exp((tf + 1.0) * lg)
            kd_ref[hd] = jnp.exp((L - 1.0 - tf) * lg) * QK_SCALE
        _project(x0_ref, nw_ref, win_ref, proj_ref.at[0])

    @pl.when(ci0 == 0)
    def _():
        srt_ref[...] = jnp.zeros_like(srt_ref)
        sht_ref[...] = jnp.zeros_like(sht_ref)

    wup_bf_ref[...] = wup_f32_ref[...].astype(BF16)
    wdn_bf_ref[...] = wdn_f32_ref[...].astype(BF16)

    lb = _lower_bound(lbl_ref[...])
    masks = _hgrn_level_masks(L, L)

    def chunk(pr, pw, xn_ref, rows):
        h_next = _rms(xn_ref[...], nw_ref[...]).astype(BF16)

        def project_piece(pi):
            cols = slice(pi * PROJ_PIECE, (pi + 1) * PROJ_PIECE)
            pw[:, cols] = _dot(h_next, win_ref[:, cols])

        cosf = cos_ref[rows, :]
        sinf = sin_ref[rows, :]

        def ret_head(hd):
            q = _rotary(pr[:, _cols(0, hd)], cosf, sinf)
            k = _rotary(pr[:, _cols(1, hd)], cosf, sinf)
            v = pr[:, _cols(2, hd)].astype(BF16)
            scores = _dot_nt(q.astype(BF16), k.astype(BF16)) * dm_ref[hd]
            project_piece(hd)
            o = _dot(scores.astype(BF16), v)
            st = srt_ref[hd]
            o = o + _dot_nt((q * qd_ref[hd]).astype(BF16), st.astype(BF16))
            kh = (k * kd_ref[hd]).astype(BF16)
            srt_ref[hd] = st * math.exp(L * _log_gamma(hd)) + _dot_tn(v, kh)
            o_ref[:, _cols(0, hd)] = _head_out(o, rnw_ref[...], pr[:, _cols(3, hd)]).astype(BF16)

        def hgrn_head(hd):
            q = _silu(pr[:, _cols(4, hd)]) * QK_SCALE
            lbh = lb[:, hd * DH:(hd + 1) * DH]
            f = lbh + (1.0 - lbh) * _sigmoid(pr[:, _cols(5, hd)])
            k = 1.0 - f
            v = pr[:, _cols(6, hd)]
            o, b = _hgrn_intra(q, k, f, v, L, t, masks,
                               functools.partial(project_piece, HEADS + hd))
            blast = b[L - 1:L, :]
            st = sht_ref[hd]
            o = o + _dot_nt((q * jnp.exp2(b)).astype(BF16), st.astype(BF16))
            kh = (k * jnp.exp2(blast - b)).astype(BF16)
            sht_ref[hd] = st * jnp.exp2(blast) + _dot_tn(v.astype(BF16), kh)
            o_ref[:, _cols(1, hd)] = _head_out(o, hnw_ref[...], pr[:, _cols(7, hd)]).astype(BF16)

        for hd in range(HEADS):
            hgrn_head(hd)
            ret_head(hd)

        x1_ref[rows, :] = xc_ref[rows, :] + _dot(o_ref[...], wout_ref[...])

    for sub in range(CH):
        chunk(proj_ref.at[sub % 2], proj_ref.at[(sub + 1) % 2], xn_refs[sub],
              slice(sub * L, (sub + 1) * L))

    @pl.when(ci0 == nc - CH)
    def _():
        for hd in range(HEADS):
            sret_ref[hd] = srt_ref[hd].T
            shg_ref[hd] = sht_ref[hd].T


def _const_spec(shape):
    return pl.BlockSpec(shape, lambda *_: (0,) * len(shape), pipeline_mode=pl.Buffered(1))


def _mix_prompt(x, nw, win_f32, cosf, sinf, rnw, hnw, lbl, wout_f32, wup_f32, wdn_f32):
    B, T, D = x.shape
    L = PROMPT_CHUNK
    CH = PROMPT_STEP_CHUNKS
    nc = T // L
    n = B * nc
    spb = nc // CH
    assert nc % CH == 0 and CH % 2 == 0

    def next_chunk_spec(i):
        idx = lambda g: jnp.minimum(CH * g + i + 1, n - 1)
        return pl.BlockSpec((None, L, D), lambda g: (idx(g) // nc, idx(g) % nc, 0))

    state = jax.ShapeDtypeStruct((B, HEADS, DH, DH), F32)
    state_spec = pl.BlockSpec((None, HEADS, DH, DH), lambda g: (g // spb, 0, 0, 0))
    rows_spec = pl.BlockSpec((None, CH * L, D), lambda g: (g // spb, g % spb, 0))
    table_spec = pl.BlockSpec((CH * L, DH), lambda g: (g % spb, 0))
    steps = n // CH
    wup_spec = pl.BlockSpec((D // steps, D_FF), lambda g: (g, 0))
    wdn_spec = pl.BlockSpec((D_FF // steps, D), lambda g: (g, 0))
    return pl.pallas_call(
        functools.partial(_mix_prompt_kernel, nc),
        grid=(steps,),
        in_specs=[
            pl.BlockSpec((None, L, D), lambda g: (0, 0, 0), pipeline_mode=pl.Buffered(1)),
            *[next_chunk_spec(i) for i in range(CH)],
            rows_spec,
            _const_spec((1, D)),
            pl.BlockSpec(memory_space=pl.ANY),
            table_spec,
            table_spec,
            _const_spec((1, DH)),
            _const_spec((1, DH)),
            _const_spec(lbl.shape),
            pl.BlockSpec(memory_space=pl.ANY),
            wup_spec,
            wdn_spec,
        ],
        out_specs=[rows_spec, state_spec, state_spec,
                   pl.BlockSpec((D, IN_COLS), lambda g: (0, 0), pipeline_mode=pl.Buffered(1)),
                   pl.BlockSpec((D, D), lambda g: (0, 0), pipeline_mode=pl.Buffered(1)),
                   wup_spec, wdn_spec],
        out_shape=[jax.ShapeDtypeStruct((B, T, D), F32), state, state,
                   jax.ShapeDtypeStruct((D, IN_COLS), BF16), jax.ShapeDtypeStruct((D, D), BF16),
                   jax.ShapeDtypeStruct((D, D_FF), BF16), jax.ShapeDtypeStruct((D_FF, D), BF16)],
        scratch_shapes=[
            pltpu.VMEM((2, L, IN_COLS), F32),
            pltpu.VMEM((L, D), BF16),
            pltpu.VMEM((HEADS, DH, DH), F32),
            pltpu.VMEM((HEADS, DH, DH), F32),
            pltpu.VMEM((HEADS, L, L), F32),
            pltpu.VMEM((HEADS, L, DH), F32),
            pltpu.VMEM((HEADS, L, DH), F32),
            pltpu.VMEM((CAST_BUFFERS, CAST_ROWS, D), F32),
            pltpu.SemaphoreType.DMA((CAST_BUFFERS, IN_COLS // D)),
        ],
        compiler_params=pltpu.CompilerParams(
            dimension_semantics=("arbitrary",), vmem_limit_bytes=VMEM_LIMIT),
        name="mix_prompt",
    )(x, *([x] * CH), x, nw, win_f32, cosf, sinf, rnw, hnw, lbl, wout_f32, wup_f32, wdn_f32)


def _mix_sample_kernel(seg, x_ref, nw_ref, win_ref, cos_ref, sin_ref, rnw_ref, hnw_ref, lbl_ref,
                       wout_ref, sret_in_ref, shg_in_ref,
                       x1_ref, sret_ref, shg_ref,
                       proj_ref, o_ref, qh_ref, kh_ref, dec_ref, ob_ref):
    rows = x_ref.shape[0]
    nsub = sret_in_ref.shape[0]
    j = pl.program_id(1)

    @pl.when(j == 0)
    def _():
        _project(x_ref, nw_ref, win_ref, proj_ref)
        t = lax.broadcasted_iota(jnp.int32, (rows, DH), 0)
        tl = (t & (seg - 1)).astype(F32)
        cosf = cos_ref[...]
        sinf = sin_ref[...]
        lb = _lower_bound(lbl_ref[...])
        masks = _hgrn_level_masks(rows, seg)
        for hd in range(HEADS):
            lg = _log_gamma(hd)
            q = _rotary(proj_ref[:, _cols(0, hd)], cosf, sinf)
            k = _rotary(proj_ref[:, _cols(1, hd)], cosf, sinf)
            scores = _dot_nt(q.astype(BF16), k.astype(BF16)) * _ret_mask(rows, seg, lg)
            o_ref[:, _cols(0, hd)] = _dot(scores.astype(BF16),
                                          proj_ref[:, _cols(2, hd)].astype(BF16))
            qh_ref[:, _cols(0, hd)] = q * jnp.exp((tl + 1.0) * lg)
            kh_ref[:, _cols(0, hd)] = k * (jnp.exp((seg - 1.0 - tl) * lg) * QK_SCALE)
        for hd in range(HEADS):
            q = _silu(proj_ref[:, _cols(4, hd)]) * QK_SCALE
            lbh = lb[:, hd * DH:(hd + 1) * DH]
            f = lbh + (1.0 - lbh) * _sigmoid(proj_ref[:, _cols(5, hd)])
            k = 1.0 - f
            o, b = _hgrn_intra(q, k, f, proj_ref[:, _cols(6, hd)], seg, t, masks)
            blast = _block_row_bcast(b, seg, seg - 1)
            o_ref[:, _cols(1, hd)] = o
            qh_ref[:, _cols(1, hd)] = q * jnp.exp2(b)
            kh_ref[:, _cols(1, hd)] = k * jnp.exp2(blast - b)
            dec_ref[:, hd * DH:(hd + 1) * DH] = jnp.exp2(blast)

    def seq_body(s, carry):
        r = pl.ds(pl.multiple_of((j * nsub + s) * seg, seg), seg)
        qh = qh_ref[r, :].astype(BF16)
        kh = kh_ref[r, :].astype(BF16)
        v_ret = proj_ref[r, 2 * GROUP_W:3 * GROUP_W].astype(BF16)
        v_hg = proj_ref[r, 6 * GROUP_W:7 * GROUP_W].astype(BF16)
        dec = dec_ref[r, :][0:1, :]
        o_parts, new_ret, new_hg = [], [], []
        for hd in range(HEADS):
            st = sret_in_ref[s, hd]
            c = slice(hd * DH, (hd + 1) * DH)
            o_parts.append(_dot(qh[:, c], st.astype(BF16)))
            new_ret.append(st * math.exp(seg * _log_gamma(hd)) + _dot_tn(kh[:, c], v_ret[:, c]))
        for hd in range(HEADS):
            st = shg_in_ref[s, hd]
            c = slice(hd * DH, (hd + 1) * DH)
            c1 = slice(GROUP_W + hd * DH, GROUP_W + (hd + 1) * DH)
            o_parts.append(_dot(qh[:, c1], st.astype(BF16)))
            dcol = jnp.broadcast_to(dec[:, c], (DH, DH)).T
            new_hg.append(st * dcol + _dot_tn(kh[:, c1], v_hg[:, c]))
        o_ref[r, :] = o_ref[r, :] + jnp.concatenate(o_parts, axis=1)
        for hd in range(HEADS):
            sret_ref[s, hd] = new_ret[hd]
            shg_ref[s, hd] = new_hg[hd]
        return carry

    lax.fori_loop(0, nsub, seq_body, 0, unroll=2)

    @pl.when(j == pl.num_programs(1) - 1)
    def _():
        for hd in range(HEADS):
            c0 = _cols(0, hd)
            ob_ref[:, c0] = _head_out(o_ref[:, c0], rnw_ref[...],
                                      proj_ref[:, _cols(3, hd)]).astype(BF16)
            c1 = _cols(1, hd)
            ob_ref[:, c1] = _head_out(o_ref[:, c1], hnw_ref[...],
                                      proj_ref[:, _cols(7, hd)]).astype(BF16)
        x1_ref[...] = x_ref[...] + _dot(ob_ref[...], wout_ref[...])


def _mix_sample(x, nw, win, cosf, sinf, rnw, hnw, lbl, wout, sret, shg):
    nb, seg, D = x.shape
    rows = SAMPLE_ROWS
    nsub = SAMPLE_SUB_SEQS
    nj = rows // seg // nsub
    x2 = x.reshape(nb * seg, D)
    state = jax.ShapeDtypeStruct((nb, HEADS, DH, DH), F32)
    state_map = lambda i, j: (i * nj + j, 0, 0, 0)
    state_spec = pl.BlockSpec((nsub, HEADS, DH, DH), state_map)
    x1, sret_new, shg_new = pl.pallas_call(
        functools.partial(_mix_sample_kernel, seg),
        grid=(nb * seg // rows, nj),
        in_specs=[
            pl.BlockSpec((rows, D), lambda i, j: (i, 0)),
            _const_spec((1, D)),
            _const_spec((D, IN_COLS)),
            _const_spec((rows, DH)),
            _const_spec((rows, DH)),
            _const_spec((1, DH)),
            _const_spec((1, DH)),
            _const_spec(lbl.shape),
            _const_spec((D, D)),
            state_spec,
            state_spec,
        ],
        out_specs=[pl.BlockSpec((rows, D), lambda i, j: (i, 0)), state_spec, state_spec],
        out_shape=[jax.ShapeDtypeStruct((nb * seg, D), F32), state, state],
        scratch_shapes=[
            pltpu.VMEM((rows, IN_COLS), F32),
            pltpu.VMEM((rows, D), F32),
            pltpu.VMEM((rows, D), F32),
            pltpu.VMEM((rows, D), F32),
            pltpu.VMEM((rows, GROUP_W), F32),
            pltpu.VMEM((rows, D), BF16),
        ],
        compiler_params=pltpu.CompilerParams(
            dimension_semantics=("arbitrary", "arbitrary"), vmem_limit_bytes=SAMPLE_VMEM_LIMIT),
        name="mix_sample",
    )(x2, nw, win, cosf, sinf, rnw, hnw, lbl, wout, sret, shg)
    return x1, sret_new, shg_new


def _ffn_kernel(x_ref, nw_ref, wup_ref, wdn_ref, fw_ref, y_ref):
    subs = [slice(r * FFN_SUB_ROWS, (r + 1) * FFN_SUB_ROWS)
            for r in range(x_ref.shape[0] // FFN_SUB_ROWS)]
    hs = [_rms(x_ref[rs, :], nw_ref[...]).astype(BF16) for rs in subs]
    accs = [x_ref[rs, :] for rs in subs]
    step = D_MODEL
    for g in range(D_FF // step):
        us = [jnp.maximum(_dot(h, wup_ref[:, g * step:(g + 1) * step]), 0.0) for h in hs]
        accs = [acc + _dot((u * u).astype(BF16), wdn_ref[g * step:(g + 1) * step, :])
                for acc, u in zip(accs, us)]
    for rs, acc in zip(subs, accs):
        y_ref[rs, :] = _rms(acc, fw_ref[...])


def _ffn_rows_kernel(n_prompt_steps, xp_ref, xs_ref, nw_ref, wup_ref, wdn_ref, fw_ref,
                     yp_ref, ys_ref):
    i = pl.program_id(0)

    @pl.when(i < n_prompt_steps)
    def _():
        _ffn_kernel(xp_ref, nw_ref, wup_ref, wdn_ref, fw_ref, yp_ref)

    @pl.when(i >= n_prompt_steps)
    def _():
        _ffn_kernel(xs_ref, nw_ref, wup_ref, wdn_ref, fw_ref, ys_ref)


def _ffn(xp, xs, nw, wup, wdn, fw):
    n_p, D = xp.shape
    n_s = xs.shape[0]
    rows = FFN_ROWS
    assert n_p % rows == 0 and n_s % rows == 0
    p_steps, s_steps = n_p // rows, n_s // rows
    p_spec = pl.BlockSpec((rows, D), lambda i: (jnp.minimum(i, p_steps - 1), 0))
    s_spec = pl.BlockSpec((rows, D), lambda i: (jnp.maximum(i - p_steps, 0), 0),
                          pipeline_mode=pl.Buffered(1))
    return pl.pallas_call(
        functools.partial(_ffn_rows_kernel, p_steps),
        grid=(p_steps + s_steps,),
        in_specs=[
            p_spec,
            s_spec,
            _const_spec((1, D)),
            _const_spec((D, D_FF)),
            _const_spec((D_FF, D)),
            _const_spec((1, D)),
        ],
        out_specs=[p_spec, s_spec],
        out_shape=[jax.ShapeDtypeStruct((n_p, D), F32), jax.ShapeDtypeStruct((n_s, D), F32)],
        compiler_params=pltpu.CompilerParams(
            dimension_semantics=("arbitrary",), vmem_limit_bytes=VMEM_LIMIT),
        name="ffn",
    )(xp, xs, nw, wup, wdn, fw)


def _rope_tables(pos):
    half = DH // 2
    inv_freq = ROPE_BASE ** (-jnp.arange(half, dtype=F32) / half)
    ang = pos.astype(F32)[:, None] * inv_freq[None, :]
    cos, sin = jnp.cos(ang), jnp.sin(ang)
    return jnp.concatenate([cos, cos], axis=-1), jnp.concatenate([-sin, sin], axis=-1)


def kernel(x_prompt, x_sample, state_ret, state_hgrn, norm_mix_w, w_in, ret_norm_w, hgrn_norm_w,
           lb_logits, w_out, norm_ffn_w, w_up, w_down, final_norm_w):
    B, T, D = x_prompt.shape
    nb, seg, _ = x_sample.shape
    assert w_in.shape[0] == 1, "one layer"

    nw = norm_mix_w[0].reshape(1, D)
    rnw = ret_norm_w[0].reshape(1, DH)
    hnw = hgrn_norm_w[0].reshape(1, DH)
    lbl = lb_logits.astype(F32)
    fnw = norm_ffn_w[0].reshape(1, D)
    fw = final_norm_w.reshape(1, D)

    cos_p, sin_p = _rope_tables(jnp.arange(T, dtype=jnp.int32))
    cos_s, sin_s = _rope_tables(PAST_LEN + jnp.arange(seg, dtype=jnp.int32))
    cos_s = jnp.tile(cos_s, (SAMPLE_ROWS // seg, 1))
    sin_s = jnp.tile(sin_s, (SAMPLE_ROWS // seg, 1))

    xp1, sret_p, shg_p, win, wout, wup, wdn = _mix_prompt(
        x_prompt, nw, w_in[0], cos_p, sin_p, rnw, hnw, lbl, w_out[0], w_up[0], w_down[0])
    xs1, sret_s, shg_s = _mix_sample(x_sample, nw, win, cos_s, sin_s, rnw, hnw, lbl, wout,
                                     state_ret[0], state_hgrn[0])

    y_p, y_s = _ffn(xp1.reshape(B * T, D), xs1, fnw, wup, wdn, fw)
    y_p = y_p.reshape(B, T, D)
    y_s = y_s.reshape(nb, seg, D)
    return (y_p, y_s, sret_p[None], shg_p[None], sret_s[None], shg_s[None])
```
